```python
import math
import jax, jax.numpy as jnp
from jax import lax
import numpy as np

D_MODEL = 1024
BATCH = 8
SEQ = 2048
DEPTH = 4
DEC_BATCH = 128
DEC_SEQ = 4
PAST_LEN = 16384
PAGE_SIZE = 128

HEAD_DIM = D_MODEL // 16
A_HEADS = 6
B_HEADS = 6
C_HEADS = 4
A_WIDTH = A_HEADS * HEAD_DIM
B_WIDTH = B_HEADS * HEAD_DIM
C_WIDTH = C_HEADS * HEAD_DIM
MIX_WIDTH = A_WIDTH + B_WIDTH + C_WIDTH
IN_COLS = 2 * A_WIDTH + 2 * B_WIDTH + 3 * C_WIDTH
CHUNK = 128
CONV_B_WIDTH = 31
CONV_C_WIDTH = 3
N_GROUPS = 4
EXPERTS_PER_GROUP = 8
TOP_K_IN_GROUP = 2
D_EXPERT = D_MODEL // 4
LN_EPS = 1e-5
DEEPNORM_ALPHA = (2.0 * DEPTH) ** 0.25
DEEPNORM_BETA = (8.0 * DEPTH) ** -0.25

kernel_name = "hybrid_gmlp_conformer_shortconv_hmoe_step"


def layer_norm(x, g, b):
    xf = x.astype(jnp.float32)
    mu = jnp.mean(xf, axis=-1, keepdims=True)
    xc = xf - mu
    var = jnp.mean(xc * xc, axis=-1, keepdims=True)
    y = xc * lax.rsqrt(var + LN_EPS) * g.astype(jnp.float32) + b.astype(jnp.float32)
    return y.astype(x.dtype)


def causal_depthwise(xpad, w):
    c = w.shape[-1]
    return lax.conv_general_dilated(
        xpad, w[:, None, :].astype(xpad.dtype), window_strides=(1,), padding='VALID',
        dimension_numbers=('NWC', 'WIO', 'NWC'), feature_group_count=c)


def hier_moe(h, w_rg, w_re, w_gate, w_up, w_down):
    n, t, d = h.shape
    tok = h.reshape(n * t, d)
    g_logits = jnp.matmul(tok, w_rg).astype(jnp.float32)
    g_prob = jax.nn.softmax(g_logits, axis=-1)
    g_sel = jnp.argmax(g_logits, axis=-1)
    g_w = jnp.take_along_axis(g_prob, g_sel[:, None], axis=-1)
    e_logits = jnp.einsum('md,dge->mge', tok, w_re).astype(jnp.float32)
    e_sel = jnp.take_along_axis(e_logits, g_sel[:, None, None], axis=1)[:, 0]
    top_v, top_i = lax.top_k(e_sel, TOP_K_IN_GROUP)
    top_w = jax.nn.softmax(top_v, axis=-1) * g_w
    in_group = jnp.sum(jax.nn.one_hot(top_i, EXPERTS_PER_GROUP, dtype=jnp.float32)
                       * top_w[..., None], axis=1)
    gate = (jax.nn.one_hot(g_sel, N_GROUPS, dtype=jnp.float32)[:, :, None]
            * in_group[:, None, :]).astype(h.dtype)
    y = jnp.zeros_like(tok)
    for g in range(N_GROUPS):
        a = jnp.einsum('md,edf->mef', tok, w_gate[g])
        u = jnp.einsum('md,edf->mef', tok, w_up[g])
        act = jax.nn.silu(a) * u
        y = y + jnp.einsum('mef,me,efd->md', act, gate[:, g], w_down[g])
    return y.reshape(n, t, d)


def trunk_layer(h, conv_b_past, conv_c_past, prompt, w_in, w_s, b_s, ln_v_g, ln_v_b,
                conv_b_w, conv_b_bias, ln_conv_g, ln_conv_b, conv_c_w, w_out,
                ln1_g, ln1_b, w_rg, w_re, w_gate, w_up, w_down, ln2_g, ln2_b):
    n, t, _ = h.shape
    z = jnp.matmul(h, w_in)
    za = z[..., :2 * A_WIDTH]
    zb = z[..., 2 * A_WIDTH:2 * A_WIDTH + 2 * B_WIDTH]
    zc = z[..., 2 * A_WIDTH + 2 * B_WIDTH:]

    za = jax.nn.gelu(za, approximate=False)
    u, v = za[..., :A_WIDTH], za[..., A_WIDTH:]
    v = layer_norm(v, ln_v_g, ln_v_b)
    u4 = u.reshape(n, t, A_HEADS, HEAD_DIM)
    v4 = v.reshape(n, t, A_HEADS, HEAD_DIM)
    wm = w_s * jnp.tril(jnp.ones((CHUNK, CHUNK), dtype=w_s.dtype))
    if prompt:
        nc = t // CHUNK
        vc = v4.reshape(n, nc, CHUNK, A_HEADS, HEAD_DIM)
        mix = jnp.einsum('hts,bcshd->bcthd', wm, vc) + b_s.T[None, None, :, :, None]
        ya = (u4.reshape(n, nc, CHUNK, A_HEADS, HEAD_DIM) * mix).reshape(n, t, A_WIDTH)
    else:
        mix = jnp.einsum('hts,bshd->bthd', wm[:, :t, :t], v4) + b_s[:, :t].T[None, :, :, None]
        ya = (u4 * mix).reshape(n, t, A_WIDTH)

    glu = zb[..., :B_WIDTH] * jax.nn.sigmoid(zb[..., B_WIDTH:])
    xpad_b = jnp.concatenate([conv_b_past.astype(glu.dtype), glu], axis=1)
    cb = causal_depthwise(xpad_b, conv_b_w) + conv_b_bias
    yb = jax.nn.silu(layer_norm(cb, ln_conv_g, ln_conv_b))
    new_b = xpad_b[:, -(CONV_B_WIDTH - 1):]

    bg = zc[..., :C_WIDTH]
    cg = zc[..., C_WIDTH:2 * C_WIDTH]
    hc = zc[..., 2 * C_WIDTH:]
    xpad_c = jnp.concatenate([conv_c_past.astype(hc.dtype), cg * hc], axis=1)
    yc = bg * causal_depthwise(xpad_c, conv_c_w)
    new_c = xpad_c[:, -(CONV_C_WIDTH - 1):]

    mix_out = jnp.matmul(jnp.concatenate([ya, yb, yc], axis=-1), w_out)
    h = layer_norm(DEEPNORM_ALPHA * h + mix_out, ln1_g, ln1_b)
    h = layer_norm(DEEPNORM_ALPHA * h + hier_moe(h, w_rg, w_re, w_gate, w_up, w_down), ln2_g, ln2_b)
    return h, new_b, new_c, v


def setup_inputs(seed: int = 0) -> dict:
    key = jax.random.key(seed)
    ks = jax.random.split(key, 26)
    f32 = jnp.float32
    nrm = lambda k, s, sc: jax.random.normal(k, s, f32) * sc
    L = DEPTH
    return {
        "x_prompt": nrm(ks[0], (BATCH, SEQ, D_MODEL), 1.0),
        "x_sample": nrm(ks[1], (DEC_BATCH, DEC_SEQ, D_MODEL), 1.0),
        "state_conv_b": nrm(ks[2], (L, DEC_BATCH, CONV_B_WIDTH - 1, B_WIDTH), 0.5),
        "state_conv_c": nrm(ks[3], (L, DEC_BATCH, CONV_C_WIDTH - 1, C_WIDTH), 1.0),
        "ln_in_g": 1.0 + nrm(ks[4], (D_MODEL,), 0.02),
        "ln_in_b": nrm(ks[5], (D_MODEL,), 0.02),
        "w_in": nrm(ks[6], (L, D_MODEL, IN_COLS), D_MODEL ** -0.5),
        "w_s": nrm(ks[7], (L, A_HEADS, CHUNK, CHUNK), CHUNK ** -0.5),
        "b_s": 1.0 + nrm(ks[8], (L, A_HEADS, CHUNK), 0.1),
        "ln_v_g": 1.0 + nrm(ks[9], (L, A_WIDTH), 0.02),
        "ln_v_b": nrm(ks[10], (L, A_WIDTH), 0.02),
        "conv_b_w": nrm(ks[11], (L, CONV_B_WIDTH, B_WIDTH), CONV_B_WIDTH ** -0.5),
        "conv_b_bias": nrm(ks[12], (L, B_WIDTH), 0.01),
        "ln_conv_g": 1.0 + nrm(ks[13], (L, B_WIDTH), 0.02),
        "ln_conv_b": nrm(ks[14], (L, B_WIDTH), 0.02),
        "conv_c_w": nrm(ks[15], (L, CONV_C_WIDTH, C_WIDTH), CONV_C_WIDTH ** -0.5),
        "w_out": nrm(ks[16], (L, MIX_WIDTH, D_MODEL), DEEPNORM_BETA * MIX_WIDTH ** -0.5),
        "ln1_g": 1.0 + nrm(ks[17], (L, D_MODEL), 0.02),
        "ln1_b": nrm(ks[18], (L, D_MODEL), 0.02),
        "w_router_group": nrm(ks[19], (L, D_MODEL, N_GROUPS), D_MODEL ** -0.5),
        "w_router_expert": nrm(ks[20], (L, D_MODEL, N_GROUPS, EXPERTS_PER_GROUP), D_MODEL ** -0.5),
        "w_gate": nrm(ks[21], (L, N_GROUPS, EXPERTS_PER_GROUP, D_MODEL, D_EXPERT), D_MODEL ** -0.5),
        "w_up": nrm(ks[22], (L, N_GROUPS, EXPERTS_PER_GROUP, D_MODEL, D_EXPERT), D_MODEL ** -0.5),
        "w_down": nrm(ks[23], (L, N_GROUPS, EXPERTS_PER_GROUP, D_EXPERT, D_MODEL), DEEPNORM_BETA * D_EXPERT ** -0.5),
        "ln2_g": 1.0 + nrm(ks[24], (L, D_MODEL), 0.02),
        "ln2_b": nrm(ks[25], (L, D_MODEL), 0.02),
    }


def reference(x_prompt, x_sample, state_conv_b, state_conv_c, ln_in_g, ln_in_b, w_in, w_s, b_s,
              ln_v_g, ln_v_b, conv_b_w, conv_b_bias, ln_conv_g, ln_conv_b, conv_c_w, w_out,
              ln1_g, ln1_b, w_router_group, w_router_expert, w_gate, w_up, w_down, ln2_g, ln2_b):
    assert SEQ % CHUNK == 0 and DEC_SEQ <= CHUNK and PAST_LEN % CHUNK == 0
    hp = layer_norm(x_prompt, ln_in_g, ln_in_b)
    hs = layer_norm(x_sample, ln_in_g, ln_in_b)
    nb = x_prompt.shape[0]
    cb_p, cc_p, cb_s, cc_s, v_s = [], [], [], [], []
    for l in range(DEPTH):
        lw = (w_in[l], w_s[l], b_s[l], ln_v_g[l], ln_v_b[l], conv_b_w[l], conv_b_bias[l],
              ln_conv_g[l], ln_conv_b[l], conv_c_w[l], w_out[l], ln1_g[l], ln1_b[l],
              w_router_group[l], w_router_expert[l], w_gate[l], w_up[l], w_down[l],
              ln2_g[l], ln2_b[l])
        zb = jnp.zeros((nb, CONV_B_WIDTH - 1, B_WIDTH), hp.dtype)
        zc = jnp.zeros((nb, CONV_C_WIDTH - 1, C_WIDTH), hp.dtype)
        hp, nbp, ncp, _ = trunk_layer(hp, zb, zc, True, *lw)
        hs, nbs, ncs, vs = trunk_layer(hs, state_conv_b[l], state_conv_c[l], False, *lw)
        cb_p.append(nbp)
        cc_p.append(ncp)
        cb_s.append(nbs)
        cc_s.append(ncs)
        v_s.append(vs)
    conv_b_prompt = jnp.stack(cb_p)
    conv_c_prompt = jnp.stack(cc_p)
    conv_b_sample = jnp.stack(cb_s)
    conv_c_sample = jnp.stack(cc_s)
    chunk_v_sample = jnp.stack(v_s)
    return (hp, hs, conv_b_prompt, conv_c_prompt, conv_b_sample, conv_c_sample, chunk_v_sample)
```

```python
import functools
import math

import jax
import jax.numpy as jnp
from jax import lax
from jax.experimental import pallas as pl
from jax.experimental.pallas import tpu as pltpu

F32 = jnp.float32
BF16 = jnp.bfloat16

HEAD_DIM = 64
A_WIDTH = 384
B_WIDTH = 384
C_WIDTH = 256
GMLP_CHUNK = 128
CONV_B_TAPS = 31
CONV_C_TAPS = 3
N_GROUPS = 4
EXPERTS_PER_GROUP = 8
N_EXPERTS = N_GROUPS * EXPERTS_PER_GROUP
LN_EPS = 1e-5
INV_SQRT2 = 1.0 / math.sqrt(2.0)

LANES = 128
SUBLANES = 8
WIN = 256
CH = 16
WIN_ROWS = -(-(2 * WIN + N_EXPERTS * (CH - 1)) // WIN) * WIN
WIN_CHUNKS = WIN_ROWS // CH
TILE_CHUNKS = 32
TILE_ROWS = TILE_CHUNKS * CH
SUB_ROWS = 256
TM = 512
CONV_RB = 16
VMEM_LIMIT = 56 * 1024 * 1024


def _ln(x, g, b):
    mu = jnp.mean(x, axis=-1, keepdims=True)
    xc = x - mu
    var = jnp.mean(xc * xc, axis=-1, keepdims=True)
    return xc * lax.rsqrt(var + LN_EPS) * g + b


def _gelu(x):
    return 0.5 * x * (1.0 + lax.erf(x * INV_SQRT2))


def _silu(x):
    return x * jax.nn.sigmoid(x)


def _dot(a, b):
    return jnp.dot(a, b, preferred_element_type=F32)


def _iota(shape, dim):
    return lax.broadcasted_iota(jnp.int32, shape, dim)


def _route_window(logits, hb):
    w = logits.shape[0]
    lane = _iota((w, LANES), 1)
    neg = jnp.float32(-jnp.inf)
    gmask = lane < N_GROUPS
    gl = jnp.where(gmask, logits, neg)
    gmax = jnp.max(gl, axis=-1, keepdims=True)
    gsel = jnp.min(jnp.where(gl == gmax, lane, LANES), axis=-1, keepdims=True)
    den = jnp.sum(jnp.where(gmask, jnp.exp(gl - gmax), 0.0), axis=-1, keepdims=True)
    gw = 1.0 / den
    lo = N_GROUPS + EXPERTS_PER_GROUP * gsel
    el = jnp.where((lane >= lo) & (lane < lo + EXPERTS_PER_GROUP), logits, neg)
    v1 = jnp.max(el, axis=-1, keepdims=True)
    i1 = jnp.min(jnp.where(el == v1, lane, LANES), axis=-1, keepdims=True)
    el2 = jnp.where(lane == i1, neg, el)
    v2 = jnp.max(el2, axis=-1, keepdims=True)
    i2 = jnp.min(jnp.where(el2 == v2, lane, LANES), axis=-1, keepdims=True)
    e2x = jnp.exp(v2 - v1)
    w1 = gw / (1.0 + e2x)
    w2 = gw * e2x / (1.0 + e2x)
    e1 = i1 - N_GROUPS
    e2 = i2 - N_GROUPS

    oh = (lane == e1) | (lane == e2)
    ohf = jnp.where(oh, 1.0, 0.0)
    lstrict = jnp.where(_iota((w, w), 0) > _iota((w, w), 1), 1.0, 0.0).astype(BF16)
    rank = _dot(lstrict, ohf.astype(BF16))
    cnt = jnp.sum(ohf, axis=0, keepdims=True)
    nch = jnp.floor((cnt + (CH - 1.0)) * (1.0 / CH))
    upper = jnp.where(_iota((LANES, LANES), 0) < _iota((LANES, LANES), 1), 1.0, 0.0).astype(BF16)
    off = _dot(jnp.broadcast_to(nch, (SUBLANES, LANES)).astype(BF16), upper)[0:1]
    dest = off * CH + rank
    r1 = jnp.sum(jnp.where(lane == e1, dest, 0.0), axis=-1, keepdims=True)
    r2 = jnp.sum(jnp.where(lane == e2, dest, 0.0), axis=-1, keepdims=True)
    info = jnp.where(lane == 0, r1, jnp.where(lane == 1, r2,
                     jnp.where(lane == 2, w1, jnp.where(lane == 3, w2, 0.0))))
    info_t = info.T
    rows = _iota((WIN_ROWS, w), 0).astype(F32)
    perm = jnp.where((rows == info_t[0:1, :]) | (rows == info_t[1:2, :]), 1.0, 0.0).astype(BF16)
    xs = _dot(perm, hb).astype(BF16)
    sub = _iota((SUBLANES, LANES), 0)
    tab = jnp.where(sub == 0, nch, jnp.where(sub == 1, off, 0.0)).astype(jnp.int32)
    return xs, info, tab


def _post_mix(h, mixin, w_out_ref, vecd_ref, wr_ref, alpha, h1_ref, xs_ref, info_ref, tab_ref):
    mix_out = _dot(mixin, w_out_ref[...])
    h1 = _ln(alpha * h + mix_out, vecd_ref[0:1, :], vecd_ref[1:2, :])
    h1_ref[...] = h1
    hb = h1.astype(BF16)
    logits = _dot(hb, wr_ref[...])
    for wi in range(h.shape[0] // WIN):
        rs = slice(wi * WIN, (wi + 1) * WIN)
        xs, info, tab = _route_window(logits[rs], hb[rs])
        xs_ref[wi] = xs
        info_ref[wi] = info
        tab_ref[wi] = tab


def _mixer_prompt_body(first, alpha, h_ref, w_in_ref, ws_ref, bs_ref, veca_ref, cbw_ref, ccw_ref,
                       w_out_ref, vecd_ref, wr_ref,
                       h1_ref, xs_ref, info_ref, tab_ref, nsb_ref, nsc_ref,
                       xsh, cbuf, mixin):
    j = pl.program_id(1)
    last = pl.num_programs(1) - 1
    h = h_ref[0]
    if first:
        h = _ln(h, vecd_ref[4:5, :], vecd_ref[5:6, :])
    hb = h.astype(BF16)
    past_b = CONV_B_TAPS - 1
    past_c = CONV_C_TAPS - 1
    b0 = 32 - past_b
    c0 = 8 - past_c

    @pl.when(j == 0)
    def _():
        xsh[0, 0:32, :] = jnp.zeros((32, B_WIDTH), F32)
        cbuf[0:8, :] = jnp.zeros((8, C_WIDTH), F32)

    u = _gelu(_dot(hb, w_in_ref[:, 0:A_WIDTH]))
    v = _ln(_gelu(_dot(hb, w_in_ref[:, A_WIDTH:2 * A_WIDTH])), veca_ref[0:1, :], veca_ref[1:2, :])
    vb = v.astype(BF16)
    tri = (_iota((2 * GMLP_CHUNK, GMLP_CHUNK), 0) % GMLP_CHUNK) >= _iota((2 * GMLP_CHUNK, GMLP_CHUNK), 1)
    wms = [jnp.where(tri, ws_ref[p], 0.0).astype(BF16) for p in range(A_WIDTH // LANES)]
    lane = _iota((GMLP_CHUNK, LANES), 1)
    for c in range(TM // GMLP_CHUNK):
        rs = slice(c * GMLP_CHUNK, (c + 1) * GMLP_CHUNK)
        parts = []
        for p in range(A_WIDTH // LANES):
            ab = _dot(wms[p], vb[rs, p * LANES:(p + 1) * LANES])
            parts.append(jnp.where(lane < HEAD_DIM, ab[:GMLP_CHUNK], ab[GMLP_CHUNK:]))
        mix = jnp.concatenate(parts, axis=1) + bs_ref[...]
        mixin[rs, 0:A_WIDTH] = (u[rs] * mix).astype(BF16)

    o = 2 * A_WIDTH
    glu = _dot(hb, w_in_ref[:, o:o + B_WIDTH]) * jax.nn.sigmoid(_dot(hb, w_in_ref[:, o + B_WIDTH:o + 2 * B_WIDTH]))
    xsh[0, 32:32 + TM, :] = glu
    for r in range(1, SUBLANES):
        xsh[r, 0:TM + 24, :] = xsh[0, r:r + TM + 24, :]

    def conv_blk(rb, carry):
        base = pl.multiple_of(rb * CONV_RB, CONV_RB)
        acc = jnp.broadcast_to(veca_ref[2:3, :], (CONV_RB, B_WIDTH))
        for k in range(CONV_B_TAPS):
            s = k + b0
            acc = acc + cbw_ref[k:k + 1, :] * xsh[s % SUBLANES, pl.ds(base + (s // SUBLANES) * SUBLANES, CONV_RB), :]
        yb = _silu(_ln(acc, veca_ref[3:4, :], veca_ref[4:5, :]))
        mixin[pl.ds(base, CONV_RB), A_WIDTH:A_WIDTH + B_WIDTH] = yb.astype(BF16)
        return carry

    lax.fori_loop(0, TM // CONV_RB, conv_blk, 0)
    new_b = xsh[0, TM + b0:TM + 32, :]
    xsh[0, b0:32, :] = new_b

    @pl.when(j == last)
    def _():
        nsb_ref[0] = new_b

    o = 2 * A_WIDTH + 2 * B_WIDTH
    bg = _dot(hb, w_in_ref[:, o:o + C_WIDTH])
    cbuf[8:8 + TM, :] = _dot(hb, w_in_ref[:, o + C_WIDTH:o + 2 * C_WIDTH]) * _dot(hb, w_in_ref[:, o + 2 * C_WIDTH:o + 3 * C_WIDTH])
    cc = ccw_ref[0:1, :] * cbuf[c0:c0 + TM, :]
    for k in range(1, CONV_C_TAPS):
        cc = cc + ccw_ref[k:k + 1, :] * cbuf[c0 + k:c0 + k + TM, :]
    mixin[:, A_WIDTH + B_WIDTH:] = (bg * cc).astype(BF16)
    new_c = cbuf[TM + c0:TM + 8, :]
    cbuf[c0:8, :] = new_c

    @pl.when(j == last)
    def _():
        nsc_ref[0] = new_c

    _post_mix(h, mixin[...], w_out_ref, vecd_ref, wr_ref, alpha, h1_ref.at[0], xs_ref, info_ref, tab_ref)


def _mixer_prompt(h, lw, first, alpha, n_win_total):
    nb, t, d = h.shape
    nj = t // TM
    wpt = TM // WIN
    const = lambda shape: pl.BlockSpec(shape, lambda n, j: (0,) * len(shape))
    out_shape = (
        jax.ShapeDtypeStruct((nb, t, d), F32),
        jax.ShapeDtypeStruct((n_win_total, WIN_ROWS, d), BF16),
        jax.ShapeDtypeStruct((n_win_total, WIN, LANES), F32),
        jax.ShapeDtypeStruct((n_win_total, SUBLANES, LANES), jnp.int32),
        jax.ShapeDtypeStruct((nb, CONV_B_TAPS - 1, B_WIDTH), F32),
        jax.ShapeDtypeStruct((nb, CONV_C_TAPS - 1, C_WIDTH), F32),
    )
    win_map = lambda n, j: (n * nj + j, 0, 0)
    return pl.pallas_call(
        functools.partial(_mixer_prompt_body, first, alpha),
        grid=(nb, nj),
        in_specs=[
            pl.BlockSpec((1, TM, d), lambda n, j: (n, j, 0)),
            const(lw["w_in"].shape), const(lw["ws2"].shape), const(lw["bs_exp"].shape),
            const(lw["veca"].shape), const(lw["cbw"].shape), const(lw["ccw"].shape),
            const(lw["w_out"].shape), const(lw["vecd"].shape), const(lw["wr"].shape),
        ],
        out_specs=(
            pl.BlockSpec((1, TM, d), lambda n, j: (n, j, 0)),
            pl.BlockSpec((wpt, WIN_ROWS, d), win_map),
            pl.BlockSpec((wpt, WIN, LANES), win_map),
            pl.BlockSpec((wpt, SUBLANES, LANES), win_map),
            pl.BlockSpec((1, CONV_B_TAPS - 1, B_WIDTH), lambda n, j: (n, 0, 0)),
            pl.BlockSpec((1, CONV_C_TAPS - 1, C_WIDTH), lambda n, j: (n, 0, 0)),
        ),
        out_shape=out_shape,
        scratch_shapes=[
            pltpu.VMEM((SUBLANES, TM + 32, B_WIDTH), F32),
            pltpu.VMEM((TM + 8, C_WIDTH), F32),
            pltpu.VMEM((TM, d), BF16),
        ],
        compiler_params=pltpu.CompilerParams(
            dimension_semantics=("arbitrary", "arbitrary"), vmem_limit_bytes=VMEM_LIMIT),
        name="mixer_prompt",
    )(h, lw["w_in"], lw["ws2"], lw["bs_exp"], lw["veca"], lw["cbw"], lw["ccw"],
      lw["w_out"], lw["vecd"], lw["wr"])


def _mixer_sample_body(first, alpha, n_seq, n_t,
                       h_ref, sb_ref, sc_ref, w_in_ref, wexp_ref, bs_ref, veca_ref, cbw_ref, ccw_ref,
                       w_out_ref, vecd_ref, wr_ref,
                       h1_ref, v_ref, nsb_ref, nsc_ref, xs_ref, info_ref, tab_ref,
                       mixin):
    h = h_ref[...]
    if first:
        h = _ln(h, vecd_ref[4:5, :], vecd_ref[5:6, :])
    hb = h.astype(BF16)
    rows = lambda t: slice(t * n_seq, (t + 1) * n_seq)

    u = _gelu(_dot(hb, w_in_ref[:, 0:A_WIDTH]))
    v = _ln(_gelu(_dot(hb, w_in_ref[:, A_WIDTH:2 * A_WIDTH])), veca_ref[0:1, :], veca_ref[1:2, :])
    v_ref[...] = v
    for t in range(n_t):
        mix = jnp.broadcast_to(bs_ref[t:t + 1, :], (n_seq, A_WIDTH))
        for s in range(t + 1):
            mix = mix + wexp_ref[t, s:s + 1, :] * v[rows(s)]
        mixin[rows(t), 0:A_WIDTH] = (u[rows(t)] * mix).astype(BF16)

    o = 2 * A_WIDTH
    glu = _dot(hb, w_in_ref[:, o:o + B_WIDTH]) * jax.nn.sigmoid(_dot(hb, w_in_ref[:, o + B_WIDTH:o + 2 * B_WIDTH]))
    past_b = CONV_B_TAPS - 1
    xp = lambda m: sb_ref[m] if m < past_b else glu[rows(m - past_b)]
    for t in range(n_t):
        acc = jnp.broadcast_to(veca_ref[2:3, :], (n_seq, B_WIDTH))
        for k in range(CONV_B_TAPS):
            acc = acc + cbw_ref[k:k + 1, :] * xp(t + k)
        yb = _silu(_ln(acc, veca_ref[3:4, :], veca_ref[4:5, :]))
        mixin[rows(t), A_WIDTH:A_WIDTH + B_WIDTH] = yb.astype(BF16)
    for r in range(past_b):
        nsb_ref[r] = xp(r + n_t)

    o = 2 * A_WIDTH + 2 * B_WIDTH
    bg = _dot(hb, w_in_ref[:, o:o + C_WIDTH])
    xc = _dot(hb, w_in_ref[:, o + C_WIDTH:o + 2 * C_WIDTH]) * _dot(hb, w_in_ref[:, o + 2 * C_WIDTH:o + 3 * C_WIDTH])
    past_c = CONV_C_TAPS - 1
    xq = lambda m: sc_ref[m] if m < past_c else xc[rows(m - past_c)]
    for t in range(n_t):
        cc = ccw_ref[0:1, :] * xq(t)
        for k in range(1, CONV_C_TAPS):
            cc = cc + ccw_ref[k:k + 1, :] * xq(t + k)
        mixin[rows(t), A_WIDTH + B_WIDTH:] = (bg[rows(t)] * cc).astype(BF16)
    for r in range(past_c):
        nsc_ref[r] = xq(r + n_t)

    _post_mix(h, mixin[...], w_out_ref, vecd_ref, wr_ref, alpha, h1_ref, xs_ref, info_ref, tab_ref)


def _mixer_sample(h, sb_t, sc_t, lw, first, alpha):
    m, d = h.shape
    n_seq = sb_t.shape[1]
    n_t = m // n_seq
    nw = m // WIN
    full = lambda a: pl.BlockSpec(a.shape, lambda i: (0,) * a.ndim)
    ins = [h, sb_t, sc_t, lw["w_in"], lw["wexp"], lw["bs_exp"], lw["veca"], lw["cbw"], lw["ccw"],
           lw["w_out"], lw["vecd"], lw["wr"]]
    out_shape = (
        jax.ShapeDtypeStruct((m, d), F32),
        jax.ShapeDtypeStruct((m, A_WIDTH), F32),
        jax.ShapeDtypeStruct(sb_t.shape, F32),
        jax.ShapeDtypeStruct(sc_t.shape, F32),
        jax.ShapeDtypeStruct((nw, WIN_ROWS, d), BF16),
        jax.ShapeDtypeStruct((nw, WIN, LANES), F32),
        jax.ShapeDtypeStruct((nw, SUBLANES, LANES), jnp.int32),
    )
    return pl.pallas_call(
        functools.partial(_mixer_sample_body, first, alpha, n_seq, n_t),
        grid=(1,),
        in_specs=[full(a) for a in ins],
        out_specs=tuple(full(o) for o in out_shape),
        out_shape=out_shape,
        scratch_shapes=[pltpu.VMEM((m, d), BF16)],
        compiler_params=pltpu.CompilerParams(
            dimension_semantics=("arbitrary",), vmem_limit_bytes=VMEM_LIMIT),
        name="mixer_sample",
    )(*ins)


def _experts_body(tile_e_ref, nvalid_ref, clist_ref, ntiles_ref,
                  xp_hbm, xq_hbm, wg_ref, wu_ref, wd_ref, yp_hbm, yq_hbm,
                  lhs, obuf, sem_in, sem_out):
    del tile_e_ref
    t = pl.program_id(0)
    nt = ntiles_ref[0]
    n_p = xp_hbm.shape[0]

    def copy_in(src, slot, j):
        return pltpu.make_async_copy(src, lhs.at[slot, pl.ds(j * CH, CH)], sem_in.at[slot])

    def copy_out(dst, slot, j):
        return pltpu.make_async_copy(obuf.at[slot, pl.ds(j * CH, CH)], dst, sem_out.at[slot])

    def for_chunks(tt, fn):
        def body(j, carry):
            fn(j, clist_ref[tt * TILE_CHUNKS + j])
            return carry
        lax.fori_loop(0, nvalid_ref[tt], body, 0)

    def start_in(tt, slot):
        def fn(j, cid):
            @pl.when(cid < n_p)
            def _():
                copy_in(xp_hbm.at[cid], slot, j).start()

            @pl.when(cid >= n_p)
            def _():
                copy_in(xq_hbm.at[cid - n_p], slot, j).start()
        for_chunks(tt, fn)

    def start_out(tt, slot):
        def fn(j, cid):
            @pl.when(cid < n_p)
            def _():
                copy_out(yp_hbm.at[cid], slot, j).start()

            @pl.when(cid >= n_p)
            def _():
                copy_out(yq_hbm.at[cid - n_p], slot, j).start()
        for_chunks(tt, fn)

    def wait_in(tt, slot):
        for_chunks(tt, lambda j, cid: copy_in(xp_hbm.at[0], slot, j).wait())

    def wait_out(tt, slot):
        for_chunks(tt, lambda j, cid: copy_out(yp_hbm.at[0], slot, j).wait())

    @pl.when(t == 0)
    def _():
        lhs[...] = jnp.zeros(lhs.shape, lhs.dtype)
        start_in(0, 0)

    @pl.when(t + 1 < nt)
    def _():
        start_in(t + 1, (t + 1) % 2)

    @pl.when(t < nt)
    def _():
        slot = t % 2
        wait_in(t, slot)

        @pl.when(t >= 2)
        def _():
            wait_out(t - 2, slot)

        for sb in range(TILE_ROWS // SUB_ROWS):
            @pl.when(sb * (SUB_ROWS // CH) < nvalid_ref[t])
            def _():
                x = lhs[slot, sb * SUB_ROWS:(sb + 1) * SUB_ROWS, :]
                act = _silu(_dot(x, wg_ref[0])) * _dot(x, wu_ref[0])
                obuf[slot, sb * SUB_ROWS:(sb + 1) * SUB_ROWS, :] = _dot(act.astype(BF16), wd_ref[0]).astype(BF16)

        start_out(t, slot)

    @pl.when(t == pl.num_programs(0) - 1)
    def _():
        for back in (2, 1):
            @pl.when(nt >= back)
            def _():
                wait_out(nt - back, (nt - back) % 2)


def _experts(xs_p, xs_s, tile_e, nvalid, clist, ntiles, wg, wu, wd, t_max):
    d = xs_p.shape[-1]
    chunks = lambda a: a.reshape(a.shape[0] * WIN_CHUNKS, CH, d)
    xp, xq = chunks(xs_p), chunks(xs_s)
    de = wg.shape[-1]
    anyspec = pl.BlockSpec(memory_space=pl.ANY)
    grid_spec = pltpu.PrefetchScalarGridSpec(
        num_scalar_prefetch=4,
        grid=(t_max,),
        in_specs=[
            anyspec, anyspec,
            pl.BlockSpec((1, d, de), lambda t, te, nv, cl, n: (te[t], 0, 0)),
            pl.BlockSpec((1, d, de), lambda t, te, nv, cl, n: (te[t], 0, 0)),
            pl.BlockSpec((1, de, d), lambda t, te, nv, cl, n: (te[t], 0, 0)),
        ],
        out_specs=(anyspec, anyspec),
        scratch_shapes=[
            pltpu.VMEM((2, TILE_ROWS, d), BF16),
            pltpu.VMEM((2, TILE_ROWS, d), BF16),
            pltpu.SemaphoreType.DMA((2,)),
            pltpu.SemaphoreType.DMA((2,)),
        ],
    )
    yp, yq = pl.pallas_call(
        _experts_body,
        grid_spec=grid_spec,
        out_shape=(jax.ShapeDtypeStruct(xp.shape, xp.dtype), jax.ShapeDtypeStruct(xq.shape, xq.dtype)),
        input_output_aliases={4: 0, 5: 1},
        compiler_params=pltpu.CompilerParams(
            dimension_semantics=("arbitrary",), vmem_limit_bytes=VMEM_LIMIT),
        name="experts",
    )(tile_e, nvalid, clist, ntiles, xp, xq, wg, wu, wd)
    return yp.reshape(xs_p.shape), yq.reshape(xs_s.shape)


def _expert_tables(tab, t_max):
    nch = tab[:, 0, :N_EXPERTS]
    off = tab[:, 1, :N_EXPERTS]
    n_win = nch.shape[0]
    cum_incl = jnp.cumsum(nch, axis=0)
    cum_excl = cum_incl - nch
    ce = cum_incl[-1]
    te = (ce + TILE_CHUNKS - 1) // TILE_CHUNKS
    tile_end = jnp.cumsum(te)
    tile_start = tile_end - te
    nt = tile_end[-1]
    t_idx = jnp.arange(t_max, dtype=jnp.int32)
    tile_e = jnp.minimum(jnp.sum(tile_end[None, :] <= t_idx[:, None], axis=1), N_EXPERTS - 1).astype(jnp.int32)
    q0 = (t_idx - tile_start[tile_e]) * TILE_CHUNKS
    nvalid = jnp.where(t_idx < nt, jnp.clip(ce[tile_e] - q0, 0, TILE_CHUNKS), 0).astype(jnp.int32)
    q = q0[:, None] + jnp.arange(TILE_CHUNKS, dtype=jnp.int32)[None, :]
    cum_e = cum_incl.T[tile_e]
    w = jnp.minimum(jnp.sum(cum_e[:, None, :] <= q[:, :, None], axis=-1), n_win - 1)
    flat = w * N_EXPERTS + tile_e[:, None]
    cid = w * WIN_CHUNKS + jnp.take(off.reshape(-1), flat) + (q - jnp.take(cum_excl.reshape(-1), flat))
    valid = jnp.arange(TILE_CHUNKS, dtype=jnp.int32)[None, :] < nvalid[:, None]
    clist = jnp.where(valid, cid, 0).astype(jnp.int32).reshape(-1)
    return tile_e, nvalid, clist, nt.astype(jnp.int32).reshape(1)


def _combine_body(alpha, h1_ref, info_ref, ys_ref, vecd_ref, out_ref):
    for wi in range(info_ref.shape[0]):
        info = info_ref[wi]
        col = _iota((WIN, WIN_ROWS), 1).astype(F32)
        pt = (jnp.where(col == info[:, 0:1], info[:, 2:3], 0.0)
              + jnp.where(col == info[:, 1:2], info[:, 3:4], 0.0)).astype(BF16)
        y = _dot(pt, ys_ref[wi])
        rs = slice(wi * WIN, (wi + 1) * WIN)
        out_ref[rs, :] = _ln(alpha * h1_ref[rs, :] + y, vecd_ref[2:3, :], vecd_ref[3:4, :])


def _combine(h1, info, ys, vecd, alpha, wpb):
    m, d = h1.shape
    win_map = lambda i: (i, 0, 0)
    return pl.pallas_call(
        functools.partial(_combine_body, alpha),
        grid=(m // (wpb * WIN),),
        in_specs=[
            pl.BlockSpec((wpb * WIN, d), lambda i: (i, 0)),
            pl.BlockSpec((wpb, WIN, LANES), win_map),
            pl.BlockSpec((wpb, WIN_ROWS, d), win_map),
            pl.BlockSpec(vecd.shape, lambda i: (0, 0)),
        ],
        out_specs=pl.BlockSpec((wpb * WIN, d), lambda i: (i, 0)),
        out_shape=jax.ShapeDtypeStruct((m, d), F32),
        compiler_params=pltpu.CompilerParams(
            dimension_semantics=("arbitrary",), vmem_limit_bytes=VMEM_LIMIT),
        name="combine",
    )(h1, info, ys, vecd)


def kernel(x_prompt, x_sample, state_conv_b, state_conv_c, ln_in_g, ln_in_b, w_in, w_s, b_s, ln_v_g, ln_v_b, conv_b_w, conv_b_bias, ln_conv_g, ln_conv_b, conv_c_w, w_out, ln1_g, ln1_b, w_router_group, w_router_expert, w_gate, w_up, w_down, ln2_g, ln2_b):
    depth = w_in.shape[0]
    nb, t, d = x_prompt.shape
    ns, nt_s, _ = x_sample.shape
    assert t % TM == 0 and (ns * nt_s) % WIN == 0 and TM % WIN == 0 and d % LANES == 0
    alpha = (2.0 * depth) ** 0.25
    n_win_p = nb * t // WIN
    n_win_s = ns * nt_s // WIN
    n_win = n_win_p + n_win_s
    max_chunks = n_win * (2 * WIN // CH + N_EXPERTS)
    t_max = max_chunks // TILE_CHUNKS + N_EXPERTS

    hp = x_prompt
    hs = jnp.transpose(x_sample, (1, 0, 2)).reshape(nt_s * ns, d)
    sb_t = jnp.transpose(state_conv_b, (0, 2, 1, 3))
    sc_t = jnp.transpose(state_conv_c, (0, 2, 1, 3))
    cb_p, cc_p, cb_s, cc_s, v_s = [], [], [], [], []
    for l in range(depth):
        wr = jnp.concatenate([w_router_group[l], w_router_expert[l].reshape(d, N_EXPERTS)], axis=1)
        wr = jnp.pad(wr, ((0, 0), (0, LANES - wr.shape[1]))).astype(BF16)
        zeros_a = jnp.zeros((A_WIDTH,), F32)
        zeros_d = jnp.zeros((d,), F32)
        lw = dict(
            w_in=w_in[l].astype(BF16),
            ws2=w_s[l].reshape(A_WIDTH // LANES, 2 * GMLP_CHUNK, GMLP_CHUNK),
            wexp=jnp.repeat(jnp.transpose(w_s[l][:, :nt_s, :nt_s], (1, 2, 0)), HEAD_DIM, axis=-1),
            bs_exp=jnp.repeat(b_s[l].T, HEAD_DIM, axis=-1),
            veca=jnp.stack([ln_v_g[l], ln_v_b[l], conv_b_bias[l], ln_conv_g[l], ln_conv_b[l],
                            zeros_a, zeros_a, zeros_a]),
            cbw=conv_b_w[l], ccw=conv_c_w[l],
            w_out=w_out[l].astype(BF16),
            vecd=jnp.stack([ln1_g[l], ln1_b[l], ln2_g[l], ln2_b[l], ln_in_g, ln_in_b, zeros_d, zeros_d]),
            wr=wr,
        )
        h1p, xs_p, info_p, tab_p, nbp, ncp = _mixer_prompt(hp, lw, l == 0, alpha, n_win_p)
        h1s, vs, nbs, ncs, xs_s, info_s, tab_s = _mixer_sample(hs, sb_t[l], sc_t[l], lw, l == 0, alpha)
        tile_e, nvalid, clist, ntiles = _expert_tables(jnp.concatenate([tab_p, tab_s]), t_max)
        ne = N_EXPERTS
        ys_p, ys_s = _experts(xs_p, xs_s, tile_e, nvalid, clist, ntiles,
                              w_gate[l].reshape(ne, d, -1).astype(BF16), w_up[l].reshape(ne, d, -1).astype(BF16),
                              w_down[l].reshape(ne, -1, d).astype(BF16), t_max)
        hp = _combine(h1p.reshape(nb * t, d), info_p, ys_p, lw["vecd"], alpha, TM // WIN).reshape(nb, t, d)
        hs = _combine(h1s, info_s, ys_s, lw["vecd"], alpha, n_win_s)
        cb_p.append(nbp)
        cc_p.append(ncp)
        cb_s.append(nbs)
        cc_s.append(ncs)
        v_s.append(vs)
    y_sample = jnp.transpose(hs.reshape(nt_s, ns, d), (1, 0, 2))
    untime = lambda xs_: jnp.transpose(jnp.stack(xs_), (0, 2, 1, 3))
    chunk_v = jnp.transpose(jnp.stack(v_s).reshape(depth, nt_s, ns, A_WIDTH), (0, 2, 1, 3))
    return (hp, y_sample, jnp.stack(cb_p), jnp.stack(cc_p), untime(cb_s), untime(cc_s), chunk_v)
```

```python
import functools
import math

import jax
import jax.numpy as jnp
from jax import lax
from jax.experimental import pallas as pl
from jax.experimental.pallas import tpu as pltpu

F32 = jnp.float32
BF16 = jnp.bfloat16

HEAD_DIM = 64
A_WIDTH = 384
B_WIDTH = 384
C_WIDTH = 256
GMLP_CHUNK = 128
CONV_B_TAPS = 31
CONV_C_TAPS = 3
N_GROUPS = 4
EXPERTS_PER_GROUP = 8
N_EXPERTS = N_GROUPS * EXPERTS_PER_GROUP
LN_EPS = 1e-5
INV_SQRT2 = 1.0 / math.sqrt(2.0)

LANES = 128
SUBLANES = 8
WIN = 256
CH = 16
WIN_ROWS = -(-(2 * WIN + N_EXPERTS * (CH - 1)) // WIN) * WIN
WIN_CHUNKS = WIN_ROWS // CH
TILE_CHUNKS = 32
TILE_ROWS = TILE_CHUNKS * CH
SUB_ROWS = 256
TM = 512
CONV_RB = 32
DMA_UNROLL = 4
VMEM_LIMIT = 56 * 1024 * 1024


def _ln(x, g, b):
    mu = jnp.mean(x, axis=-1, keepdims=True)
    xc = x - mu
    var = jnp.mean(xc * xc, axis=-1, keepdims=True)
    return xc * lax.rsqrt(var + LN_EPS) * g + b


def _gelu(x):
    return 0.5 * x * (1.0 + lax.erf(x * INV_SQRT2))


def _silu(x):
    return x * jax.nn.sigmoid(x)


def _dot(a, b):
    return jnp.dot(a, b, preferred_element_type=F32)


def _iota(shape, dim):
    return lax.broadcasted_iota(jnp.int32, shape, dim)


def _route_window(logits, hb):
    w = logits.shape[0]
    lane = _iota((w, LANES), 1)
    neg = jnp.float32(-jnp.inf)
    gmask = lane < N_GROUPS
    gl = jnp.where(gmask, logits, neg)
    gmax = jnp.max(gl, axis=-1, keepdims=True)
    gsel = jnp.min(jnp.where(gl == gmax, lane, LANES), axis=-1, keepdims=True)
    den = jnp.sum(jnp.where(gmask, jnp.exp(gl - gmax), 0.0), axis=-1, keepdims=True)
    gw = 1.0 / den
    lo = N_GROUPS + EXPERTS_PER_GROUP * gsel
    el = jnp.where((lane >= lo) & (lane < lo + EXPERTS_PER_GROUP), logits, neg)
    v1 = jnp.max(el, axis=-1, keepdims=True)
    i1 = jnp.min(jnp.where(el == v1, lane, LANES), axis=-1, keepdims=True)
    el2 = jnp.where(lane == i1, neg, el)
    v2 = jnp.max(el2, axis=-1, keepdims=True)
    i2 = jnp.min(jnp.where(el2 == v2, lane, LANES), axis=-1, keepdims=True)
    e2x = jnp.exp(v2 - v1)
    w1 = gw / (1.0 + e2x)
    w2 = gw * e2x / (1.0 + e2x)
    e1 = i1 - N_GROUPS
    e2 = i2 - N_GROUPS

    oh = (lane == e1) | (lane == e2)
    ohf = jnp.where(oh, 1.0, 0.0)
    lstrict = jnp.where(_iota((w, w), 0) > _iota((w, w), 1), 1.0, 0.0).astype(BF16)
    rank = _dot(lstrict, ohf.astype(BF16))
    cnt = jnp.sum(ohf, axis=0, keepdims=True)
    nch = jnp.floor((cnt + (CH - 1.0)) * (1.0 / CH))
    upper = jnp.where(_iota((LANES, LANES), 0) < _iota((LANES, LANES), 1), 1.0, 0.0).astype(BF16)
    off = _dot(jnp.broadcast_to(nch, (SUBLANES, LANES)).astype(BF16), upper)[0:1]
    dest = off * CH + rank
    r1 = jnp.sum(jnp.where(lane == e1, dest, 0.0), axis=-1, keepdims=True)
    r2 = jnp.sum(jnp.where(lane == e2, dest, 0.0), axis=-1, keepdims=True)
    info = jnp.where(lane == 0, r1, jnp.where(lane == 1, r2,
                     jnp.where(lane == 2, w1, jnp.where(lane == 3, w2, 0.0))))
    info_t = info.T
    rows = _iota((WIN_ROWS, w), 0).astype(F32)
    perm = jnp.where((rows == info_t[0:1, :]) | (rows == info_t[1:2, :]), 1.0, 0.0).astype(BF16)
    xs = _dot(perm, hb).astype(BF16)
    sub = _iota((SUBLANES, LANES), 0)
    tab = jnp.where(sub == 0, nch, jnp.where(sub == 1, off, 0.0)).astype(jnp.int32)
    return xs, info, tab


def _post_mix(h, mixin, w_out_ref, vecd_ref, wr_ref, alpha, h1_ref, xs_ref, info_ref, tab_ref):
    mix_out = _dot(mixin, w_out_ref[...])
    h1 = _ln(alpha * h + mix_out, vecd_ref[0:1, :], vecd_ref[1:2, :])
    h1_ref[...] = h1
    hb = h1.astype(BF16)
    logits = _dot(hb, wr_ref[...])
    for wi in range(h.shape[0] // WIN):
        rs = slice(wi * WIN, (wi + 1) * WIN)
        xs, info, tab = _route_window(logits[rs], hb[rs])
        xs_ref[wi] = xs
        info_ref[wi] = info
        tab_ref[wi] = tab


def _mixer_prompt_body(first, alpha, nj, n_tiles, h_ref, w_in_ref, ws_ref, bs_ref, veca_ref, cbw_ref,
                       ccw_ref, w_out_ref, vecd_ref, wr_ref, xs_s_ref, info_s_ref, tab_s_ref,
                       h1_ref, xs_ref, info_ref, tab_ref, nsb_ref, nsc_ref,
                       xsh, cbuf, mixin, wb, cbs):
    i = pl.program_id(0)

    @pl.when(i < n_tiles)
    def _():
        _mixer_prompt_tile(first, alpha, i % nj, nj - 1, h_ref, w_in_ref, ws_ref, bs_ref, veca_ref,
                           cbw_ref, ccw_ref, w_out_ref, vecd_ref, wr_ref,
                           h1_ref, xs_ref, info_ref, tab_ref, nsb_ref, nsc_ref,
                           xsh, cbuf, mixin, wb, cbs)

    @pl.when(i >= n_tiles)
    def _():
        xs_ref[...] = xs_s_ref[...]
        info_ref[...] = info_s_ref[...]
        tab_ref[...] = tab_s_ref[...]


def _mixer_prompt_tile(first, alpha, j, last, h_ref, w_in_ref, ws_ref, bs_ref, veca_ref, cbw_ref, ccw_ref,
                       w_out_ref, vecd_ref, wr_ref,
                       h1_ref, xs_ref, info_ref, tab_ref, nsb_ref, nsc_ref,
                       xsh, cbuf, mixin, wb, cbs):
    h = h_ref[0]
    if first:
        h = _ln(h, vecd_ref[4:5, :], vecd_ref[5:6, :])
    hb = h.astype(BF16)
    past_b = CONV_B_TAPS - 1
    past_c = CONV_C_TAPS - 1
    b0 = 32 - past_b
    c0 = 8 - past_c

    @pl.when(j == 0)
    def _():
        xsh[0, 0:32, :] = jnp.zeros((32, B_WIDTH), F32)
        cbuf[0:8, :] = jnp.zeros((8, C_WIDTH), F32)
        for k in range(CONV_B_TAPS):
            wb[k] = jnp.broadcast_to(cbw_ref[k:k + 1, :], (SUBLANES, B_WIDTH))

    u = _gelu(_dot(hb, w_in_ref[:, 0:A_WIDTH]))
    v = _ln(_gelu(_dot(hb, w_in_ref[:, A_WIDTH:2 * A_WIDTH])), veca_ref[0:1, :], veca_ref[1:2, :])
    vb = v.astype(BF16)
    tri = (_iota((2 * GMLP_CHUNK, GMLP_CHUNK), 0) % GMLP_CHUNK) >= _iota((2 * GMLP_CHUNK, GMLP_CHUNK), 1)
    wms = [jnp.where(tri, ws_ref[p], 0.0).astype(BF16) for p in range(A_WIDTH // LANES)]
    lane = _iota((GMLP_CHUNK, LANES), 1)
    for c in range(TM // GMLP_CHUNK):
        rs = slice(c * GMLP_CHUNK, (c + 1) * GMLP_CHUNK)
        parts = []
        for p in range(A_WIDTH // LANES):
            ab = _dot(wms[p], vb[rs, p * LANES:(p + 1) * LANES])
            parts.append(jnp.where(lane < HEAD_DIM, ab[:GMLP_CHUNK], ab[GMLP_CHUNK:]))
        mix = jnp.concatenate(parts, axis=1) + bs_ref[...]
        mixin[rs, 0:A_WIDTH] = (u[rs] * mix).astype(BF16)

    o = 2 * A_WIDTH
    glu = _dot(hb, w_in_ref[:, o:o + B_WIDTH]) * jax.nn.sigmoid(_dot(hb, w_in_ref[:, o + B_WIDTH:o + 2 * B_WIDTH]))
    xsh[0, 32:32 + TM, :] = glu
    for r in range(1, SUBLANES):
        xsh[r, 0:TM + 24, :] = xsh[0, r:r + TM + 24, :]

    def conv_blk(rb, carry):
        base = pl.multiple_of(rb * CONV_RB, CONV_RB)
        n_sub = CONV_RB // SUBLANES
        accs = [jnp.broadcast_to(veca_ref[2:3, :], (SUBLANES, B_WIDTH))] * n_sub
        for k in range(CONV_B_TAPS):
            s = k + b0
            w8 = wb[k]
            for a in range(n_sub):
                row = pl.multiple_of(base + (s // SUBLANES + a) * SUBLANES, SUBLANES)
                accs[a] = accs[a] + w8 * xsh[s % SUBLANES, pl.ds(row, SUBLANES), :]
        for a in range(n_sub):
            cbs[pl.ds(base + a * SUBLANES, SUBLANES), :] = accs[a]
        return carry

    lax.fori_loop(0, TM // CONV_RB, conv_blk, 0)
    yb = _silu(_ln(cbs[...], veca_ref[3:4, :], veca_ref[4:5, :]))
    mixin[:, A_WIDTH:A_WIDTH + B_WIDTH] = yb.astype(BF16)
    new_b = xsh[0, TM + b0:TM + 32, :]
    xsh[0, b0:32, :] = new_b

    @pl.when(j == last)
    def _():
        nsb_ref[0] = new_b

    o = 2 * A_WIDTH + 2 * B_WIDTH
    bg = _dot(hb, w_in_ref[:, o:o + C_WIDTH])
    cbuf[8:8 + TM, :] = _dot(hb, w_in_ref[:, o + C_WIDTH:o + 2 * C_WIDTH]) * _dot(hb, w_in_ref[:, o + 2 * C_WIDTH:o + 3 * C_WIDTH])
    cc = ccw_ref[0:1, :] * cbuf[c0:c0 + TM, :]
    for k in range(1, CONV_C_TAPS):
        cc = cc + ccw_ref[k:k + 1, :] * cbuf[c0 + k:c0 + k + TM, :]
    mixin[:, A_WIDTH + B_WIDTH:] = (bg * cc).astype(BF16)
    new_c = cbuf[TM + c0:TM + 8, :]
    cbuf[c0:8, :] = new_c

    @pl.when(j == last)
    def _():
        nsc_ref[0] = new_c

    _post_mix(h, mixin[...], w_out_ref, vecd_ref, wr_ref, alpha, h1_ref.at[0], xs_ref, info_ref, tab_ref)


def _mixer_prompt(h, lw, first, alpha, xs_s, info_s, tab_s):
    nb, t, d = h.shape
    nj = t // TM
    wpt = TM // WIN
    n_tiles = nb * nj
    n_extra = xs_s.shape[0] // wpt
    n_win_total = (n_tiles + n_extra) * wpt
    const = lambda shape: pl.BlockSpec(shape, lambda i: (0,) * len(shape))
    tile = lambda i: jnp.minimum(i, n_tiles - 1)
    h_map = lambda i: (tile(i) // nj, tile(i) % nj, 0)
    state_map = lambda i: (tile(i) // nj, 0, 0)
    extra_map = lambda i: (jnp.maximum(i - n_tiles, 0), 0, 0)
    out_shape = (
        jax.ShapeDtypeStruct((nb, t, d), F32),
        jax.ShapeDtypeStruct((n_win_total, WIN_ROWS, d), BF16),
        jax.ShapeDtypeStruct((n_win_total, WIN, LANES), F32),
        jax.ShapeDtypeStruct((n_win_total, SUBLANES, LANES), jnp.int32),
        jax.ShapeDtypeStruct((nb, CONV_B_TAPS - 1, B_WIDTH), F32),
        jax.ShapeDtypeStruct((nb, CONV_C_TAPS - 1, C_WIDTH), F32),
    )
    win_map = lambda i: (i, 0, 0)
    return pl.pallas_call(
        functools.partial(_mixer_prompt_body, first, alpha, nj, n_tiles),
        grid=(n_tiles + n_extra,),
        in_specs=[
            pl.BlockSpec((1, TM, d), h_map),
            const(lw["w_in"].shape), const(lw["ws2"].shape), const(lw["bs_exp"].shape),
            const(lw["veca"].shape), const(lw["cbw"].shape), const(lw["ccw"].shape),
            const(lw["w_out"].shape), const(lw["vecd"].shape), const(lw["wr"].shape),
            pl.BlockSpec((wpt, WIN_ROWS, d), extra_map),
            pl.BlockSpec((wpt, WIN, LANES), extra_map),
            pl.BlockSpec((wpt, SUBLANES, LANES), extra_map),
        ],
        out_specs=(
            pl.BlockSpec((1, TM, d), h_map),
            pl.BlockSpec((wpt, WIN_ROWS, d), win_map),
            pl.BlockSpec((wpt, WIN, LANES), win_map),
            pl.BlockSpec((wpt, SUBLANES, LANES), win_map),
            pl.BlockSpec((1, CONV_B_TAPS - 1, B_WIDTH), state_map),
            pl.BlockSpec((1, CONV_C_TAPS - 1, C_WIDTH), state_map),
        ),
        out_shape=out_shape,
        scratch_shapes=[
            pltpu.VMEM((SUBLANES, TM + 32, B_WIDTH), F32),
            pltpu.VMEM((TM + 8, C_WIDTH), F32),
            pltpu.VMEM((TM, d), BF16),
            pltpu.VMEM((CONV_B_TAPS, SUBLANES, B_WIDTH), F32),
            pltpu.VMEM((TM, B_WIDTH), F32),
        ],
        compiler_params=pltpu.CompilerParams(
            dimension_semantics=("arbitrary",), vmem_limit_bytes=VMEM_LIMIT),
        name="mixer_prompt",
    )(h, lw["w_in"], lw["ws2"], lw["bs_exp"], lw["veca"], lw["cbw"], lw["ccw"],
      lw["w_out"], lw["vecd"], lw["wr"], xs_s, info_s, tab_s)


def _mixer_sample_body(first, alpha, n_seq, n_t,
                       h_ref, sb_ref, sc_ref, w_in_ref, wexp_ref, bs_ref, veca_ref, cbw_ref, ccw_ref,
                       w_out_ref, vecd_ref, wr_ref,
                       h1_ref, v_ref, nsb_ref, nsc_ref, xs_ref, info_ref, tab_ref,
                       mixin):
    h = h_ref[...]
    if first:
        h = _ln(h, vecd_ref[4:5, :], vecd_ref[5:6, :])
    hb = h.astype(BF16)
    rows = lambda t: slice(t * n_seq, (t + 1) * n_seq)

    u = _gelu(_dot(hb, w_in_ref[:, 0:A_WIDTH]))
    v = _ln(_gelu(_dot(hb, w_in_ref[:, A_WIDTH:2 * A_WIDTH])), veca_ref[0:1, :], veca_ref[1:2, :])
    v_ref[...] = v
    for t in range(n_t):
        mix = jnp.broadcast_to(bs_ref[t:t + 1, :], (n_seq, A_WIDTH))
        for s in range(t + 1):
            mix = mix + wexp_ref[t, s:s + 1, :] * v[rows(s)]
        mixin[rows(t), 0:A_WIDTH] = (u[rows(t)] * mix).astype(BF16)

    o = 2 * A_WIDTH
    glu = _dot(hb, w_in_ref[:, o:o + B_WIDTH]) * jax.nn.sigmoid(_dot(hb, w_in_ref[:, o + B_WIDTH:o + 2 * B_WIDTH]))
    past_b = CONV_B_TAPS - 1
    xp = lambda m: sb_ref[m] if m < past_b else glu[rows(m - past_b)]
    for t in range(n_t):
        acc = jnp.broadcast_to(veca_ref[2:3, :], (n_seq, B_WIDTH))
        for k in range(CONV_B_TAPS):
            acc = acc + cbw_ref[k:k + 1, :] * xp(t + k)
        yb = _silu(_ln(acc, veca_ref[3:4, :], veca_ref[4:5, :]))
        mixin[rows(t), A_WIDTH:A_WIDTH + B_WIDTH] = yb.astype(BF16)
    for r in range(past_b):
        nsb_ref[r] = xp(r + n_t)

    o = 2 * A_WIDTH + 2 * B_WIDTH
    bg = _dot(hb, w_in_ref[:, o:o + C_WIDTH])
    xc = _dot(hb, w_in_ref[:, o + C_WIDTH:o + 2 * C_WIDTH]) * _dot(hb, w_in_ref[:, o + 2 * C_WIDTH:o + 3 * C_WIDTH])
    past_c = CONV_C_TAPS - 1
    xq = lambda m: sc_ref[m] if m < past_c else xc[rows(m - past_c)]
    for t in range(n_t):
        cc = ccw_ref[0:1, :] * xq(t)
        for k in range(1, CONV_C_TAPS):
            cc = cc + ccw_ref[k:k + 1, :] * xq(t + k)
        mixin[rows(t), A_WIDTH + B_WIDTH:] = (bg[rows(t)] * cc).astype(BF16)
    for r in range(past_c):
        nsc_ref[r] = xq(r + n_t)

    _post_mix(h, mixin[...], w_out_ref, vecd_ref, wr_ref, alpha, h1_ref, xs_ref, info_ref, tab_ref)


def _mixer_sample(h, sb_t, sc_t, lw, first, alpha):
    m, d = h.shape
    n_seq = sb_t.shape[1]
    n_t = m // n_seq
    nw = m // WIN
    full = lambda a: pl.BlockSpec(a.shape, lambda i: (0,) * a.ndim)
    ins = [h, sb_t, sc_t, lw["w_in"], lw["wexp"], lw["bs_exp"], lw["veca"], lw["cbw"], lw["ccw"],
           lw["w_out"], lw["vecd"], lw["wr"]]
    out_shape = (
        jax.ShapeDtypeStruct((m, d), F32),
        jax.ShapeDtypeStruct((m, A_WIDTH), F32),
        jax.ShapeDtypeStruct(sb_t.shape, F32),
        jax.ShapeDtypeStruct(sc_t.shape, F32),
        jax.ShapeDtypeStruct((nw, WIN_ROWS, d), BF16),
        jax.ShapeDtypeStruct((nw, WIN, LANES), F32),
        jax.ShapeDtypeStruct((nw, SUBLANES, LANES), jnp.int32),
    )
    return pl.pallas_call(
        functools.partial(_mixer_sample_body, first, alpha, n_seq, n_t),
        grid=(1,),
        in_specs=[full(a) for a in ins],
        out_specs=tuple(full(o) for o in out_shape),
        out_shape=out_shape,
        scratch_shapes=[pltpu.VMEM((m, d), BF16)],
        compiler_params=pltpu.CompilerParams(
            dimension_semantics=("arbitrary",), vmem_limit_bytes=VMEM_LIMIT),
        name="mixer_sample",
    )(*ins)


def _experts_body(tile_e_ref, fresh_ref, nvalid_ref, clist_ref, ntiles_ref,
                  xs_hbm, wg_ref, wu_ref, wd_ref, ys_hbm,
                  lhs, obuf, wgb, wub, wdb, sem_in, sem_out):
    del tile_e_ref
    t = pl.program_id(0)
    nt = ntiles_ref[0]

    def copy_in(cid, slot, j):
        return pltpu.make_async_copy(xs_hbm.at[cid], lhs.at[slot, pl.ds(j * CH, CH)], sem_in.at[slot])

    def copy_out(cid, slot, j):
        return pltpu.make_async_copy(obuf.at[slot, pl.ds(j * CH, CH)], ys_hbm.at[cid], sem_out.at[slot])

    def start_all(make, tt, slot):
        n = nvalid_ref[tt]
        base = tt * TILE_CHUNKS
        groups = n // DMA_UNROLL

        def group(g, carry):
            for k in range(DMA_UNROLL):
                j = g * DMA_UNROLL + k
                make(clist_ref[base + j], slot, j).start()
            return carry

        def single(j, carry):
            make(clist_ref[base + j], slot, j).start()
            return carry

        lax.fori_loop(0, groups, group, 0)
        lax.fori_loop(groups * DMA_UNROLL, n, single, 0)

    def wait_all(make, tt, slot):
        def body(j, carry):
            make(0, slot, j).wait()
            return carry
        lax.fori_loop(0, nvalid_ref[tt], body, 0)

    @pl.when(t == 0)
    def _():
        lhs[...] = jnp.zeros(lhs.shape, lhs.dtype)
        start_all(copy_in, 0, 0)

    @pl.when(t + 1 < nt)
    def _():
        start_all(copy_in, t + 1, (t + 1) % 2)

    @pl.when(fresh_ref[t] == 1)
    def _():
        wgb[...] = wg_ref[0].astype(BF16)
        wub[...] = wu_ref[0].astype(BF16)
        wdb[...] = wd_ref[0].astype(BF16)

    @pl.when(t < nt)
    def _():
        slot = t % 2
        wait_all(copy_in, t, slot)

        @pl.when(t >= 2)
        def _():
            wait_all(copy_out, t - 2, slot)

        def ffn(rows):
            x = lhs[slot, 0:rows, :]
            act = _silu(_dot(x, wgb[...])) * _dot(x, wub[...])
            obuf[slot, 0:rows, :] = _dot(act.astype(BF16), wdb[...]).astype(BF16)

        @pl.when(nvalid_ref[t] > SUB_ROWS // CH)
        def _():
            ffn(TILE_ROWS)

        @pl.when(nvalid_ref[t] <= SUB_ROWS // CH)
        def _():
            ffn(SUB_ROWS)

        start_all(copy_out, t, slot)

    @pl.when(t == pl.num_programs(0) - 1)
    def _():
        for back in (2, 1):
            @pl.when(nt >= back)
            def _():
                wait_all(copy_out, nt - back, (nt - back) % 2)


def _experts(xs, tile_e, fresh, nvalid, clist, ntiles, wg, wu, wd, t_max):
    n_win, _, d = xs.shape
    xc = xs.reshape(n_win * WIN_CHUNKS, CH, d)
    de = wg.shape[-1]
    anyspec = pl.BlockSpec(memory_space=pl.ANY)
    n_prefetch = 5
    by_expert = lambda t, te, fr, nv, cl, n: (te[t], 0, 0)
    grid_spec = pltpu.PrefetchScalarGridSpec(
        num_scalar_prefetch=n_prefetch,
        grid=(t_max,),
        in_specs=[
            anyspec,
            pl.BlockSpec((1, d, de), by_expert),
            pl.BlockSpec((1, d, de), by_expert),
            pl.BlockSpec((1, de, d), by_expert),
        ],
        out_specs=anyspec,
        scratch_shapes=[
            pltpu.VMEM((2, TILE_ROWS, d), BF16),
            pltpu.VMEM((2, TILE_ROWS, d), BF16),
            pltpu.VMEM((d, de), BF16),
            pltpu.VMEM((d, de), BF16),
            pltpu.VMEM((de, d), BF16),
            pltpu.SemaphoreType.DMA((2,)),
            pltpu.SemaphoreType.DMA((2,)),
        ],
    )
    ys = pl.pallas_call(
        _experts_body,
        grid_spec=grid_spec,
        out_shape=jax.ShapeDtypeStruct(xc.shape, xc.dtype),
        input_output_aliases={n_prefetch: 0},
        compiler_params=pltpu.CompilerParams(
            dimension_semantics=("arbitrary",), vmem_limit_bytes=VMEM_LIMIT),
        name="experts",
    )(tile_e, fresh, nvalid, clist, ntiles, xc, wg, wu, wd)
    return ys.reshape(xs.shape)


def _expert_tables(tab, t_max):
    i32 = jnp.int32
    nch = tab[:, 0, :N_EXPERTS]
    off = tab[:, 1, :N_EXPERTS]
    n_win = nch.shape[0]
    cum_incl = jnp.cumsum(nch, axis=0)
    cum_excl = cum_incl - nch
    ce = cum_incl[-1]
    te = (ce + TILE_CHUNKS - 1) // TILE_CHUNKS
    tile_end = jnp.cumsum(te)
    tile_start = tile_end - te
    nt = tile_end[-1]
    t_idx = jnp.arange(t_max, dtype=i32)
    tile_e = jnp.minimum(jnp.sum((tile_end[None, :] <= t_idx[:, None]).astype(i32), axis=1), N_EXPERTS - 1)
    sel_e = (tile_e[:, None] == jnp.arange(N_EXPERTS, dtype=i32)[None, :]).astype(i32)
    pick = lambda v: jnp.sum(sel_e * v[None, :], axis=1)
    q0 = (t_idx - pick(tile_start)) * TILE_CHUNKS
    nvalid = jnp.where(t_idx < nt, jnp.clip(pick(ce) - q0, 0, TILE_CHUNKS), 0).astype(i32)
    q = q0[:, None] + jnp.arange(TILE_CHUNKS, dtype=i32)[None, :]
    by_win = lambda m: jnp.sum(sel_e[:, :, None] * m.T[None, :, :], axis=1)
    cum_e = by_win(cum_incl)
    w = jnp.minimum(jnp.sum((cum_e[:, None, :] <= q[:, :, None]).astype(i32), axis=-1), n_win - 1)
    sel_w = (w[:, :, None] == jnp.arange(n_win, dtype=i32)[None, None, :]).astype(i32)
    shift = jnp.sum(sel_w * by_win(off - cum_excl)[:, None, :], axis=-1)
    cid = w * WIN_CHUNKS + shift + q
    valid = jnp.arange(TILE_CHUNKS, dtype=i32)[None, :] < nvalid[:, None]
    clist = jnp.where(valid, cid, 0).astype(i32).reshape(-1)
    prev_e = jnp.concatenate([jnp.full((1,), -1, i32), tile_e[:-1]])
    fresh = ((t_idx < nt) & (tile_e != prev_e)).astype(i32)
    return tile_e.astype(i32), fresh, nvalid, clist, nt.astype(i32).reshape(1)


def _combine_body(alpha, h1_ref, info_ref, ys_ref, vecd_ref, out_ref):
    for wi in range(info_ref.shape[0]):
        info = info_ref[wi]
        col = _iota((WIN, WIN_ROWS), 1).astype(F32)
        pt = (jnp.where(col == info[:, 0:1], info[:, 2:3], 0.0)
              + jnp.where(col == info[:, 1:2], info[:, 3:4], 0.0)).astype(BF16)
        y = _dot(pt, ys_ref[wi])
        rs = slice(wi * WIN, (wi + 1) * WIN)
        out_ref[rs, :] = _ln(alpha * h1_ref[rs, :] + y, vecd_ref[2:3, :], vecd_ref[3:4, :])


def _combine(h1, info, ys, vecd, alpha, wpb, block0):
    m, d = h1.shape
    win_map = lambda i: (block0 + i, 0, 0)
    return pl.pallas_call(
        functools.partial(_combine_body, alpha),
        grid=(m // (wpb * WIN),),
        in_specs=[
            pl.BlockSpec((wpb * WIN, d), lambda i: (i, 0)),
            pl.BlockSpec((wpb, WIN, LANES), win_map),
            pl.BlockSpec((wpb, WIN_ROWS, d), win_map),
            pl.BlockSpec(vecd.shape, lambda i: (0, 0)),
        ],
        out_specs=pl.BlockSpec((wpb * WIN, d), lambda i: (i, 0)),
        out_shape=jax.ShapeDtypeStruct((m, d), F32),
        compiler_params=pltpu.CompilerParams(
            dimension_semantics=("arbitrary",), vmem_limit_bytes=VMEM_LIMIT),
        name="combine",
    )(h1, info, ys, vecd)


def kernel(x_prompt, x_sample, state_conv_b, state_conv_c, ln_in_g, ln_in_b, w_in, w_s, b_s, ln_v_g, ln_v_b, conv_b_w, conv_b_bias, ln_conv_g, ln_conv_b, conv_c_w, w_out, ln1_g, ln1_b, w_router_group, w_router_expert, w_gate, w_up, w_down, ln2_g, ln2_b):
    depth = w_in.shape[0]
    nb, t, d = x_prompt.shape
    ns, nt_s, _ = x_sample.shape
    assert t % TM == 0 and (ns * nt_s) % WIN == 0 and TM % WIN == 0 and d % LANES == 0
    alpha = (2.0 * depth) ** 0.25
    n_win_p = nb * t // WIN
    n_win_s = ns * nt_s // WIN
    n_win = n_win_p + n_win_s
    assert n_win_p % n_win_s == 0 and n_win_s % (TM // WIN) == 0
    max_chunks = n_win * (2 * WIN // CH + N_EXPERTS)
    t_max = max_chunks // TILE_CHUNKS + N_EXPERTS

    hp = x_prompt
    hs = jnp.transpose(x_sample, (1, 0, 2)).reshape(nt_s * ns, d)
    sb_t = jnp.transpose(state_conv_b, (0, 2, 1, 3))
    sc_t = jnp.transpose(state_conv_c, (0, 2, 1, 3))
    cb_p, cc_p, cb_s, cc_s, v_s = [], [], [], [], []
    for l in range(depth):
        wr = jnp.concatenate([w_router_group[l], w_router_expert[l].reshape(d, N_EXPERTS)], axis=1)
        wr = jnp.pad(wr, ((0, 0), (0, LANES - wr.shape[1]))).astype(BF16)
        zeros_a = jnp.zeros((A_WIDTH,), F32)
        zeros_d = jnp.zeros((d,), F32)
        lw = dict(
            w_in=w_in[l].astype(BF16),
            ws2=w_s[l].reshape(A_WIDTH // LANES, 2 * GMLP_CHUNK, GMLP_CHUNK),
            wexp=jnp.repeat(jnp.transpose(w_s[l][:, :nt_s, :nt_s], (1, 2, 0)), HEAD_DIM, axis=-1),
            bs_exp=jnp.repeat(b_s[l].T, HEAD_DIM, axis=-1),
            veca=jnp.stack([ln_v_g[l], ln_v_b[l], conv_b_bias[l], ln_conv_g[l], ln_conv_b[l],
                            zeros_a, zeros_a, zeros_a]),
            cbw=conv_b_w[l], ccw=conv_c_w[l],
            w_out=w_out[l].astype(BF16),
            vecd=jnp.stack([ln1_g[l], ln1_b[l], ln2_g[l], ln2_b[l], ln_in_g, ln_in_b, zeros_d, zeros_d]),
            wr=wr,
        )
        h1s, vs, nbs, ncs, xs_s, info_s, tab_s = _mixer_sample(hs, sb_t[l], sc_t[l], lw, l == 0, alpha)
        h1p, xs, info, tab, nbp, ncp = _mixer_prompt(hp, lw, l == 0, alpha, xs_s, info_s, tab_s)
        tile_e, fresh, nvalid, clist, ntiles = _expert_tables(tab, t_max)
        ne = N_EXPERTS
        ys = _experts(xs, tile_e, fresh, nvalid, clist, ntiles, w_gate[l].reshape(ne, d, -1),
                      w_up[l].reshape(ne, d, -1), w_down[l].reshape(ne, -1, d), t_max)
        hp = _combine(h1p.reshape(nb * t, d), info, ys, lw["vecd"], alpha, TM // WIN, 0).reshape(nb, t, d)
        hs = _combine(h1s, info, ys, lw["vecd"], alpha, n_win_s, n_win_p // n_win_s)
        cb_p.append(nbp)
        cc_p.append(ncp)
        cb_s.append(nbs)
        cc_s.append(ncs)
        v_s.append(vs)
    y_sample = jnp.transpose(hs.reshape(nt_s, ns, d), (1, 0, 2))
    untime = lambda xs_: jnp.transpose(jnp.stack(xs_), (0, 2, 1, 3))
    chunk_v = jnp.transpose(jnp.stack(v_s).reshape(depth, nt_s, ns, A_WIDTH), (0, 2, 1, 3))
    return (hp, y_sample, jnp.stack(cb_p), jnp.stack(cc_p), untime(cb_s), untime(cc_s), chunk_v)
```

```python
import functools
import math

import jax
import jax.numpy as jnp
from jax import lax
from jax.experimental import pallas as pl
from jax.experimental.pallas import tpu as pltpu

F32 = jnp.float32
BF16 = jnp.bfloat16

HEAD_DIM = 64
A_WIDTH = 384
B_WIDTH = 384
C_WIDTH = 256
GMLP_CHUNK = 128
CONV_B_TAPS = 31
CONV_C_TAPS = 3
N_GROUPS = 4
EXPERTS_PER_GROUP = 8
N_EXPERTS = N_GROUPS * EXPERTS_PER_GROUP
LN_EPS = 1e-5
INV_SQRT2 = 1.0 / math.sqrt(2.0)

LANES = 128
SUBLANES = 8
WIN = 256
CH = 16
WIN_ROWS = -(-(2 * WIN + N_EXPERTS * (CH - 1)) // WIN) * WIN
WIN_CHUNKS = WIN_ROWS // CH
TILE_CHUNKS = 32
TILE_ROWS = TILE_CHUNKS * CH
SUB_ROWS = 256
TM = 512
CONV_RB = 32
DMA_UNROLL = 4
VMEM_LIMIT = 56 * 1024 * 1024


def _ln(x, g, b):
    mu = jnp.mean(x, axis=-1, keepdims=True)
    xc = x - mu
    var = jnp.mean(xc * xc, axis=-1, keepdims=True)
    return xc * lax.rsqrt(var + LN_EPS) * g + b


def _gelu(x):
    return 0.5 * x * (1.0 + lax.erf(x * INV_SQRT2))


def _silu(x):
    return x * jax.nn.sigmoid(x)


def _dot(a, b):
    return jnp.dot(a, b, preferred_element_type=F32)


def _iota(shape, dim):
    return lax.broadcasted_iota(jnp.int32, shape, dim)


def _route_window(logits, hb):
    w = logits.shape[0]
    lane = _iota((w, LANES), 1)
    neg = jnp.float32(-jnp.inf)
    gmask = lane < N_GROUPS
    gl = jnp.where(gmask, logits, neg)
    gmax = jnp.max(gl, axis=-1, keepdims=True)
    gsel = jnp.min(jnp.where(gl == gmax, lane, LANES), axis=-1, keepdims=True)
    den = jnp.sum(jnp.where(gmask, jnp.exp(gl - gmax), 0.0), axis=-1, keepdims=True)
    gw = 1.0 / den
    lo = N_GROUPS + EXPERTS_PER_GROUP * gsel
    el = jnp.where((lane >= lo) & (lane < lo + EXPERTS_PER_GROUP), logits, neg)
    v1 = jnp.max(el, axis=-1, keepdims=True)
    i1 = jnp.min(jnp.where(el == v1, lane, LANES), axis=-1, keepdims=True)
    el2 = jnp.where(lane == i1, neg, el)
    v2 = jnp.max(el2, axis=-1, keepdims=True)
    i2 = jnp.min(jnp.where(el2 == v2, lane, LANES), axis=-1, keepdims=True)
    e2x = jnp.exp(v2 - v1)
    w1 = gw / (1.0 + e2x)
    w2 = gw * e2x / (1.0 + e2x)
    e1 = i1 - N_GROUPS
    e2 = i2 - N_GROUPS

    oh = (lane == e1) | (lane == e2)
    ohf = jnp.where(oh, 1.0, 0.0)
    lstrict = jnp.where(_iota((w, w), 0) > _iota((w, w), 1), 1.0, 0.0).astype(BF16)
    rank = _dot(lstrict, ohf.astype(BF16))
    cnt = jnp.sum(ohf, axis=0, keepdims=True)
    nch = jnp.floor((cnt + (CH - 1.0)) * (1.0 / CH))
    upper = jnp.where(_iota((LANES, LANES), 0) < _iota((LANES, LANES), 1), 1.0, 0.0).astype(BF16)
    off = _dot(jnp.broadcast_to(nch, (SUBLANES, LANES)).astype(BF16), upper)[0:1]
    dest = off * CH + rank
    r1 = jnp.sum(jnp.where(lane == e1, dest, 0.0), axis=-1, keepdims=True)
    r2 = jnp.sum(jnp.where(lane == e2, dest, 0.0), axis=-1, keepdims=True)
    info = jnp.where(lane == 0, r1, jnp.where(lane == 1, r2,
                     jnp.where(lane == 2, w1, jnp.where(lane == 3, w2, 0.0))))
    info_t = info.T
    rows = _iota((WIN_ROWS, w), 0).astype(F32)
    perm = jnp.where((rows == info_t[0:1, :]) | (rows == info_t[1:2, :]), 1.0, 0.0).astype(BF16)
    xs = _dot(perm, hb).astype(BF16)
    sub = _iota((SUBLANES, LANES), 0)
    tab = jnp.where(sub == 0, nch, jnp.where(sub == 1, off, 0.0)).astype(jnp.int32)
    return xs, info, tab


def _post_mix(h, mixin, w_out_ref, vecd_ref, wr_ref, alpha, h1_ref, xs_ref, info_ref, tab_ref):
    mix_out = _dot(mixin, w_out_ref[...])
    h1 = _ln(alpha * h + mix_out, vecd_ref[0:1, :], vecd_ref[1:2, :])
    h1_ref[...] = h1
    hb = h1.astype(BF16)
    logits = _dot(hb, wr_ref[...])
    for wi in range(h.shape[0] // WIN):
        rs = slice(wi * WIN, (wi + 1) * WIN)
        xs, info, tab = _route_window(logits[rs], hb[rs])
        xs_ref[wi] = xs
        info_ref[wi] = info
        tab_ref[wi] = tab


def _mixer_prompt_body(first, alpha, nj, n_tiles, h_ref, w_in_ref, ws_ref, bs_ref, veca_ref, cbw_ref,
                       ccw_ref, w_out_ref, vecd_ref, wr_ref, xs_s_ref, info_s_ref, tab_s_ref,
                       h1_ref, xs_ref, info_ref, tab_ref, nsb_ref, nsc_ref,
                       xsh, cbuf, mixin, wb, cbs):
    i = pl.program_id(0)

    @pl.when(i < n_tiles)
    def _():
        _mixer_prompt_tile(first, alpha, i % nj, nj - 1, h_ref, w_in_ref, ws_ref, bs_ref, veca_ref,
                           cbw_ref, ccw_ref, w_out_ref, vecd_ref, wr_ref,
                           h1_ref, xs_ref, info_ref, tab_ref, nsb_ref, nsc_ref,
                           xsh, cbuf, mixin, wb, cbs)

    @pl.when(i >= n_tiles)
    def _():
        xs_ref[...] = xs_s_ref[...]
        info_ref[...] = info_s_ref[...]
        tab_ref[...] = tab_s_ref[...]


def _mixer_prompt_tile(first, alpha, j, last, h_ref, w_in_ref, ws_ref, bs_ref, veca_ref, cbw_ref, ccw_ref,
                       w_out_ref, vecd_ref, wr_ref,
                       h1_ref, xs_ref, info_ref, tab_ref, nsb_ref, nsc_ref,
                       xsh, cbuf, mixin, wb, cbs):
    h = h_ref[0]
    if first:
        h = _ln(h, vecd_ref[4:5, :], vecd_ref[5:6, :])
    hb = h.astype(BF16)
    past_b = CONV_B_TAPS - 1
    past_c = CONV_C_TAPS - 1
    b0 = 32 - past_b
    c0 = 8 - past_c

    @pl.when(j == 0)
    def _():
        xsh[0, 0:32, :] = jnp.zeros((32, B_WIDTH), F32)
        cbuf[0:8, :] = jnp.zeros((8, C_WIDTH), F32)
        for k in range(CONV_B_TAPS):
            wb[k] = jnp.broadcast_to(cbw_ref[k:k + 1, :], (SUBLANES, B_WIDTH))

    oa, ob = 2 * A_WIDTH, 2 * A_WIDTH + 2 * B_WIDTH
    za = _dot(hb, w_in_ref[:, 0:oa])
    zb = _dot(hb, w_in_ref[:, oa:ob])
    zc = _dot(hb, w_in_ref[:, ob:ob + 3 * C_WIDTH])

    ga = _gelu(za)
    u = ga[:, 0:A_WIDTH]
    v = _ln(ga[:, A_WIDTH:], veca_ref[0:1, :], veca_ref[1:2, :])
    vb = v.astype(BF16)
    tri = (_iota((2 * GMLP_CHUNK, GMLP_CHUNK), 0) % GMLP_CHUNK) >= _iota((2 * GMLP_CHUNK, GMLP_CHUNK), 1)
    wms = [jnp.where(tri, ws_ref[p], 0.0).astype(BF16) for p in range(A_WIDTH // LANES)]
    lane = _iota((GMLP_CHUNK, LANES), 1)
    for c in range(TM // GMLP_CHUNK):
        rs = slice(c * GMLP_CHUNK, (c + 1) * GMLP_CHUNK)
        parts = []
        for p in range(A_WIDTH // LANES):
            ab = _dot(wms[p], vb[rs, p * LANES:(p + 1) * LANES])
            parts.append(jnp.where(lane < HEAD_DIM, ab[:GMLP_CHUNK], ab[GMLP_CHUNK:]))
        mix = jnp.concatenate(parts, axis=1) + bs_ref[...]
        mixin[rs, 0:A_WIDTH] = (u[rs] * mix).astype(BF16)

    glu = zb[:, 0:B_WIDTH] * jax.nn.sigmoid(zb[:, B_WIDTH:])
    xsh[0, 32:32 + TM, :] = glu
    for r in range(1, SUBLANES):
        xsh[r, 0:TM + 24, :] = xsh[0, r:r + TM + 24, :]

    n_sub = CONV_RB // SUBLANES
    for rb in range(TM // CONV_RB):
        base = rb * CONV_RB
        accs = [jnp.broadcast_to(veca_ref[2:3, :], (SUBLANES, B_WIDTH))] * n_sub
        for k in range(CONV_B_TAPS):
            s = k + b0
            w8 = wb[k]
            for a in range(n_sub):
                row = base + (s // SUBLANES + a) * SUBLANES
                accs[a] = accs[a] + w8 * xsh[s % SUBLANES, row:row + SUBLANES, :]
        for a in range(n_sub):
            cbs[base + a * SUBLANES:base + (a + 1) * SUBLANES, :] = accs[a]
    yb = _silu(_ln(cbs[...], veca_ref[3:4, :], veca_ref[4:5, :]))
    mixin[:, A_WIDTH:A_WIDTH + B_WIDTH] = yb.astype(BF16)
    new_b = xsh[0, TM + b0:TM + 32, :]
    xsh[0, b0:32, :] = new_b

    @pl.when(j == last)
    def _():
        nsb_ref[0] = new_b

    bg = zc[:, 0:C_WIDTH]
    cbuf[8:8 + TM, :] = zc[:, C_WIDTH:2 * C_WIDTH] * zc[:, 2 * C_WIDTH:]
    cc = ccw_ref[0:1, :] * cbuf[c0:c0 + TM, :]
    for k in range(1, CONV_C_TAPS):
        cc = cc + ccw_ref[k:k + 1, :] * cbuf[c0 + k:c0 + k + TM, :]
    mixin[:, A_WIDTH + B_WIDTH:] = (bg * cc).astype(BF16)
    new_c = cbuf[TM + c0:TM + 8, :]
    cbuf[c0:8, :] = new_c

    @pl.when(j == last)
    def _():
        nsc_ref[0] = new_c

    _post_mix(h, mixin[...], w_out_ref, vecd_ref, wr_ref, alpha, h1_ref.at[0], xs_ref, info_ref, tab_ref)


def _mixer_prompt(h, lw, first, alpha, xs_s, info_s, tab_s):
    nb, t, d = h.shape
    nj = t // TM
    wpt = TM // WIN
    n_tiles = nb * nj
    n_extra = xs_s.shape[0] // wpt
    n_win_total = (n_tiles + n_extra) * wpt
    const = lambda shape: pl.BlockSpec(shape, lambda i: (0,) * len(shape))
    tile = lambda i: jnp.minimum(i, n_tiles - 1)
    h_map = lambda i: (tile(i) // nj, tile(i) % nj, 0)
    state_map = lambda i: (tile(i) // nj, 0, 0)
    extra_map = lambda i: (jnp.maximum(i - n_tiles, 0), 0, 0)
    out_shape = (
        jax.ShapeDtypeStruct((nb, t, d), F32),
        jax.ShapeDtypeStruct((n_win_total, WIN_ROWS, d), BF16),
        jax.ShapeDtypeStruct((n_win_total, WIN, LANES), F32),
        jax.ShapeDtypeStruct((n_win_total, SUBLANES, LANES), jnp.int32),
        jax.ShapeDtypeStruct((nb, CONV_B_TAPS - 1, B_WIDTH), F32),
        jax.ShapeDtypeStruct((nb, CONV_C_TAPS - 1, C_WIDTH), F32),
    )
    win_map = lambda i: (i, 0, 0)
    return pl.pallas_call(
        functools.partial(_mixer_prompt_body, first, alpha, nj, n_tiles),
        grid=(n_tiles + n_extra,),
        in_specs=[
            pl.BlockSpec((1, TM, d), h_map),
            const(lw["w_in"].shape), const(lw["ws2"].shape), const(lw["bs_exp"].shape),
            const(lw["veca"].shape), const(lw["cbw"].shape), const(lw["ccw"].shape),
            const(lw["w_out"].shape), const(lw["vecd"].shape), const(lw["wr"].shape),
            pl.BlockSpec((wpt, WIN_ROWS, d), extra_map),
            pl.BlockSpec((wpt, WIN, LANES), extra_map),
            pl.BlockSpec((wpt, SUBLANES, LANES), extra_map),
        ],
        out_specs=(
            pl.BlockSpec((1, TM, d), h_map),
            pl.BlockSpec((wpt, WIN_ROWS, d), win_map),
            pl.BlockSpec((wpt, WIN, LANES), win_map),
            pl.BlockSpec((wpt, SUBLANES, LANES), win_map),
            pl.BlockSpec((1, CONV_B_TAPS - 1, B_WIDTH), state_map),
            pl.BlockSpec((1, CONV_C_TAPS - 1, C_WIDTH), state_map),
        ),
        out_shape=out_shape,
        scratch_shapes=[
            pltpu.VMEM((SUBLANES, TM + 32, B_WIDTH), F32),
            pltpu.VMEM((TM + 8, C_WIDTH), F32),
            pltpu.VMEM((TM, d), BF16),
            pltpu.VMEM((CONV_B_TAPS, SUBLANES, B_WIDTH), F32),
            pltpu.VMEM((TM, B_WIDTH), F32),
        ],
        compiler_params=pltpu.CompilerParams(
            dimension_semantics=("arbitrary",), vmem_limit_bytes=VMEM_LIMIT),
        name="mixer_prompt",
    )(h, lw["w_in"], lw["ws2"], lw["bs_exp"], lw["veca"], lw["cbw"], lw["ccw"],
      lw["w_out"], lw["vecd"], lw["wr"], xs_s, info_s, tab_s)


def _mixer_sample_body(first, alpha, n_seq, n_t,
                       h_ref, sb_ref, sc_ref, w_in_ref, wexp_ref, bs_ref, veca_ref, cbw_ref, ccw_ref,
                       w_out_ref, vecd_ref, wr_ref,
                       h1_ref, v_ref, nsb_ref, nsc_ref, xs_ref, info_ref, tab_ref,
                       mixin):
    h = h_ref[...]
    if first:
        h = _ln(h, vecd_ref[4:5, :], vecd_ref[5:6, :])
    hb = h.astype(BF16)
    rows = lambda t: slice(t * n_seq, (t + 1) * n_seq)

    u = _gelu(_dot(hb, w_in_ref[:, 0:A_WIDTH]))
    v = _ln(_gelu(_dot(hb, w_in_ref[:, A_WIDTH:2 * A_WIDTH])), veca_ref[0:1, :], veca_ref[1:2, :])
    v_ref[...] = v
    for t in range(n_t):
        mix = jnp.broadcast_to(bs_ref[t:t + 1, :], (n_seq, A_WIDTH))
        for s in range(t + 1):
            mix = mix + wexp_ref[t, s:s + 1, :] * v[rows(s)]
        mixin[rows(t), 0:A_WIDTH] = (u[rows(t)] * mix).astype(BF16)

    o = 2 * A_WIDTH
    glu = _dot(hb, w_in_ref[:, o:o + B_WIDTH]) * jax.nn.sigmoid(_dot(hb, w_in_ref[:, o + B_WIDTH:o + 2 * B_WIDTH]))
    past_b = CONV_B_TAPS - 1
    xp = lambda m: sb_ref[m] if m < past_b else glu[rows(m - past_b)]
    for t in range(n_t):
        acc = jnp.broadcast_to(veca_ref[2:3, :], (n_seq, B_WIDTH))
        for k in range(CONV_B_TAPS):
            acc = acc + cbw_ref[k:k + 1, :] * xp(t + k)
        yb = _silu(_ln(acc, veca_ref[3:4, :], veca_ref[4:5, :]))
        mixin[rows(t), A_WIDTH:A_WIDTH + B_WIDTH] = yb.astype(BF16)
    for r in range(past_b):
        nsb_ref[r] = xp(r + n_t)

    o = 2 * A_WIDTH + 2 * B_WIDTH
    bg = _dot(hb, w_in_ref[:, o:o + C_WIDTH])
    xc = _dot(hb, w_in_ref[:, o + C_WIDTH:o + 2 * C_WIDTH]) * _dot(hb, w_in_ref[:, o + 2 * C_WIDTH:o + 3 * C_WIDTH])
    past_c = CONV_C_TAPS - 1
    xq = lambda m: sc_ref[m] if m < past_c else xc[rows(m - past_c)]
    for t in range(n_t):
        cc = ccw_ref[0:1, :] * xq(t)
        for k in range(1, CONV_C_TAPS):
            cc = cc + ccw_ref[k:k + 1, :] * xq(t + k)
        mixin[rows(t), A_WIDTH + B_WIDTH:] = (bg[rows(t)] * cc).astype(BF16)
    for r in range(past_c):
        nsc_ref[r] = xq(r + n_t)

    _post_mix(h, mixin[...], w_out_ref, vecd_ref, wr_ref, alpha, h1_ref, xs_ref, info_ref, tab_ref)


def _mixer_sample(h, sb_t, sc_t, lw, first, alpha):
    m, d = h.shape
    n_seq = sb_t.shape[1]
    n_t = m // n_seq
    nw = m // WIN
    full = lambda a: pl.BlockSpec(a.shape, lambda i: (0,) * a.ndim)
    ins = [h, sb_t, sc_t, lw["w_in"], lw["wexp"], lw["bs_exp"], lw["veca"], lw["cbw"], lw["ccw"],
           lw["w_out"], lw["vecd"], lw["wr"]]
    out_shape = (
        jax.ShapeDtypeStruct((m, d), F32),
        jax.ShapeDtypeStruct((m, A_WIDTH), F32),
        jax.ShapeDtypeStruct(sb_t.shape, F32),
        jax.ShapeDtypeStruct(sc_t.shape, F32),
        jax.ShapeDtypeStruct((nw, WIN_ROWS, d), BF16),
        jax.ShapeDtypeStruct((nw, WIN, LANES), F32),
        jax.ShapeDtypeStruct((nw, SUBLANES, LANES), jnp.int32),
    )
    return pl.pallas_call(
        functools.partial(_mixer_sample_body, first, alpha, n_seq, n_t),
        grid=(1,),
        in_specs=[full(a) for a in ins],
        out_specs=tuple(full(o) for o in out_shape),
        out_shape=out_shape,
        scratch_shapes=[pltpu.VMEM((m, d), BF16)],
        compiler_params=pltpu.CompilerParams(
            dimension_semantics=("arbitrary",), vmem_limit_bytes=VMEM_LIMIT),
        name="mixer_sample",
    )(*ins)


def _experts_body(tile_e_ref, fresh_ref, nvalid_ref, clist_ref, ntiles_ref,
                  xs_hbm, wg_ref, wu_ref, wd_ref, ys_hbm,
                  lhs, obuf, wgb, wub, wdb, sem_in, sem_out):
    del tile_e_ref
    t = pl.program_id(0)
    nt = ntiles_ref[0]

    def copy_in(cid, slot, j):
        return pltpu.make_async_copy(xs_hbm.at[cid], lhs.at[slot, pl.ds(j * CH, CH)], sem_in.at[slot])

    def copy_out(cid, slot, j):
        return pltpu.make_async_copy(obuf.at[slot, pl.ds(j * CH, CH)], ys_hbm.at[cid], sem_out.at[slot])

    def start_all(make, tt, slot):
        n = nvalid_ref[tt]
        base = tt * TILE_CHUNKS
        groups = n // DMA_UNROLL

        def group(g, carry):
            for k in range(DMA_UNROLL):
                j = g * DMA_UNROLL + k
                make(clist_ref[base + j], slot, j).start()
            return carry

        def single(j, carry):
            make(clist_ref[base + j], slot, j).start()
            return carry

        lax.fori_loop(0, groups, group, 0)
        lax.fori_loop(groups * DMA_UNROLL, n, single, 0)

    def wait_all(make, tt, slot):
        def body(j, carry):
            make(0, slot, j).wait()
            return carry
        lax.fori_loop(0, nvalid_ref[tt], body, 0)

    def start_full(make, tt, slot):
        for j in range(TILE_CHUNKS):
            make(clist_ref[tt * TILE_CHUNKS + j], slot, j).start()

    @pl.when(t == 0)
    def _():
        lhs[...] = jnp.zeros(lhs.shape, lhs.dtype)
        start_all(copy_in, 0, 0)

    @pl.when(fresh_ref[t] == 1)
    def _():
        wgb[...] = wg_ref[0].astype(BF16)
        wub[...] = wu_ref[0].astype(BF16)
        wdb[...] = wd_ref[0].astype(BF16)

    @pl.when(t < nt)
    def _():
        slot = t % 2
        other = 1 - slot
        wait_all(copy_in, t, slot)

        @pl.when(t >= 2)
        def _():
            wait_all(copy_out, t - 2, slot)

        def ffn(rows):
            x = lhs[slot, 0:rows, :]
            act = _silu(_dot(x, wgb[...])) * _dot(x, wub[...])
            obuf[slot, 0:rows, :] = _dot(act.astype(BF16), wdb[...]).astype(BF16)

        t_next = jnp.minimum(t + 1, nt - 1)
        t_prev = jnp.maximum(t - 1, 0)
        fast = ((t + 1 < nt) & (t >= 1) & (nvalid_ref[t_next] == TILE_CHUNKS)
                & (nvalid_ref[t_prev] == TILE_CHUNKS) & (nvalid_ref[t] > SUB_ROWS // CH))

        @pl.when(fast)
        def _():
            start_full(copy_in, t + 1, other)
            start_full(copy_out, t - 1, other)
            ffn(TILE_ROWS)

        @pl.when(jnp.logical_not(fast))
        def _():
            @pl.when(t + 1 < nt)
            def _():
                start_all(copy_in, t + 1, other)

            @pl.when(t >= 1)
            def _():
                start_all(copy_out, t - 1, other)

            @pl.when(nvalid_ref[t] > SUB_ROWS // CH)
            def _():
                ffn(TILE_ROWS)

            @pl.when(nvalid_ref[t] <= SUB_ROWS // CH)
            def _():
                ffn(SUB_ROWS)

        @pl.when(t == nt - 1)
        def _():
            start_all(copy_out, t, slot)

    @pl.when(t == pl.num_programs(0) - 1)
    def _():
        for back in (2, 1):
            @pl.when(nt >= back)
            def _():
                wait_all(copy_out, nt - back, (nt - back) % 2)


def _experts(xs, tile_e, fresh, nvalid, clist, ntiles, wg, wu, wd, layer, t_max):
    n_win, _, d = xs.shape
    xc = xs.reshape(n_win * WIN_CHUNKS, CH, d)
    de = wg.shape[-1]
    anyspec = pl.BlockSpec(memory_space=pl.ANY)
    n_prefetch = 5
    by_expert = lambda t, te, fr, nv, cl, n: (layer * N_EXPERTS + te[t], 0, 0)
    grid_spec = pltpu.PrefetchScalarGridSpec(
        num_scalar_prefetch=n_prefetch,
        grid=(t_max,),
        in_specs=[
            anyspec,
            pl.BlockSpec((1, d, de), by_expert),
            pl.BlockSpec((1, d, de), by_expert),
            pl.BlockSpec((1, de, d), by_expert),
        ],
        out_specs=anyspec,
        scratch_shapes=[
            pltpu.VMEM((2, TILE_ROWS, d), BF16),
            pltpu.VMEM((2, TILE_ROWS, d), BF16),
            pltpu.VMEM((d, de), BF16),
            pltpu.VMEM((d, de), BF16),
            pltpu.VMEM((de, d), BF16),
            pltpu.SemaphoreType.DMA((2,)),
            pltpu.SemaphoreType.DMA((2,)),
        ],
    )
    ys = pl.pallas_call(
        _experts_body,
        grid_spec=grid_spec,
        out_shape=jax.ShapeDtypeStruct(xc.shape, xc.dtype),
        input_output_aliases={n_prefetch: 0},
        compiler_params=pltpu.CompilerParams(
            dimension_semantics=("arbitrary",), vmem_limit_bytes=VMEM_LIMIT),
        name="experts",
    )(tile_e, fresh, nvalid, clist, ntiles, xc, wg, wu, wd)
    return ys.reshape(xs.shape)


def _expert_tables(tab, t_max):
    i32 = jnp.int32
    nch = tab[:, 0, :N_EXPERTS]
    off = tab[:, 1, :N_EXPERTS]
    n_win = nch.shape[0]
    cum_incl = jnp.cumsum(nch, axis=0)
    cum_excl = cum_incl - nch
    ce = cum_incl[-1]
    te = (ce + TILE_CHUNKS - 1) // TILE_CHUNKS
    tile_end = jnp.cumsum(te)
    tile_start = tile_end - te
    nt = tile_end[-1]
    t_idx = jnp.arange(t_max, dtype=i32)
    tile_e = jnp.minimum(jnp.sum((tile_end[None, :] <= t_idx[:, None]).astype(i32), axis=1), N_EXPERTS - 1)
    sel_e = (tile_e[:, None] == jnp.arange(N_EXPERTS, dtype=i32)[None, :]).astype(i32)
    pick = lambda v: jnp.sum(sel_e * v[None, :], axis=1)
    q0 = (t_idx - pick(tile_start)) * TILE_CHUNKS
    nvalid = jnp.where(t_idx < nt, jnp.clip(pick(ce) - q0, 0, TILE_CHUNKS), 0).astype(i32)
    q = q0[:, None] + jnp.arange(TILE_CHUNKS, dtype=i32)[None, :]
    by_win = lambda m: jnp.sum(sel_e[:, :, None] * m.T[None, :, :], axis=1)
    cum_e = by_win(cum_incl)
    w = jnp.minimum(jnp.sum((cum_e[:, None, :] <= q[:, :, None]).astype(i32), axis=-1), n_win - 1)
    sel_w = (w[:, :, None] == jnp.arange(n_win, dtype=i32)[None, None, :]).astype(i32)
    shift = jnp.sum(sel_w * by_win(off - cum_excl)[:, None, :], axis=-1)
    cid = w * WIN_CHUNKS + shift + q
    valid = jnp.arange(TILE_CHUNKS, dtype=i32)[None, :] < nvalid[:, None]
    clist = jnp.where(valid, cid, 0).astype(i32).reshape(-1)
    prev_e = jnp.concatenate([jnp.full((1,), -1, i32), tile_e[:-1]])
    fresh = ((t_idx < nt) & (tile_e != prev_e)).astype(i32)
    return tile_e.astype(i32), fresh, nvalid, clist, nt.astype(i32).reshape(1)


def _combine_body(alpha, h1_ref, info_ref, ys_ref, vecd_ref, out_ref):
    for wi in range(info_ref.shape[0]):
        info = info_ref[wi]
        col = _iota((WIN, WIN_ROWS), 1).astype(F32)
        pt = (jnp.where(col == info[:, 0:1], info[:, 2:3], 0.0)
              + jnp.where(col == info[:, 1:2], info[:, 3:4], 0.0)).astype(BF16)
        y = _dot(pt, ys_ref[wi])
        rs = slice(wi * WIN, (wi + 1) * WIN)
        out_ref[rs, :] = _ln(alpha * h1_ref[rs, :] + y, vecd_ref[2:3, :], vecd_ref[3:4, :])


def _combine(h1, info, ys, vecd, alpha, wpb, block0):
    m, d = h1.shape
    win_map = lambda i: (block0 + i, 0, 0)
    return pl.pallas_call(
        functools.partial(_combine_body, alpha),
        grid=(m // (wpb * WIN),),
        in_specs=[
            pl.BlockSpec((wpb * WIN, d), lambda i: (i, 0)),
            pl.BlockSpec((wpb, WIN, LANES), win_map),
            pl.BlockSpec((wpb, WIN_ROWS, d), win_map),
            pl.BlockSpec(vecd.shape, lambda i: (0, 0)),
        ],
        out_specs=pl.BlockSpec((wpb * WIN, d), lambda i: (i, 0)),
        out_shape=jax.ShapeDtypeStruct((m, d), F32),
        compiler_params=pltpu.CompilerParams(
            dimension_semantics=("arbitrary",), vmem_limit_bytes=VMEM_LIMIT),
        name="combine",
    )(h1, info, ys, vecd)


def kernel(x_prompt, x_sample, state_conv_b, state_conv_c, ln_in_g, ln_in_b, w_in, w_s, b_s, ln_v_g, ln_v_b, conv_b_w, conv_b_bias, ln_conv_g, ln_conv_b, conv_c_w, w_out, ln1_g, ln1_b, w_router_group, w_router_expert, w_gate, w_up, w_down, ln2_g, ln2_b):
    depth = w_in.shape[0]
    nb, t, d = x_prompt.shape
    ns, nt_s, _ = x_sample.shape
    assert t % TM == 0 and (ns * nt_s) % WIN == 0 and TM % WIN == 0 and d % LANES == 0
    alpha = (2.0 * depth) ** 0.25
    n_win_p = nb * t // WIN
    n_win_s = ns * nt_s // WIN
    n_win = n_win_p + n_win_s
    assert n_win_p % n_win_s == 0 and n_win_s % (TM // WIN) == 0
    max_chunks = n_win * (2 * WIN // CH + N_EXPERTS)
    t_max = max_chunks // TILE_CHUNKS + N_EXPERTS

    hp = x_prompt
    hs = jnp.transpose(x_sample, (1, 0, 2)).reshape(nt_s * ns, d)
    sb_t = jnp.transpose(state_conv_b, (0, 2, 1, 3))
    sc_t = jnp.transpose(state_conv_c, (0, 2, 1, 3))
    cb_p, cc_p, cb_s, cc_s, v_s = [], [], [], [], []
    for l in range(depth):
        wr = jnp.concatenate([w_router_group[l], w_router_expert[l].reshape(d, N_EXPERTS)], axis=1)
        wr = jnp.pad(wr, ((0, 0), (0, LANES - wr.shape[1]))).astype(BF16)
        zeros_a = jnp.zeros((A_WIDTH,), F32)
        zeros_d = jnp.zeros((d,), F32)
        lw = dict(
            w_in=w_in[l].astype(BF16),
            ws2=w_s[l].reshape(A_WIDTH // LANES, 2 * GMLP_CHUNK, GMLP_CHUNK),
            wexp=jnp.repeat(jnp.transpose(w_s[l][:, :nt_s, :nt_s], (1, 2, 0)), HEAD_DIM, axis=-1),
            bs_exp=jnp.repeat(b_s[l].T, HEAD_DIM, axis=-1),
            veca=jnp.stack([ln_v_g[l], ln_v_b[l], conv_b_bias[l], ln_conv_g[l], ln_conv_b[l],
                            zeros_a, zeros_a, zeros_a]),
            cbw=conv_b_w[l], ccw=conv_c_w[l],
            w_out=w_out[l].astype(BF16),
            vecd=jnp.stack([ln1_g[l], ln1_b[l], ln2_g[l], ln2_b[l], ln_in_g, ln_in_b, zeros_d, zeros_d]),
            wr=wr,
        )
        h1s, vs, nbs, ncs, xs_s, info_s, tab_s = _mixer_sample(hs, sb_t[l], sc_t[l], lw, l == 0, alpha)
        h1p, xs, info, tab, nbp, ncp = _mixer_prompt(hp, lw, l == 0, alpha, xs_s, info_s, tab_s)
        tile_e, fresh, nvalid, clist, ntiles = _expert_tables(tab, t_max)
        ne = N_EXPERTS
        ys = _experts(xs, tile_e, fresh, nvalid, clist, ntiles, w_gate.reshape(depth * ne, d, -1),
                      w_up.reshape(depth * ne, d, -1), w_down.reshape(depth * ne, -1, d), l, t_max)
        hp = _combine(h1p.reshape(nb * t, d), info, ys, lw["vecd"], alpha, TM // WIN, 0).reshape(nb, t, d)
        hs = _combine(h1s, info, ys, lw["vecd"], alpha, n_win_s, n_win_p // n_win_s)
        cb_p.append(nbp)
        cc_p.append(ncp)
        cb_s.append(nbs)
        cc_s.append(ncs)
        v_s.append(vs)
    y_sample = jnp.transpose(hs.reshape(nt_s, ns, d), (1, 0, 2))
    untime = lambda xs_: jnp.transpose(jnp.stack(xs_), (0, 2, 1, 3))
    chunk_v = jnp.transpose(jnp.stack(v_s).reshape(depth, nt_s, ns, A_WIDTH), (0, 2, 1, 3))
    return (hp, y_sample, jnp.stack(cb_p), jnp.stack(cc_p), untime(cb_s), untime(cc_s), chunk_v)
```

```python
import functools
import math

import jax
import jax.numpy as jnp
from jax import lax
from jax.experimental import pallas as pl
from jax.experimental.pallas import tpu as pltpu

F32 = jnp.float32
BF16 = jnp.bfloat16

HEAD_DIM = 64
A_WIDTH = 384
B_WIDTH = 384
C_WIDTH = 256
GMLP_CHUNK = 128
CONV_B_TAPS = 31
CONV_C_TAPS = 3
N_GROUPS = 4
EXPERTS_PER_GROUP = 8
N_EXPERTS = N_GROUPS * EXPERTS_PER_GROUP
LN_EPS = 1e-5
INV_SQRT2 = 1.0 / math.sqrt(2.0)

LANES = 128
SUBLANES = 8
WIN = 256
CH = 16
WIN_ROWS = -(-(2 * WIN + N_EXPERTS * (CH - 1)) // WIN) * WIN
WIN_CHUNKS = WIN_ROWS // CH
TILE_CHUNKS = 32
TILE_ROWS = TILE_CHUNKS * CH
SUB_ROWS = 256
TM = 512
CONV_RB = 32
DMA_UNROLL = 4
VMEM_LIMIT = 56 * 1024 * 1024


def _ln(x, g, b):
    mu = jnp.mean(x, axis=-1, keepdims=True)
    xc = x - mu
    var = jnp.mean(xc * xc, axis=-1, keepdims=True)
    return xc * lax.rsqrt(var + LN_EPS) * g + b


def _gelu(x):
    return 0.5 * x * (1.0 + lax.erf(x * INV_SQRT2))


def _silu(x):
    return x * jax.nn.sigmoid(x)


def _dot(a, b):
    return jnp.dot(a, b, preferred_element_type=F32)


def _iota(shape, dim):
    return lax.broadcasted_iota(jnp.int32, shape, dim)


def _route_window(logits, hb):
    w = logits.shape[0]
    lane = _iota((w, LANES), 1)
    neg = jnp.float32(-jnp.inf)
    gmask = lane < N_GROUPS
    gl = jnp.where(gmask, logits, neg)
    gmax = jnp.max(gl, axis=-1, keepdims=True)
    gsel = jnp.min(jnp.where(gl == gmax, lane, LANES), axis=-1, keepdims=True)
    den = jnp.sum(jnp.where(gmask, jnp.exp(gl - gmax), 0.0), axis=-1, keepdims=True)
    gw = 1.0 / den
    lo = N_GROUPS + EXPERTS_PER_GROUP * gsel
    el = jnp.where((lane >= lo) & (lane < lo + EXPERTS_PER_GROUP), logits, neg)
    v1 = jnp.max(el, axis=-1, keepdims=True)
    i1 = jnp.min(jnp.where(el == v1, lane, LANES), axis=-1, keepdims=True)
    el2 = jnp.where(lane == i1, neg, el)
    v2 = jnp.max(el2, axis=-1, keepdims=True)
    i2 = jnp.min(jnp.where(el2 == v2, lane, LANES), axis=-1, keepdims=True)
    e2x = jnp.exp(v2 - v1)
    w1 = gw / (1.0 + e2x)
    w2 = gw * e2x / (1.0 + e2x)
    e1 = i1 - N_GROUPS
    e2 = i2 - N_GROUPS

    oh = (lane == e1) | (lane == e2)
    ohf = jnp.where(oh, 1.0, 0.0)
    lstrict = jnp.where(_iota((w, w), 0) > _iota((w, w), 1), 1.0, 0.0).astype(BF16)
    rank = _dot(lstrict, ohf.astype(BF16))
    cnt = jnp.sum(ohf, axis=0, keepdims=True)
    nch = jnp.floor((cnt + (CH - 1.0)) * (1.0 / CH))
    upper = jnp.where(_iota((LANES, LANES), 0) < _iota((LANES, LANES), 1), 1.0, 0.0).astype(BF16)
    off = _dot(jnp.broadcast_to(nch, (SUBLANES, LANES)).astype(BF16), upper)[0:1]
    dest = off * CH + rank
    r1 = jnp.sum(jnp.where(lane == e1, dest, 0.0), axis=-1, keepdims=True)
    r2 = jnp.sum(jnp.where(lane == e2, dest, 0.0), axis=-1, keepdims=True)
    info = jnp.where(lane == 0, r1, jnp.where(lane == 1, r2,
                     jnp.where(lane == 2, w1, jnp.where(lane == 3, w2, 0.0))))
    info_t = info.T
    rows = _iota((WIN_ROWS, w), 0).astype(F32)
    perm = jnp.where((rows == info_t[0:1, :]) | (rows == info_t[1:2, :]), 1.0, 0.0).astype(BF16)
    xs = _dot(perm, hb).astype(BF16)
    sub = _iota((SUBLANES, LANES), 0)
    tab = jnp.where(sub == 0, nch, jnp.where(sub == 1, off, 0.0)).astype(jnp.int32)
    return xs, info, tab


def _post_mix(h, mixin, w_out_ref, vecd_ref, wr_ref, alpha, h1_ref, xs_ref, info_ref, tab_ref):
    mix_out = _dot(mixin, w_out_ref[...])
    h1 = _ln(alpha * h + mix_out, vecd_ref[0:1, :], vecd_ref[1:2, :])
    h1_ref[...] = h1
    hb = h1.astype(BF16)
    logits = _dot(hb, wr_ref[...])
    for wi in range(h.shape[0] // WIN):
        rs = slice(wi * WIN, (wi + 1) * WIN)
        xs, info, tab = _route_window(logits[rs], hb[rs])
        xs_ref[wi] = xs
        info_ref[wi] = info
        tab_ref[wi] = tab


def _mixer_prompt_body(first, alpha, nj, n_tiles, h_ref, w_in_ref, ws_ref, bs_ref, veca_ref, cbw_ref,
                       ccw_ref, w_out_ref, vecd_ref, wr_ref, xs_s_ref, info_s_ref, tab_s_ref,
                       h1_ref, xs_ref, info_ref, tab_ref, nsb_ref, nsc_ref,
                       xsh, cbuf, mixin, wb, cbs):
    i = pl.program_id(0)

    @pl.when(i < n_tiles)
    def _():
        _mixer_prompt_tile(first, alpha, i % nj, nj - 1, h_ref, w_in_ref, ws_ref, bs_ref, veca_ref,
                           cbw_ref, ccw_ref, w_out_ref, vecd_ref, wr_ref,
                           h1_ref, xs_ref, info_ref, tab_ref, nsb_ref, nsc_ref,
                           xsh, cbuf, mixin, wb, cbs)

    @pl.when(i >= n_tiles)
    def _():
        xs_ref[...] = xs_s_ref[...]
        info_ref[...] = info_s_ref[...]
        tab_ref[...] = tab_s_ref[...]


def _mixer_prompt_tile(first, alpha, j, last, h_ref, w_in_ref, ws_ref, bs_ref, veca_ref, cbw_ref, ccw_ref,
                       w_out_ref, vecd_ref, wr_ref,
                       h1_ref, xs_ref, info_ref, tab_ref, nsb_ref, nsc_ref,
                       xsh, cbuf, mixin, wb, cbs):
    past_b = CONV_B_TAPS - 1
    past_c = CONV_C_TAPS - 1
    b0 = 32 - past_b
    c0 = 8 - past_c
    n_half = TM // WIN
    oa, ob = 2 * A_WIDTH, 2 * A_WIDTH + 2 * B_WIDTH

    @pl.when(j == 0)
    def _():
        xsh[0, 0:32, :] = jnp.zeros((32, B_WIDTH), F32)
        cbuf[0:8, :] = jnp.zeros((8, C_WIDTH), F32)
        for k in range(CONV_B_TAPS):
            wb[k] = jnp.broadcast_to(cbw_ref[k:k + 1, :], (SUBLANES, B_WIDTH))

    tri = (_iota((2 * GMLP_CHUNK, GMLP_CHUNK), 0) % GMLP_CHUNK) >= _iota((2 * GMLP_CHUNK, GMLP_CHUNK), 1)
    wms = [jnp.where(tri, ws_ref[p], 0.0).astype(BF16) for p in range(A_WIDTH // LANES)]
    lane = _iota((GMLP_CHUNK, LANES), 1)

    hs, bgs = [], []
    for hf in range(n_half):
        r0 = hf * WIN
        h = h_ref[0, r0:r0 + WIN, :]
        if first:
            h = _ln(h, vecd_ref[4:5, :], vecd_ref[5:6, :])
        hs.append(h)
        hb = h.astype(BF16)
        za = _dot(hb, w_in_ref[:, 0:oa])
        zb = _dot(hb, w_in_ref[:, oa:ob])
        zc = _dot(hb, w_in_ref[:, ob:ob + 3 * C_WIDTH])

        ga = _gelu(za)
        u = ga[:, 0:A_WIDTH]
        vb = _ln(ga[:, A_WIDTH:], veca_ref[0:1, :], veca_ref[1:2, :]).astype(BF16)
        for c in range(WIN // GMLP_CHUNK):
            rs = slice(c * GMLP_CHUNK, (c + 1) * GMLP_CHUNK)
            parts = []
            for p in range(A_WIDTH // LANES):
                ab = _dot(wms[p], vb[rs, p * LANES:(p + 1) * LANES])
                parts.append(jnp.where(lane < HEAD_DIM, ab[:GMLP_CHUNK], ab[GMLP_CHUNK:]))
            mix = jnp.concatenate(parts, axis=1) + bs_ref[...]
            mixin[r0 + c * GMLP_CHUNK:r0 + (c + 1) * GMLP_CHUNK, 0:A_WIDTH] = (u[rs] * mix).astype(BF16)

        xsh[0, 32 + r0:32 + r0 + WIN, :] = zb[:, 0:B_WIDTH] * jax.nn.sigmoid(zb[:, B_WIDTH:])
        lo = 0 if hf == 0 else r0 + 24
        for r in range(1, SUBLANES):
            xsh[r, lo:r0 + WIN + 24, :] = xsh[0, lo + r:r0 + WIN + 24 + r, :]

        bgs.append(zc[:, 0:C_WIDTH])
        cbuf[8 + r0:8 + r0 + WIN, :] = zc[:, C_WIDTH:2 * C_WIDTH] * zc[:, 2 * C_WIDTH:]

    n_sub = CONV_RB // SUBLANES
    for hf in range(n_half):
        r0 = hf * WIN
        for rb in range(WIN // CONV_RB):
            base = r0 + rb * CONV_RB
            accs = [jnp.broadcast_to(veca_ref[2:3, :], (SUBLANES, B_WIDTH))] * n_sub
            for k in range(CONV_B_TAPS):
                s = k + b0
                w8 = wb[k]
                for a in range(n_sub):
                    row = base + (s // SUBLANES + a) * SUBLANES
                    accs[a] = accs[a] + w8 * xsh[s % SUBLANES, row:row + SUBLANES, :]
            for a in range(n_sub):
                cbs[base + a * SUBLANES:base + (a + 1) * SUBLANES, :] = accs[a]
        yb = _silu(_ln(cbs[r0:r0 + WIN, :], veca_ref[3:4, :], veca_ref[4:5, :]))
        mixin[r0:r0 + WIN, A_WIDTH:A_WIDTH + B_WIDTH] = yb.astype(BF16)

        cc = ccw_ref[0:1, :] * cbuf[c0 + r0:c0 + r0 + WIN, :]
        for k in range(1, CONV_C_TAPS):
            cc = cc + ccw_ref[k:k + 1, :] * cbuf[c0 + r0 + k:c0 + r0 + k + WIN, :]
        mixin[r0:r0 + WIN, A_WIDTH + B_WIDTH:] = (bgs[hf] * cc).astype(BF16)

        _post_mix(hs[hf], mixin[r0:r0 + WIN, :], w_out_ref, vecd_ref, wr_ref, alpha,
                  h1_ref.at[0, r0:r0 + WIN], xs_ref.at[hf:hf + 1], info_ref.at[hf:hf + 1],
                  tab_ref.at[hf:hf + 1])

    new_b = xsh[0, TM + b0:TM + 32, :]
    xsh[0, b0:32, :] = new_b
    new_c = cbuf[TM + c0:TM + 8, :]
    cbuf[c0:8, :] = new_c

    @pl.when(j == last)
    def _():
        nsb_ref[0] = new_b
        nsc_ref[0] = new_c


def _mixer_prompt(h, lw, first, alpha, xs_s, info_s, tab_s):
    nb, t, d = h.shape
    nj = t // TM
    wpt = TM // WIN
    n_tiles = nb * nj
    n_extra = xs_s.shape[0] // wpt
    n_win_total = (n_tiles + n_extra) * wpt
    const = lambda shape: pl.BlockSpec(shape, lambda i: (0,) * len(shape))
    tile = lambda i: jnp.minimum(i, n_tiles - 1)
    h_map = lambda i: (tile(i) // nj, tile(i) % nj, 0)
    state_map = lambda i: (tile(i) // nj, 0, 0)
    extra_map = lambda i: (jnp.maximum(i - n_tiles, 0), 0, 0)
    out_shape = (
        jax.ShapeDtypeStruct((nb, t, d), F32),
        jax.ShapeDtypeStruct((n_win_total, WIN_ROWS, d), BF16),
        jax.ShapeDtypeStruct((n_win_total, WIN, LANES), F32),
        jax.ShapeDtypeStruct((n_win_total, SUBLANES, LANES), jnp.int32),
        jax.ShapeDtypeStruct((nb, CONV_B_TAPS - 1, B_WIDTH), F32),
        jax.ShapeDtypeStruct((nb, CONV_C_TAPS - 1, C_WIDTH), F32),
    )
    win_map = lambda i: (i, 0, 0)
    return pl.pallas_call(
        functools.partial(_mixer_prompt_body, first, alpha, nj, n_tiles),
        grid=(n_tiles + n_extra,),
        in_specs=[
            pl.BlockSpec((1, TM, d), h_map),
            const(lw["w_in"].shape), const(lw["ws2"].shape), const(lw["bs_exp"].shape),
            const(lw["veca"].shape), const(lw["cbw"].shape), const(lw["ccw"].shape),
            const(lw["w_out"].shape), const(lw["vecd"].shape), const(lw["wr"].shape),
            pl.BlockSpec((wpt, WIN_ROWS, d), extra_map),
            pl.BlockSpec((wpt, WIN, LANES), extra_map),
            pl.BlockSpec((wpt, SUBLANES, LANES), extra_map),
        ],
        out_specs=(
            pl.BlockSpec((1, TM, d), h_map),
            pl.BlockSpec((wpt, WIN_ROWS, d), win_map),
            pl.BlockSpec((wpt, WIN, LANES), win_map),
            pl.BlockSpec((wpt, SUBLANES, LANES), win_map),
            pl.BlockSpec((1, CONV_B_TAPS - 1, B_WIDTH), state_map),
            pl.BlockSpec((1, CONV_C_TAPS - 1, C_WIDTH), state_map),
        ),
        out_shape=out_shape,
        scratch_shapes=[
            pltpu.VMEM((SUBLANES, TM + 32, B_WIDTH), F32),
            pltpu.VMEM((TM + 8, C_WIDTH), F32),
            pltpu.VMEM((TM, d), BF16),
            pltpu.VMEM((CONV_B_TAPS, SUBLANES, B_WIDTH), F32),
            pltpu.VMEM((TM, B_WIDTH), F32),
        ],
        compiler_params=pltpu.CompilerParams(
            dimension_semantics=("arbitrary",), vmem_limit_bytes=VMEM_LIMIT),
        name="mixer_prompt",
    )(h, lw["w_in"], lw["ws2"], lw["bs_exp"], lw["veca"], lw["cbw"], lw["ccw"],
      lw["w_out"], lw["vecd"], lw["wr"], xs_s, info_s, tab_s)


def _mixer_sample_body(first, alpha, n_seq, n_t,
                       h_ref, sb_ref, sc_ref, w_in_ref, wexp_ref, bs_ref, veca_ref, cbw_ref, ccw_ref,
                       w_out_ref, vecd_ref, wr_ref,
                       h1_ref, v_ref, nsb_ref, nsc_ref, xs_ref, info_ref, tab_ref,
                       mixin):
    h = h_ref[...]
    if first:
        h = _ln(h, vecd_ref[4:5, :], vecd_ref[5:6, :])
    hb = h.astype(BF16)
    rows = lambda t: slice(t * n_seq, (t + 1) * n_seq)

    u = _gelu(_dot(hb, w_in_ref[:, 0:A_WIDTH]))
    v = _ln(_gelu(_dot(hb, w_in_ref[:, A_WIDTH:2 * A_WIDTH])), veca_ref[0:1, :], veca_ref[1:2, :])
    v_ref[...] = v
    for t in range(n_t):
        mix = jnp.broadcast_to(bs_ref[t:t + 1, :], (n_seq, A_WIDTH))
        for s in range(t + 1):
            mix = mix + wexp_ref[t, s:s + 1, :] * v[rows(s)]
        mixin[rows(t), 0:A_WIDTH] = (u[rows(t)] * mix).astype(BF16)

    o = 2 * A_WIDTH
    glu = _dot(hb, w_in_ref[:, o:o + B_WIDTH]) * jax.nn.sigmoid(_dot(hb, w_in_ref[:, o + B_WIDTH:o + 2 * B_WIDTH]))
    past_b = CONV_B_TAPS - 1
    xp = lambda m: sb_ref[m] if m < past_b else glu[rows(m - past_b)]
    for t in range(n_t):
        acc = jnp.broadcast_to(veca_ref[2:3, :], (n_seq, B_WIDTH))
        for k in range(CONV_B_TAPS):
            acc = acc + cbw_ref[k:k + 1, :] * xp(t + k)
        yb = _silu(_ln(acc, veca_ref[3:4, :], veca_ref[4:5, :]))
        mixin[rows(t), A_WIDTH:A_WIDTH + B_WIDTH] = yb.astype(BF16)
    for r in range(past_b):
        nsb_ref[r] = xp(r + n_t)

    o = 2 * A_WIDTH + 2 * B_WIDTH
    bg = _dot(hb, w_in_ref[:, o:o + C_WIDTH])
    xc = _dot(hb, w_in_ref[:, o + C_WIDTH:o + 2 * C_WIDTH]) * _dot(hb, w_in_ref[:, o + 2 * C_WIDTH:o + 3 * C_WIDTH])
    past_c = CONV_C_TAPS - 1
    xq = lambda m: sc_ref[m] if m < past_c else xc[rows(m - past_c)]
    for t in range(n_t):
        cc = ccw_ref[0:1, :] * xq(t)
        for k in range(1, CONV_C_TAPS):
            cc = cc + ccw_ref[k:k + 1, :] * xq(t + k)
        mixin[rows(t), A_WIDTH + B_WIDTH:] = (bg[rows(t)] * cc).astype(BF16)
    for r in range(past_c):
        nsc_ref[r] = xq(r + n_t)

    _post_mix(h, mixin[...], w_out_ref, vecd_ref, wr_ref, alpha, h1_ref, xs_ref, info_ref, tab_ref)


def _mixer_sample(h, sb_t, sc_t, lw, first, alpha):
    m, d = h.shape
    n_seq = sb_t.shape[1]
    n_t = m // n_seq
    nw = m // WIN
    full = lambda a: pl.BlockSpec(a.shape, lambda i: (0,) * a.ndim)
    ins = [h, sb_t, sc_t, lw["w_in"], lw["wexp"], lw["bs_exp"], lw["veca"], lw["cbw"], lw["ccw"],
           lw["w_out"], lw["vecd"], lw["wr"]]
    out_shape = (
        jax.ShapeDtypeStruct((m, d), F32),
        jax.ShapeDtypeStruct((m, A_WIDTH), F32),
        jax.ShapeDtypeStruct(sb_t.shape, F32),
        jax.ShapeDtypeStruct(sc_t.shape, F32),
        jax.ShapeDtypeStruct((nw, WIN_ROWS, d), BF16),
        jax.ShapeDtypeStruct((nw, WIN, LANES), F32),
        jax.ShapeDtypeStruct((nw, SUBLANES, LANES), jnp.int32),
    )
    return pl.pallas_call(
        functools.partial(_mixer_sample_body, first, alpha, n_seq, n_t),
        grid=(1,),
        in_specs=[full(a) for a in ins],
        out_specs=tuple(full(o) for o in out_shape),
        out_shape=out_shape,
        scratch_shapes=[pltpu.VMEM((m, d), BF16)],
        compiler_params=pltpu.CompilerParams(
            dimension_semantics=("arbitrary",), vmem_limit_bytes=VMEM_LIMIT),
        name="mixer_sample",
    )(*ins)


def _experts_body(layer, tile_e_ref, fresh_ref, wslot_ref, next_e_ref, nvalid_ref, clist_ref, ntiles_ref,
                  xs_hbm, wg_hbm, wu_hbm, wd_hbm, ys_hbm,
                  lhs, obuf, sg, su, sd, wgb, wub, wdb, sem_in, sem_out, sem_w):
    nt = ntiles_ref[0]

    def copy_in(cid, slot, j):
        return pltpu.make_async_copy(xs_hbm.at[cid], lhs.at[slot, pl.ds(j * CH, CH)], sem_in.at[slot])

    def copy_out(cid, slot, j):
        return pltpu.make_async_copy(obuf.at[slot, pl.ds(j * CH, CH)], ys_hbm.at[cid], sem_out.at[slot])

    def weight_copies(e, ws):
        row = layer * N_EXPERTS + e
        return [pltpu.make_async_copy(src.at[row], dst.at[ws], sem_w.at[ws])
                for src, dst in ((wg_hbm, sg), (wu_hbm, su), (wd_hbm, sd))]

    def start_all(make, tt, slot):
        n = nvalid_ref[tt]
        base = tt * TILE_CHUNKS
        groups = n // DMA_UNROLL

        def group(g, carry):
            for k in range(DMA_UNROLL):
                j = g * DMA_UNROLL + k
                make(clist_ref[base + j], slot, j).start()
            return carry

        def single(j, carry):
            make(clist_ref[base + j], slot, j).start()
            return carry

        lax.fori_loop(0, groups, group, 0)
        lax.fori_loop(groups * DMA_UNROLL, n, single, 0)

    def wait_all(make, tt, slot):
        def body(j, carry):
            make(0, slot, j).wait()
            return carry
        lax.fori_loop(0, nvalid_ref[tt], body, 0)

    def start_full(make, tt, slot):
        for j in range(TILE_CHUNKS):
            make(clist_ref[tt * TILE_CHUNKS + j], slot, j).start()

    def wait_full(make, slot):
        for j in range(TILE_CHUNKS):
            make(0, slot, j).wait()

    lhs[...] = jnp.zeros(lhs.shape, lhs.dtype)
    start_all(copy_in, 0, 0)
    for c in weight_copies(tile_e_ref[0], 0):
        c.start()

    def tile(t, carry):
        slot = t % 2
        other = 1 - slot

        @pl.when(fresh_ref[t] == 1)
        def _():
            ws = wslot_ref[t]
            for c in weight_copies(0, ws):
                c.wait()
            wgb[...] = sg[ws].astype(BF16)
            wub[...] = su[ws].astype(BF16)
            wdb[...] = sd[ws].astype(BF16)

            @pl.when(next_e_ref[t] >= 0)
            def _():
                for c in weight_copies(next_e_ref[t], 1 - ws):
                    c.start()

        def ffn(rows):
            x = lhs[slot, 0:rows, :]
            act = _silu(_dot(x, wgb[...])) * _dot(x, wub[...])
            obuf[slot, 0:rows, :] = _dot(act.astype(BF16), wdb[...]).astype(BF16)

        full = lambda tt: nvalid_ref[jnp.clip(tt, 0, nt - 1)] == TILE_CHUNKS
        fast = (t + 1 < nt) & (t >= 2) & full(t + 1) & full(t) & full(t - 1) & full(t - 2)

        @pl.when(fast)
        def _():
            wait_full(copy_in, slot)
            wait_full(copy_out, slot)
            start_full(copy_in, t + 1, other)
            start_full(copy_out, t - 1, other)
            ffn(TILE_ROWS)

        @pl.when(jnp.logical_not(fast))
        def _():
            wait_all(copy_in, t, slot)

            @pl.when(t >= 2)
            def _():
                wait_all(copy_out, t - 2, slot)

            @pl.when(t + 1 < nt)
            def _():
                start_all(copy_in, t + 1, other)

            @pl.when(t >= 1)
            def _():
                start_all(copy_out, t - 1, other)

            @pl.when(nvalid_ref[t] > SUB_ROWS // CH)
            def _():
                ffn(TILE_ROWS)

            @pl.when(nvalid_ref[t] <= SUB_ROWS // CH)
            def _():
                ffn(SUB_ROWS)

        @pl.when(t == nt - 1)
        def _():
            start_all(copy_out, t, slot)

        return carry

    lax.fori_loop(0, nt, tile, 0)
    for back in (2, 1):
        @pl.when(nt >= back)
        def _():
            wait_all(copy_out, nt - back, (nt - back) % 2)


def _experts(xs, tables, wg, wu, wd, layer):
    n_win, _, d = xs.shape
    xc = xs.reshape(n_win * WIN_CHUNKS, CH, d)
    de = wg.shape[-1]
    anyspec = pl.BlockSpec(memory_space=pl.ANY)
    grid_spec = pltpu.PrefetchScalarGridSpec(
        num_scalar_prefetch=len(tables),
        grid=(1,),
        in_specs=[anyspec, anyspec, anyspec, anyspec],
        out_specs=anyspec,
        scratch_shapes=[
            pltpu.VMEM((2, TILE_ROWS, d), BF16),
            pltpu.VMEM((2, TILE_ROWS, d), BF16),
            pltpu.VMEM((2, d, de), F32),
            pltpu.VMEM((2, d, de), F32),
            pltpu.VMEM((2, de, d), F32),
            pltpu.VMEM((d, de), BF16),
            pltpu.VMEM((d, de), BF16),
            pltpu.VMEM((de, d), BF16),
            pltpu.SemaphoreType.DMA((2,)),
            pltpu.SemaphoreType.DMA((2,)),
            pltpu.SemaphoreType.DMA((2,)),
        ],
    )
    ys = pl.pallas_call(
        functools.partial(_experts_body, layer),
        grid_spec=grid_spec,
        out_shape=jax.ShapeDtypeStruct(xc.shape, xc.dtype),
        input_output_aliases={len(tables): 0},
        compiler_params=pltpu.CompilerParams(
            dimension_semantics=("arbitrary",), vmem_limit_bytes=VMEM_LIMIT),
        name="experts",
    )(*tables, xc, wg, wu, wd)
    return ys.reshape(xs.shape)


def _expert_tables(tab, t_max):
    i32 = jnp.int32
    nch = tab[:, 0, :N_EXPERTS]
    off = tab[:, 1, :N_EXPERTS]
    n_win = nch.shape[0]
    cum_incl = jnp.cumsum(nch, axis=0)
    cum_excl = cum_incl - nch
    ce = cum_incl[-1]
    te = (ce + TILE_CHUNKS - 1) // TILE_CHUNKS
    tile_end = jnp.cumsum(te)
    tile_start = tile_end - te
    nt = tile_end[-1]
    t_idx = jnp.arange(t_max, dtype=i32)
    tile_e = jnp.minimum(jnp.sum((tile_end[None, :] <= t_idx[:, None]).astype(i32), axis=1), N_EXPERTS - 1)
    sel_e = (tile_e[:, None] == jnp.arange(N_EXPERTS, dtype=i32)[None, :]).astype(i32)
    pick = lambda v: jnp.sum(sel_e * v[None, :], axis=1)
    q0 = (t_idx - pick(tile_start)) * TILE_CHUNKS
    nvalid = jnp.where(t_idx < nt, jnp.clip(pick(ce) - q0, 0, TILE_CHUNKS), 0).astype(i32)
    q = q0[:, None] + jnp.arange(TILE_CHUNKS, dtype=i32)[None, :]
    by_win = lambda m: jnp.sum(sel_e[:, :, None] * m.T[None, :, :], axis=1)
    cum_e = by_win(cum_incl)
    w = jnp.minimum(jnp.sum((cum_e[:, None, :] <= q[:, :, None]).astype(i32), axis=-1), n_win - 1)
    sel_w = (w[:, :, None] == jnp.arange(n_win, dtype=i32)[None, None, :]).astype(i32)
    shift = jnp.sum(sel_w * by_win(off - cum_excl)[:, None, :], axis=-1)
    cid = w * WIN_CHUNKS + shift + q
    valid = jnp.arange(TILE_CHUNKS, dtype=i32)[None, :] < nvalid[:, None]
    clist = jnp.where(valid, cid, 0).astype(i32).reshape(-1)
    prev_e = jnp.concatenate([jnp.full((1,), -1, i32), tile_e[:-1]])
    fresh = ((t_idx < nt) & (tile_e != prev_e)).astype(i32)
    wslot = (jnp.cumsum(fresh) - 1) % 2
    later_fresh = (t_idx[None, :] > t_idx[:, None]) & (fresh[None, :] == 1)
    nxt = jnp.min(jnp.where(later_fresh, t_idx[None, :], t_max), axis=1)
    next_e = jnp.where(nxt < t_max, jnp.sum((t_idx[None, :] == nxt[:, None]).astype(i32) * tile_e[None, :], axis=1), -1)
    return (tile_e.astype(i32), fresh, wslot.astype(i32), next_e.astype(i32), nvalid, clist,
            nt.astype(i32).reshape(1))


def _combine_body(alpha, h1_ref, info_ref, ys_ref, vecd_ref, out_ref):
    for wi in range(info_ref.shape[0]):
        info = info_ref[wi]
        col = _iota((WIN, WIN_ROWS), 1).astype(F32)
        pt = (jnp.where(col == info[:, 0:1], info[:, 2:3], 0.0)
              + jnp.where(col == info[:, 1:2], info[:, 3:4], 0.0)).astype(BF16)
        y = _dot(pt, ys_ref[wi])
        rs = slice(wi * WIN, (wi + 1) * WIN)
        out_ref[rs, :] = _ln(alpha * h1_ref[rs, :] + y, vecd_ref[2:3, :], vecd_ref[3:4, :])


def _combine(h1, info, ys, vecd, alpha, wpb, block0):
    m, d = h1.shape
    win_map = lambda i: (block0 + i, 0, 0)
    return pl.pallas_call(
        functools.partial(_combine_body, alpha),
        grid=(m // (wpb * WIN),),
        in_specs=[
            pl.BlockSpec((wpb * WIN, d), lambda i: (i, 0)),
            pl.BlockSpec((wpb, WIN, LANES), win_map),
            pl.BlockSpec((wpb, WIN_ROWS, d), win_map),
            pl.BlockSpec(vecd.shape, lambda i: (0, 0)),
        ],
        out_specs=pl.BlockSpec((wpb * WIN, d), lambda i: (i, 0)),
        out_shape=jax.ShapeDtypeStruct((m, d), F32),
        compiler_params=pltpu.CompilerParams(
            dimension_semantics=("arbitrary",), vmem_limit_bytes=VMEM_LIMIT),
        name="combine",
    )(h1, info, ys, vecd)


def kernel(x_prompt, x_sample, state_conv_b, state_conv_c, ln_in_g, ln_in_b, w_in, w_s, b_s, ln_v_g, ln_v_b, conv_b_w, conv_b_bias, ln_conv_g, ln_conv_b, conv_c_w, w_out, ln1_g, ln1_b, w_router_group, w_router_expert, w_gate, w_up, w_down, ln2_g, ln2_b):
    depth = w_in.shape[0]
    nb, t, d = x_prompt.shape
    ns, nt_s, _ = x_sample.shape
    assert t % TM == 0 and (ns * nt_s) % WIN == 0 and TM % WIN == 0 and d % LANES == 0
    alpha = (2.0 * depth) ** 0.25
    n_win_p = nb * t // WIN
    n_win_s = ns * nt_s // WIN
    n_win = n_win_p + n_win_s
    assert n_win_p % n_win_s == 0 and n_win_s % (TM // WIN) == 0
    max_chunks = n_win * (2 * WIN // CH + N_EXPERTS)
    t_max = max_chunks // TILE_CHUNKS + N_EXPERTS

    hp = x_prompt
    hs = jnp.transpose(x_sample, (1, 0, 2)).reshape(nt_s * ns, d)
    sb_t = jnp.transpose(state_conv_b, (0, 2, 1, 3))
    sc_t = jnp.transpose(state_conv_c, (0, 2, 1, 3))
    cb_p, cc_p, cb_s, cc_s, v_s = [], [], [], [], []
    for l in range(depth):
        wr = jnp.concatenate([w_router_group[l], w_router_expert[l].reshape(d, N_EXPERTS)], axis=1)
        wr = jnp.pad(wr, ((0, 0), (0, LANES - wr.shape[1]))).astype(BF16)
        zeros_a = jnp.zeros((A_WIDTH,), F32)
        zeros_d = jnp.zeros((d,), F32)
        lw = dict(
            w_in=w_in[l].astype(BF16),
            ws2=w_s[l].reshape(A_WIDTH // LANES, 2 * GMLP_CHUNK, GMLP_CHUNK),
            wexp=jnp.repeat(jnp.transpose(w_s[l][:, :nt_s, :nt_s], (1, 2, 0)), HEAD_DIM, axis=-1),
            bs_exp=jnp.repeat(b_s[l].T, HEAD_DIM, axis=-1),
            veca=jnp.stack([ln_v_g[l], ln_v_b[l], conv_b_bias[l], ln_conv_g[l], ln_conv_b[l],
                            zeros_a, zeros_a, zeros_a]),
            cbw=conv_b_w[l], ccw=conv_c_w[l],
            w_out=w_out[l].astype(BF16),
            vecd=jnp.stack([ln1_g[l], ln1_b[l], ln2_g[l], ln2_b[l], ln_in_g, ln_in_b, zeros_d, zeros_d]),
            wr=wr,
        )
        h1s, vs, nbs, ncs, xs_s, info_s, tab_s = _mixer_sample(hs, sb_t[l], sc_t[l], lw, l == 0, alpha)
        h1p, xs, info, tab, nbp, ncp = _mixer_prompt(hp, lw, l == 0, alpha, xs_s, info_s, tab_s)
        ne = N_EXPERTS
        ys = _experts(xs, _expert_tables(tab, t_max), w_gate.reshape(depth * ne, d, -1),
                      w_up.reshape(depth * ne, d, -1), w_down.reshape(depth * ne, -1, d), l)
        hp = _combine(h1p.reshape(nb * t, d), info, ys, lw["vecd"], alpha, TM // WIN, 0).reshape(nb, t, d)
        hs = _combine(h1s, info, ys, lw["vecd"], alpha, n_win_s, n_win_p // n_win_s)
        cb_p.append(nbp)
        cc_p.append(ncp)
        cb_s.append(nbs)
        cc_s.append(ncs)
        v_s.append(vs)
    y_sample = jnp.transpose(hs.reshape(nt_s, ns, d), (1, 0, 2))
    untime = lambda xs_: jnp.transpose(jnp.stack(xs_), (0, 2, 1, 3))
    chunk_v = jnp.transpose(jnp.stack(v_s).reshape(depth, nt_s, ns, A_WIDTH), (0, 2, 1, 3))
    return (hp, y_sample, jnp.stack(cb_p), jnp.stack(cc_p), untime(cb_s), untime(cc_s), chunk_v)
```

```python
import functools
import math

import jax
import jax.numpy as jnp
from jax import lax
from jax.experimental import pallas as pl
from jax.experimental.pallas import tpu as pltpu

F32 = jnp.float32
BF16 = jnp.bfloat16

HEAD_DIM = 64
A_WIDTH = 384
B_WIDTH = 384
C_WIDTH = 256
GMLP_CHUNK = 128
CONV_B_TAPS = 31
CONV_C_TAPS = 3
N_GROUPS = 4
EXPERTS_PER_GROUP = 8
N_EXPERTS = N_GROUPS * EXPERTS_PER_GROUP
LN_EPS = 1e-5
INV_SQRT2 = 1.0 / math.sqrt(2.0)

LANES = 128
SUBLANES = 8
WIN = 256
CH = 16
WIN_ROWS = -(-(2 * WIN + N_EXPERTS * (CH - 1)) // WIN) * WIN
WIN_CHUNKS = WIN_ROWS // CH
TILE_CHUNKS = 32
TILE_ROWS = TILE_CHUNKS * CH
SUB_ROWS = 256
TM = 512
CONV_RB = 32
DMA_UNROLL = 4
VMEM_LIMIT = 56 * 1024 * 1024


def _ln(x, g, b):
    mu = jnp.mean(x, axis=-1, keepdims=True)
    xc = x - mu
    var = jnp.mean(xc * xc, axis=-1, keepdims=True)
    return xc * lax.rsqrt(var + LN_EPS) * g + b


def _gelu(x):
    return 0.5 * x * (1.0 + lax.erf(x * INV_SQRT2))


def _silu(x):
    return x * jax.nn.sigmoid(x)


def _dot(a, b):
    return jnp.dot(a, b, preferred_element_type=F32)


def _iota(shape, dim):
    return lax.broadcasted_iota(jnp.int32, shape, dim)


def _route_window(logits, hb):
    w = logits.shape[0]
    lane = _iota((w, LANES), 1)
    neg = jnp.float32(-jnp.inf)
    gmask = lane < N_GROUPS
    gl = jnp.where(gmask, logits, neg)
    gmax = jnp.max(gl, axis=-1, keepdims=True)
    gsel = jnp.min(jnp.where(gl == gmax, lane, LANES), axis=-1, keepdims=True)
    den = jnp.sum(jnp.where(gmask, jnp.exp(gl - gmax), 0.0), axis=-1, keepdims=True)
    gw = 1.0 / den
    lo = N_GROUPS + EXPERTS_PER_GROUP * gsel
    el = jnp.where((lane >= lo) & (lane < lo + EXPERTS_PER_GROUP), logits, neg)
    v1 = jnp.max(el, axis=-1, keepdims=True)
    i1 = jnp.min(jnp.where(el == v1, lane, LANES), axis=-1, keepdims=True)
    el2 = jnp.where(lane == i1, neg, el)
    v2 = jnp.max(el2, axis=-1, keepdims=True)
    i2 = jnp.min(jnp.where(el2 == v2, lane, LANES), axis=-1, keepdims=True)
    e2x = jnp.exp(v2 - v1)
    w1 = gw / (1.0 + e2x)
    w2 = gw * e2x / (1.0 + e2x)
    e1 = i1 - N_GROUPS
    e2 = i2 - N_GROUPS

    oh = (lane == e1) | (lane == e2)
    ohf = jnp.where(oh, 1.0, 0.0)
    lstrict = jnp.where(_iota((w, w), 0) > _iota((w, w), 1), 1.0, 0.0).astype(BF16)
    rank = _dot(lstrict, ohf.astype(BF16))
    cnt = jnp.sum(ohf, axis=0, keepdims=True)
    nch = jnp.floor((cnt + (CH - 1.0)) * (1.0 / CH))
    upper = jnp.where(_iota((LANES, LANES), 0) < _iota((LANES, LANES), 1), 1.0, 0.0).astype(BF16)
    off = _dot(jnp.broadcast_to(nch, (SUBLANES, LANES)).astype(BF16), upper)[0:1]
    dest = off * CH + rank
    r1 = jnp.sum(jnp.where(lane == e1, dest, 0.0), axis=-1, keepdims=True)
    r2 = jnp.sum(jnp.where(lane == e2, dest, 0.0), axis=-1, keepdims=True)
    info = jnp.where(lane == 0, r1, jnp.where(lane == 1, r2,
                     jnp.where(lane == 2, w1, jnp.where(lane == 3, w2, 0.0))))
    info_t = info.T
    rows = _iota((WIN_ROWS, w), 0).astype(F32)
    perm = jnp.where((rows == info_t[0:1, :]) | (rows == info_t[1:2, :]), 1.0, 0.0).astype(BF16)
    xs = _dot(perm, hb).astype(BF16)
    sub = _iota((SUBLANES, LANES), 0)
    tab = jnp.where(sub == 0, nch, jnp.where(sub == 1, off, 0.0)).astype(jnp.int32)
    return xs, info, tab


def _unsort_norm(h1, info, ys, g, b, alpha):
    col = _iota((WIN, WIN_ROWS), 1).astype(F32)
    pt = (jnp.where(col == info[:, 0:1], info[:, 2:3], 0.0)
          + jnp.where(col == info[:, 1:2], info[:, 3:4], 0.0)).astype(BF16)
    return _ln(alpha * h1 + _dot(pt, ys), g, b)


def _post_mix(h, mixin, w_out_ref, vecd_ref, wr_ref, alpha, h1_ref, xs_ref, info_ref, tab_ref):
    mix_out = _dot(mixin, w_out_ref[...])
    h1 = _ln(alpha * h + mix_out, vecd_ref[0:1, :], vecd_ref[1:2, :])
    h1_ref[...] = h1
    hb = h1.astype(BF16)
    h_lo = (h1 - hb.astype(F32)).astype(BF16)
    l2 = _dot(hb, wr_ref[...])
    logits = l2[:, 0:LANES] + (l2[:, LANES:] + _dot(h_lo, wr_ref[:, 0:LANES]))
    for wi in range(h.shape[0] // WIN):
        rs = slice(wi * WIN, (wi + 1) * WIN)
        xs, info, tab = _route_window(logits[rs], hb[rs])
        xs_ref[wi] = xs
        info_ref[wi] = info
        tab_ref[wi] = tab


def _mixer_prompt_body(first, alpha, nj, n_tiles, *refs):
    n_prev = 0 if first else 3
    h_ref, prev = refs[0], refs[1:1 + n_prev]
    (w_in_ref, ws_ref, bs_ref, veca_ref, cbw_ref, ccw_ref, w_out_ref, vecd_ref, wr_ref,
     xs_s_ref, info_s_ref, tab_s_ref,
     h1_ref, xs_ref, info_ref, tab_ref, nsb_ref, nsc_ref,
     xsh, cbuf, mixin, wb, cbs) = refs[1 + n_prev:]
    i = pl.program_id(0)

    @pl.when(i < n_tiles)
    def _():
        _mixer_prompt_tile(first, alpha, i % nj, nj - 1, h_ref, prev, w_in_ref, ws_ref, bs_ref, veca_ref,
                           cbw_ref, ccw_ref, w_out_ref, vecd_ref, wr_ref,
                           h1_ref, xs_ref, info_ref, tab_ref, nsb_ref, nsc_ref,
                           xsh, cbuf, mixin, wb, cbs)

    @pl.when(i >= n_tiles)
    def _():
        xs_ref[...] = xs_s_ref[...]
        info_ref[...] = info_s_ref[...]
        tab_ref[...] = tab_s_ref[...]


def _mixer_prompt_tile(first, alpha, j, last, h_ref, prev, w_in_ref, ws_ref, bs_ref, veca_ref, cbw_ref,
                       ccw_ref, w_out_ref, vecd_ref, wr_ref,
                       h1_ref, xs_ref, info_ref, tab_ref, nsb_ref, nsc_ref,
                       xsh, cbuf, mixin, wb, cbs):
    past_b = CONV_B_TAPS - 1
    past_c = CONV_C_TAPS - 1
    b0 = 32 - past_b
    c0 = 8 - past_c
    n_half = TM // WIN
    oa, ob = 2 * A_WIDTH, 2 * A_WIDTH + 2 * B_WIDTH

    @pl.when(j == 0)
    def _():
        xsh[0, 0:32, :] = jnp.zeros((32, B_WIDTH), F32)
        cbuf[0:8, :] = jnp.zeros((8, C_WIDTH), F32)
        for k in range(CONV_B_TAPS):
            wb[k] = jnp.broadcast_to(cbw_ref[k:k + 1, :], (SUBLANES, B_WIDTH))

    tri = (_iota((2 * GMLP_CHUNK, GMLP_CHUNK), 0) % GMLP_CHUNK) >= _iota((2 * GMLP_CHUNK, GMLP_CHUNK), 1)
    wms = [jnp.where(tri, ws_ref[p], 0.0).astype(BF16) for p in range(A_WIDTH // LANES)]
    lane = _iota((GMLP_CHUNK, LANES), 1)

    hs, bgs = [], []
    for hf in range(n_half):
        r0 = hf * WIN
        h = h_ref[0, r0:r0 + WIN, :]
        if first:
            h = _ln(h, vecd_ref[4:5, :], vecd_ref[5:6, :])
        else:
            info_p, ys_p, vecd_p = prev
            h = _unsort_norm(h, info_p[hf], ys_p[hf], vecd_p[2:3, :], vecd_p[3:4, :], alpha)
        hs.append(h)
        hb = h.astype(BF16)
        za = _dot(hb, w_in_ref[:, 0:oa])
        zb = _dot(hb, w_in_ref[:, oa:ob])
        zc = _dot(hb, w_in_ref[:, ob:ob + 3 * C_WIDTH])

        ga = _gelu(za)
        u = ga[:, 0:A_WIDTH]
        vb = _ln(ga[:, A_WIDTH:], veca_ref[0:1, :], veca_ref[1:2, :]).astype(BF16)
        for c in range(WIN // GMLP_CHUNK):
            rs = slice(c * GMLP_CHUNK, (c + 1) * GMLP_CHUNK)
            parts = []
            for p in range(A_WIDTH // LANES):
                ab = _dot(wms[p], vb[rs, p * LANES:(p + 1) * LANES])
                parts.append(jnp.where(lane < HEAD_DIM, ab[:GMLP_CHUNK], ab[GMLP_CHUNK:]))
            mix = jnp.concatenate(parts, axis=1) + bs_ref[...]
            mixin[r0 + c * GMLP_CHUNK:r0 + (c + 1) * GMLP_CHUNK, 0:A_WIDTH] = (u[rs] * mix).astype(BF16)

        xsh[0, 32 + r0:32 + r0 + WIN, :] = zb[:, 0:B_WIDTH] * jax.nn.sigmoid(zb[:, B_WIDTH:])
        lo = 0 if hf == 0 else r0 + 24
        for r in range(1, SUBLANES):
            xsh[r, lo:r0 + WIN + 24, :] = xsh[0, lo + r:r0 + WIN + 24 + r, :]

        bgs.append(zc[:, 0:C_WIDTH])
        cbuf[8 + r0:8 + r0 + WIN, :] = zc[:, C_WIDTH:2 * C_WIDTH] * zc[:, 2 * C_WIDTH:]

    n_sub = CONV_RB // SUBLANES
    for hf in range(n_half):
        r0 = hf * WIN
        for rb in range(WIN // CONV_RB):
            base = r0 + rb * CONV_RB
            accs = [jnp.broadcast_to(veca_ref[2:3, :], (SUBLANES, B_WIDTH))] * n_sub
            for k in range(CONV_B_TAPS):
                s = k + b0
                w8 = wb[k]
                for a in range(n_sub):
                    row = base + (s // SUBLANES + a) * SUBLANES
                    accs[a] = accs[a] + w8 * xsh[s % SUBLANES, row:row + SUBLANES, :]
            for a in range(n_sub):
                cbs[base + a * SUBLANES:base + (a + 1) * SUBLANES, :] = accs[a]
        yb = _silu(_ln(cbs[r0:r0 + WIN, :], veca_ref[3:4, :], veca_ref[4:5, :]))
        mixin[r0:r0 + WIN, A_WIDTH:A_WIDTH + B_WIDTH] = yb.astype(BF16)

        cc = ccw_ref[0:1, :] * cbuf[c0 + r0:c0 + r0 + WIN, :]
        for k in range(1, CONV_C_TAPS):
            cc = cc + ccw_ref[k:k + 1, :] * cbuf[c0 + r0 + k:c0 + r0 + k + WIN, :]
        mixin[r0:r0 + WIN, A_WIDTH + B_WIDTH:] = (bgs[hf] * cc).astype(BF16)

        _post_mix(hs[hf], mixin[r0:r0 + WIN, :], w_out_ref, vecd_ref, wr_ref, alpha,
                  h1_ref.at[0, r0:r0 + WIN], xs_ref.at[hf:hf + 1], info_ref.at[hf:hf + 1],
                  tab_ref.at[hf:hf + 1])

    new_b = xsh[0, TM + b0:TM + 32, :]
    xsh[0, b0:32, :] = new_b
    new_c = cbuf[TM + c0:TM + 8, :]
    cbuf[c0:8, :] = new_c

    @pl.when(j == last)
    def _():
        nsb_ref[0] = new_b
        nsc_ref[0] = new_c


def _mixer_prompt(h, prev, lw, first, alpha, xs_s, info_s, tab_s):
    nb, t, d = h.shape
    nj = t // TM
    wpt = TM // WIN
    n_tiles = nb * nj
    n_extra = xs_s.shape[0] // wpt
    n_win_total = (n_tiles + n_extra) * wpt
    const = lambda shape: pl.BlockSpec(shape, lambda i: (0,) * len(shape))
    tile = lambda i: jnp.minimum(i, n_tiles - 1)
    h_map = lambda i: (tile(i) // nj, tile(i) % nj, 0)
    state_map = lambda i: (tile(i) // nj, 0, 0)
    extra_map = lambda i: (jnp.maximum(i - n_tiles, 0), 0, 0)
    prev_win_map = lambda i: (tile(i), 0, 0)
    prev_specs = [] if first else [pl.BlockSpec((wpt, WIN, LANES), prev_win_map),
                                   pl.BlockSpec((wpt, WIN_ROWS, d), prev_win_map),
                                   const(prev[2].shape)]
    out_shape = (
        jax.ShapeDtypeStruct((nb, t, d), F32),
        jax.ShapeDtypeStruct((n_win_total, WIN_ROWS, d), BF16),
        jax.ShapeDtypeStruct((n_win_total, WIN, LANES), F32),
        jax.ShapeDtypeStruct((n_win_total, SUBLANES, LANES), jnp.int32),
        jax.ShapeDtypeStruct((nb, CONV_B_TAPS - 1, B_WIDTH), F32),
        jax.ShapeDtypeStruct((nb, CONV_C_TAPS - 1, C_WIDTH), F32),
    )
    win_map = lambda i: (i, 0, 0)
    return pl.pallas_call(
        functools.partial(_mixer_prompt_body, first, alpha, nj, n_tiles),
        grid=(n_tiles + n_extra,),
        in_specs=[pl.BlockSpec((1, TM, d), h_map)] + prev_specs + [
            const(lw["w_in"].shape), const(lw["ws2"].shape), const(lw["bs_exp"].shape),
            const(lw["veca"].shape), const(lw["cbw"].shape), const(lw["ccw"].shape),
            const(lw["w_out"].shape), const(lw["vecd"].shape), const(lw["wr"].shape),
            pl.BlockSpec((wpt, WIN_ROWS, d), extra_map),
            pl.BlockSpec((wpt, WIN, LANES), extra_map),
            pl.BlockSpec((wpt, SUBLANES, LANES), extra_map),
        ],
        out_specs=(
            pl.BlockSpec((1, TM, d), h_map),
            pl.BlockSpec((wpt, WIN_ROWS, d), win_map),
            pl.BlockSpec((wpt, WIN, LANES), win_map),
            pl.BlockSpec((wpt, SUBLANES, LANES), win_map),
            pl.BlockSpec((1, CONV_B_TAPS - 1, B_WIDTH), state_map),
            pl.BlockSpec((1, CONV_C_TAPS - 1, C_WIDTH), state_map),
        ),
        out_shape=out_shape,
        scratch_shapes=[
            pltpu.VMEM((SUBLANES, TM + 32, B_WIDTH), F32),
            pltpu.VMEM((TM + 8, C_WIDTH), F32),
            pltpu.VMEM((TM, d), BF16),
            pltpu.VMEM((CONV_B_TAPS, SUBLANES, B_WIDTH), F32),
            pltpu.VMEM((TM, B_WIDTH), F32),
        ],
        compiler_params=pltpu.CompilerParams(
            dimension_semantics=("arbitrary",), vmem_limit_bytes=VMEM_LIMIT),
        name="mixer_prompt",
    )(h, *prev, lw["w_in"], lw["ws2"], lw["bs_exp"], lw["veca"], lw["cbw"], lw["ccw"],
      lw["w_out"], lw["vecd"], lw["wr"], xs_s, info_s, tab_s)


def _mixer_sample_body(first, alpha, n_seq, n_t,
                       h_ref, sb_ref, sc_ref, w_in_ref, wexp_ref, bs_ref, veca_ref, cbw_ref, ccw_ref,
                       w_out_ref, vecd_ref, wr_ref,
                       h1_ref, v_ref, nsb_ref, nsc_ref, xs_ref, info_ref, tab_ref,
                       mixin):
    h = h_ref[...]
    if first:
        h = _ln(h, vecd_ref[4:5, :], vecd_ref[5:6, :])
    hb = h.astype(BF16)
    rows = lambda t: slice(t * n_seq, (t + 1) * n_seq)

    u = _gelu(_dot(hb, w_in_ref[:, 0:A_WIDTH]))
    v = _ln(_gelu(_dot(hb, w_in_ref[:, A_WIDTH:2 * A_WIDTH])), veca_ref[0:1, :], veca_ref[1:2, :])
    v_ref[...] = v
    for t in range(n_t):
        mix = jnp.broadcast_to(bs_ref[t:t + 1, :], (n_seq, A_WIDTH))
        for s in range(t + 1):
            mix = mix + wexp_ref[t, s:s + 1, :] * v[rows(s)]
        mixin[rows(t), 0:A_WIDTH] = (u[rows(t)] * mix).astype(BF16)

    o = 2 * A_WIDTH
    glu = _dot(hb, w_in_ref[:, o:o + B_WIDTH]) * jax.nn.sigmoid(_dot(hb, w_in_ref[:, o + B_WIDTH:o + 2 * B_WIDTH]))
    past_b = CONV_B_TAPS - 1
    xp = lambda m: sb_ref[m] if m < past_b else glu[rows(m - past_b)]
    for t in range(n_t):
        acc = jnp.broadcast_to(veca_ref[2:3, :], (n_seq, B_WIDTH))
        for k in range(CONV_B_TAPS):
            acc = acc + cbw_ref[k:k + 1, :] * xp(t + k)
        yb = _silu(_ln(acc, veca_ref[3:4, :], veca_ref[4:5, :]))
        mixin[rows(t), A_WIDTH:A_WIDTH + B_WIDTH] = yb.astype(BF16)
    for r in range(past_b):
        nsb_ref[r] = xp(r + n_t)

    o = 2 * A_WIDTH + 2 * B_WIDTH
    bg = _dot(hb, w_in_ref[:, o:o + C_WIDTH])
    xc = _dot(hb, w_in_ref[:, o + C_WIDTH:o + 2 * C_WIDTH]) * _dot(hb, w_in_ref[:, o + 2 * C_WIDTH:o + 3 * C_WIDTH])
    past_c = CONV_C_TAPS - 1
    xq = lambda m: sc_ref[m] if m < past_c else xc[rows(m - past_c)]
    for t in range(n_t):
        cc = ccw_ref[0:1, :] * xq(t)
        for k in range(1, CONV_C_TAPS):
            cc = cc + ccw_ref[k:k + 1, :] * xq(t + k)
        mixin[rows(t), A_WIDTH + B_WIDTH:] = (bg[rows(t)] * cc).astype(BF16)
    for r in range(past_c):
        nsc_ref[r] = xq(r + n_t)

    _post_mix(h, mixin[...], w_out_ref, vecd_ref, wr_ref, alpha, h1_ref, xs_ref, info_ref, tab_ref)


def _mixer_sample(h, sb_t, sc_t, lw, first, alpha):
    m, d = h.shape
    n_seq = sb_t.shape[1]
    n_t = m // n_seq
    nw = m // WIN
    full = lambda a: pl.BlockSpec(a.shape, lambda i: (0,) * a.ndim)
    ins = [h, sb_t, sc_t, lw["w_in"], lw["wexp"], lw["bs_exp"], lw["veca"], lw["cbw"], lw["ccw"],
           lw["w_out"], lw["vecd"], lw["wr"]]
    out_shape = (
        jax.ShapeDtypeStruct((m, d), F32),
        jax.ShapeDtypeStruct((m, A_WIDTH), F32),
        jax.ShapeDtypeStruct(sb_t.shape, F32),
        jax.ShapeDtypeStruct(sc_t.shape, F32),
        jax.ShapeDtypeStruct((nw, WIN_ROWS, d), BF16),
        jax.ShapeDtypeStruct((nw, WIN, LANES), F32),
        jax.ShapeDtypeStruct((nw, SUBLANES, LANES), jnp.int32),
    )
    return pl.pallas_call(
        functools.partial(_mixer_sample_body, first, alpha, n_seq, n_t),
        grid=(1,),
        in_specs=[full(a) for a in ins],
        out_specs=tuple(full(o) for o in out_shape),
        out_shape=out_shape,
        scratch_shapes=[pltpu.VMEM((m, d), BF16)],
        compiler_params=pltpu.CompilerParams(
            dimension_semantics=("arbitrary",), vmem_limit_bytes=VMEM_LIMIT),
        name="mixer_sample",
    )(*ins)


def _experts_body(layer, tile_e_ref, fresh_ref, wslot_ref, next_e_ref, nvalid_ref, clist_ref, ntiles_ref,
                  xs_hbm, wg_hbm, wu_hbm, wd_hbm, ys_hbm,
                  lhs, obuf, sg, su, sd, wgb, wub, wdb, sem_in, sem_out, sem_w):
    nt = ntiles_ref[0]

    def copy_in(cid, slot, j):
        return pltpu.make_async_copy(xs_hbm.at[cid], lhs.at[slot, pl.ds(j * CH, CH)], sem_in.at[slot])

    def copy_out(cid, slot, j):
        return pltpu.make_async_copy(obuf.at[slot, pl.ds(j * CH, CH)], ys_hbm.at[cid], sem_out.at[slot])

    def weight_copies(e, ws):
        row = layer * N_EXPERTS + e
        return [pltpu.make_async_copy(src.at[row], dst.at[ws], sem_w.at[ws])
                for src, dst in ((wg_hbm, sg), (wu_hbm, su), (wd_hbm, sd))]

    def start_all(make, tt, slot):
        n = nvalid_ref[tt]
        base = tt * TILE_CHUNKS
        groups = n // DMA_UNROLL

        def group(g, carry):
            for k in range(DMA_UNROLL):
                j = g * DMA_UNROLL + k
                make(clist_ref[base + j], slot, j).start()
            return carry

        def single(j, carry):
            make(clist_ref[base + j], slot, j).start()
            return carry

        lax.fori_loop(0, groups, group, 0)
        lax.fori_loop(groups * DMA_UNROLL, n, single, 0)

    def wait_all(make, tt, slot):
        def body(j, carry):
            make(0, slot, j).wait()
            return carry
        lax.fori_loop(0, nvalid_ref[tt], body, 0)

    def start_full(make, tt, slot):
        for j in range(TILE_CHUNKS):
            make(clist_ref[tt * TILE_CHUNKS + j], slot, j).start()

    def wait_full(make, slot):
        for j in range(TILE_CHUNKS):
            make(0, slot, j).wait()

    lhs[...] = jnp.zeros(lhs.shape, lhs.dtype)
    start_all(copy_in, 0, 0)
    for c in weight_copies(tile_e_ref[0], 0):
        c.start()

    def tile(t, carry):
        slot = t % 2
        other = 1 - slot

        @pl.when(fresh_ref[t] == 1)
        def _():
            ws = wslot_ref[t]
            for c in weight_copies(0, ws):
                c.wait()
            wgb[...] = sg[ws].astype(BF16)
            wub[...] = su[ws].astype(BF16)
            wdb[...] = sd[ws].astype(BF16)

            @pl.when(next_e_ref[t] >= 0)
            def _():
                for c in weight_copies(next_e_ref[t], 1 - ws):
                    c.start()

        def ffn(rows):
            x = lhs[slot, 0:rows, :]
            act = _silu(_dot(x, wgb[...])) * _dot(x, wub[...])
            obuf[slot, 0:rows, :] = _dot(act.astype(BF16), wdb[...]).astype(BF16)

        full = lambda tt: nvalid_ref[jnp.clip(tt, 0, nt - 1)] == TILE_CHUNKS
        fast = (t + 1 < nt) & (t >= 2) & full(t + 1) & full(t) & full(t - 1) & full(t - 2)

        @pl.when(fast)
        def _():
            wait_full(copy_in, slot)
            wait_full(copy_out, slot)
            start_full(copy_in, t + 1, other)
            start_full(copy_out, t - 1, other)
            ffn(TILE_ROWS)

        @pl.when(jnp.logical_not(fast))
        def _():
            wait_all(copy_in, t, slot)

            @pl.when(t >= 2)
            def _():
                wait_all(copy_out, t - 2, slot)

            @pl.when(t + 1 < nt)
            def _():
                start_all(copy_in, t + 1, other)

            @pl.when(t >= 1)
            def _():
                start_all(copy_out, t - 1, other)

            @pl.when(nvalid_ref[t] > SUB_ROWS // CH)
            def _():
                ffn(TILE_ROWS)

            @pl.when(nvalid_ref[t] <= SUB_ROWS // CH)
            def _():
                ffn(SUB_ROWS)

        @pl.when(t == nt - 1)
        def _():
            start_all(copy_out, t, slot)

        return carry

    lax.fori_loop(0, nt, tile, 0)
    for back in (2, 1):
        @pl.when(nt >= back)
        def _():
            wait_all(copy_out, nt - back, (nt - back) % 2)


def _experts(xs, tables, wg, wu, wd, layer):
    n_win, _, d = xs.shape
    xc = xs.reshape(n_win * WIN_CHUNKS, CH, d)
    de = wg.shape[-1]
    anyspec = pl.BlockSpec(memory_space=pl.ANY)
    grid_spec = pltpu.PrefetchScalarGridSpec(
        num_scalar_prefetch=len(tables),
        grid=(1,),
        in_specs=[anyspec, anyspec, anyspec, anyspec],
        out_specs=anyspec,
        scratch_shapes=[
            pltpu.VMEM((2, TILE_ROWS, d), BF16),
            pltpu.VMEM((2, TILE_ROWS, d), BF16),
            pltpu.VMEM((2, d, de), F32),
            pltpu.VMEM((2, d, de), F32),
            pltpu.VMEM((2, de, d), F32),
            pltpu.VMEM((d, de), BF16),
            pltpu.VMEM((d, de), BF16),
            pltpu.VMEM((de, d), BF16),
            pltpu.SemaphoreType.DMA((2,)),
            pltpu.SemaphoreType.DMA((2,)),
            pltpu.SemaphoreType.DMA((2,)),
        ],
    )
    ys = pl.pallas_call(
        functools.partial(_experts_body, layer),
        grid_spec=grid_spec,
        out_shape=jax.ShapeDtypeStruct(xc.shape, xc.dtype),
        input_output_aliases={len(tables): 0},
        compiler_params=pltpu.CompilerParams(
            dimension_semantics=("arbitrary",), vmem_limit_bytes=VMEM_LIMIT),
        name="experts",
    )(*tables, xc, wg, wu, wd)
    return ys.reshape(xs.shape)


def _expert_tables(tab, t_max):
    i32 = jnp.int32
    nch = tab[:, 0, :N_EXPERTS]
    off = tab[:, 1, :N_EXPERTS]
    n_win = nch.shape[0]
    cum_incl = jnp.cumsum(nch, axis=0)
    cum_excl = cum_incl - nch
    ce = cum_incl[-1]
    te = (ce + TILE_CHUNKS - 1) // TILE_CHUNKS
    tile_end = jnp.cumsum(te)
    tile_start = tile_end - te
    nt = tile_end[-1]
    t_idx = jnp.arange(t_max, dtype=i32)
    tile_e = jnp.minimum(jnp.sum((tile_end[None, :] <= t_idx[:, None]).astype(i32), axis=1), N_EXPERTS - 1)
    sel_e = (tile_e[:, None] == jnp.arange(N_EXPERTS, dtype=i32)[None, :]).astype(i32)
    pick = lambda v: jnp.sum(sel_e * v[None, :], axis=1)
    q0 = (t_idx - pick(tile_start)) * TILE_CHUNKS
    nvalid = jnp.where(t_idx < nt, jnp.clip(pick(ce) - q0, 0, TILE_CHUNKS), 0).astype(i32)
    q = q0[:, None] + jnp.arange(TILE_CHUNKS, dtype=i32)[None, :]
    by_win = lambda m: jnp.sum(sel_e[:, :, None] * m.T[None, :, :], axis=1)
    cum_e = by_win(cum_incl)
    w = jnp.minimum(jnp.sum((cum_e[:, None, :] <= q[:, :, None]).astype(i32), axis=-1), n_win - 1)
    sel_w = (w[:, :, None] == jnp.arange(n_win, dtype=i32)[None, None, :]).astype(i32)
    shift = jnp.sum(sel_w * by_win(off - cum_excl)[:, None, :], axis=-1)
    cid = w * WIN_CHUNKS + shift + q
    valid = jnp.arange(TILE_CHUNKS, dtype=i32)[None, :] < nvalid[:, None]
    clist = jnp.where(valid, cid, 0).astype(i32).reshape(-1)
    prev_e = jnp.concatenate([jnp.full((1,), -1, i32), tile_e[:-1]])
    fresh = ((t_idx < nt) & (tile_e != prev_e)).astype(i32)
    wslot = (jnp.cumsum(fresh) - 1) % 2
    later_fresh = (t_idx[None, :] > t_idx[:, None]) & (fresh[None, :] == 1)
    nxt = jnp.min(jnp.where(later_fresh, t_idx[None, :], t_max), axis=1)
    next_e = jnp.where(nxt < t_max, jnp.sum((t_idx[None, :] == nxt[:, None]).astype(i32) * tile_e[None, :], axis=1), -1)
    return (tile_e.astype(i32), fresh, wslot.astype(i32), next_e.astype(i32), nvalid, clist,
            nt.astype(i32).reshape(1))


def _combine_body(alpha, h1_ref, info_ref, ys_ref, vecd_ref, out_ref):
    for wi in range(info_ref.shape[0]):
        rs = slice(wi * WIN, (wi + 1) * WIN)
        out_ref[rs, :] = _unsort_norm(h1_ref[rs, :], info_ref[wi], ys_ref[wi],
                                      vecd_ref[2:3, :], vecd_ref[3:4, :], alpha)


def _combine(h1, info, ys, vecd, alpha, wpb, block0):
    m, d = h1.shape
    win_map = lambda i: (block0 + i, 0, 0)
    return pl.pallas_call(
        functools.partial(_combine_body, alpha),
        grid=(m // (wpb * WIN),),
        in_specs=[
            pl.BlockSpec((wpb * WIN, d), lambda i: (i, 0)),
            pl.BlockSpec((wpb, WIN, LANES), win_map),
            pl.BlockSpec((wpb, WIN_ROWS, d), win_map),
            pl.BlockSpec(vecd.shape, lambda i: (0, 0)),
        ],
        out_specs=pl.BlockSpec((wpb * WIN, d), lambda i: (i, 0)),
        out_shape=jax.ShapeDtypeStruct((m, d), F32),
        compiler_params=pltpu.CompilerParams(
            dimension_semantics=("arbitrary",), vmem_limit_bytes=VMEM_LIMIT),
        name="combine",
    )(h1, info, ys, vecd)


def kernel(x_prompt, x_sample, state_conv_b, state_conv_c, ln_in_g, ln_in_b, w_in, w_s, b_s, ln_v_g, ln_v_b, conv_b_w, conv_b_bias, ln_conv_g, ln_conv_b, conv_c_w, w_out, ln1_g, ln1_b, w_router_group, w_router_expert, w_gate, w_up, w_down, ln2_g, ln2_b):
    depth = w_in.shape[0]
    nb, t, d = x_prompt.shape
    ns, nt_s, _ = x_sample.shape
    assert t % TM == 0 and (ns * nt_s) % WIN == 0 and TM % WIN == 0 and d % LANES == 0
    alpha = (2.0 * depth) ** 0.25
    n_win_p = nb * t // WIN
    n_win_s = ns * nt_s // WIN
    n_win = n_win_p + n_win_s
    assert n_win_p % n_win_s == 0 and n_win_s % (TM // WIN) == 0
    max_chunks = n_win * (2 * WIN // CH + N_EXPERTS)
    t_max = max_chunks // TILE_CHUNKS + N_EXPERTS

    hp = x_prompt
    hs = jnp.transpose(x_sample, (1, 0, 2)).reshape(nt_s * ns, d)
    sb_t = jnp.transpose(state_conv_b, (0, 2, 1, 3))
    sc_t = jnp.transpose(state_conv_c, (0, 2, 1, 3))
    cb_p, cc_p, cb_s, cc_s, v_s = [], [], [], [], []
    prev = ()
    for l in range(depth):
        wr = jnp.concatenate([w_router_group[l], w_router_expert[l].reshape(d, N_EXPERTS)], axis=1)
        wr = jnp.pad(wr, ((0, 0), (0, LANES - wr.shape[1])))
        wr_hi = wr.astype(BF16)
        wr = jnp.concatenate([wr_hi, (wr - wr_hi.astype(F32)).astype(BF16)], axis=1)
        zeros_a = jnp.zeros((A_WIDTH,), F32)
        zeros_d = jnp.zeros((d,), F32)
        lw = dict(
            w_in=w_in[l].astype(BF16),
            ws2=w_s[l].reshape(A_WIDTH // LANES, 2 * GMLP_CHUNK, GMLP_CHUNK),
            wexp=jnp.repeat(jnp.transpose(w_s[l][:, :nt_s, :nt_s], (1, 2, 0)), HEAD_DIM, axis=-1),
            bs_exp=jnp.repeat(b_s[l].T, HEAD_DIM, axis=-1),
            veca=jnp.stack([ln_v_g[l], ln_v_b[l], conv_b_bias[l], ln_conv_g[l], ln_conv_b[l],
                            zeros_a, zeros_a, zeros_a]),
            cbw=conv_b_w[l], ccw=conv_c_w[l],
            w_out=w_out[l].astype(BF16),
            vecd=jnp.stack([ln1_g[l], ln1_b[l], ln2_g[l], ln2_b[l], ln_in_g, ln_in_b, zeros_d, zeros_d]),
            wr=wr,
        )
        h1s, vs, nbs, ncs, xs_s, info_s, tab_s = _mixer_sample(hs, sb_t[l], sc_t[l], lw, l == 0, alpha)
        h1p, xs, info, tab, nbp, ncp = _mixer_prompt(hp, prev, lw, l == 0, alpha, xs_s, info_s, tab_s)
        ne = N_EXPERTS
        ys = _experts(xs, _expert_tables(tab, t_max), w_gate.reshape(depth * ne, d, -1),
                      w_up.reshape(depth * ne, d, -1), w_down.reshape(depth * ne, -1, d), l)
        hp, prev = h1p, (info, ys, lw["vecd"])
        hs = _combine(h1s, info, ys, lw["vecd"], alpha, n_win_s, n_win_p // n_win_s)
        cb_p.append(nbp)
        cc_p.append(ncp)
        cb_s.append(nbs)
        cc_s.append(ncs)
        v_s.append(vs)
    hp = _combine(hp.reshape(nb * t, d), *prev, alpha, TM // WIN, 0).reshape(nb, t, d)
    y_sample = jnp.transpose(hs.reshape(nt_s, ns, d), (1, 0, 2))
    untime = lambda xs_: jnp.transpose(jnp.stack(xs_), (0, 2, 1, 3))
    chunk_v = jnp.transpose(jnp.stack(v_s).reshape(depth, nt_s, ns, A_WIDTH), (0, 2, 1, 3))
    return (hp, y_sample, jnp.stack(cb_p), jnp.stack(cc_p), untime(cb_s), untime(cc_s), chunk_v)
```

```python
import functools
import math

import jax
import jax.numpy as jnp
from jax import lax
from jax.experimental import pallas as pl
from jax.experimental.pallas import tpu as pltpu

F32 = jnp.float32
BF16 = jnp.bfloat16

HEAD_DIM = 64
A_WIDTH = 384
B_WIDTH = 384
C_WIDTH = 256
GMLP_CHUNK = 128
CONV_B_TAPS = 31
CONV_C_TAPS = 3
N_GROUPS = 4
EXPERTS_PER_GROUP = 8
N_EXPERTS = N_GROUPS * EXPERTS_PER_GROUP
LN_EPS = 1e-5
INV_SQRT2 = 1.0 / math.sqrt(2.0)

LANES = 128
SUBLANES = 8
WIN = 256
CH = 16
WIN_ROWS = -(-(2 * WIN + N_EXPERTS * (CH - 1)) // WIN) * WIN
WIN_CHUNKS = WIN_ROWS // CH
TILE_CHUNKS = 32
TILE_ROWS = TILE_CHUNKS * CH
SUB_ROWS = 256
TM = 512
CONV_RB = 32
DMA_UNROLL = 4
N_SLOTS = 3
GATHER_AHEAD = N_SLOTS - 1
VMEM_LIMIT = 56 * 1024 * 1024


def _ln(x, g, b):
    mu = jnp.mean(x, axis=-1, keepdims=True)
    xc = x - mu
    var = jnp.mean(xc * xc, axis=-1, keepdims=True)
    return xc * lax.rsqrt(var + LN_EPS) * g + b


def _gelu(x):
    return 0.5 * x * (1.0 + lax.erf(x * INV_SQRT2))


def _silu(x):
    return x * jax.nn.sigmoid(x)


def _dot(a, b):
    return jnp.dot(a, b, preferred_element_type=F32)


def _iota(shape, dim):
    return lax.broadcasted_iota(jnp.int32, shape, dim)


def _route_window(logits, hb):
    w = logits.shape[0]
    lane = _iota((w, LANES), 1)
    neg = jnp.float32(-jnp.inf)
    gmask = lane < N_GROUPS
    gl = jnp.where(gmask, logits, neg)
    gmax = jnp.max(gl, axis=-1, keepdims=True)
    gsel = jnp.min(jnp.where(gl == gmax, lane, LANES), axis=-1, keepdims=True)
    den = jnp.sum(jnp.where(gmask, jnp.exp(gl - gmax), 0.0), axis=-1, keepdims=True)
    gw = 1.0 / den
    lo = N_GROUPS + EXPERTS_PER_GROUP * gsel
    el = jnp.where((lane >= lo) & (lane < lo + EXPERTS_PER_GROUP), logits, neg)
    v1 = jnp.max(el, axis=-1, keepdims=True)
    i1 = jnp.min(jnp.where(el == v1, lane, LANES), axis=-1, keepdims=True)
    el2 = jnp.where(lane == i1, neg, el)
    v2 = jnp.max(el2, axis=-1, keepdims=True)
    i2 = jnp.min(jnp.where(el2 == v2, lane, LANES), axis=-1, keepdims=True)
    e2x = jnp.exp(v2 - v1)
    w1 = gw / (1.0 + e2x)
    w2 = gw * e2x / (1.0 + e2x)
    e1 = i1 - N_GROUPS
    e2 = i2 - N_GROUPS

    oh = (lane == e1) | (lane == e2)
    ohf = jnp.where(oh, 1.0, 0.0)
    lstrict = jnp.where(_iota((w, w), 0) > _iota((w, w), 1), 1.0, 0.0).astype(BF16)
    rank = _dot(lstrict, ohf.astype(BF16))
    cnt = jnp.sum(ohf, axis=0, keepdims=True)
    nch = jnp.floor((cnt + (CH - 1.0)) * (1.0 / CH))
    upper = jnp.where(_iota((LANES, LANES), 0) < _iota((LANES, LANES), 1), 1.0, 0.0).astype(BF16)
    off = _dot(jnp.broadcast_to(nch, (SUBLANES, LANES)).astype(BF16), upper)[0:1]
    dest = off * CH + rank
    r1 = jnp.sum(jnp.where(lane == e1, dest, 0.0), axis=-1, keepdims=True)
    r2 = jnp.sum(jnp.where(lane == e2, dest, 0.0), axis=-1, keepdims=True)
    info = jnp.where(lane == 0, r1, jnp.where(lane == 1, r2,
                     jnp.where(lane == 2, w1, jnp.where(lane == 3, w2, 0.0))))
    info_t = info.T
    rows = _iota((WIN_ROWS, w), 0).astype(F32)
    perm = jnp.where((rows == info_t[0:1, :]) | (rows == info_t[1:2, :]), 1.0, 0.0).astype(BF16)
    xs = _dot(perm, hb).astype(BF16)
    sub = _iota((SUBLANES, LANES), 0)
    tab = jnp.where(sub == 0, nch, jnp.where(sub == 1, off, 0.0)).astype(jnp.int32)
    return xs, info, tab


def _unsort_norm(h1, info, ys, g, b, alpha):
    col = _iota((WIN, WIN_ROWS), 1).astype(F32)
    pt = (jnp.where(col == info[:, 0:1], info[:, 2:3], 0.0)
          + jnp.where(col == info[:, 1:2], info[:, 3:4], 0.0)).astype(BF16)
    return _ln(alpha * h1 + _dot(pt, ys), g, b)


def _post_mix(h, mixin, w_out_ref, vecd_ref, wr_ref, alpha, h1_ref, xs_ref, info_ref, tab_ref):
    mix_out = _dot(mixin, w_out_ref[...])
    h1 = _ln(alpha * h + mix_out, vecd_ref[0:1, :], vecd_ref[1:2, :])
    h1_ref[...] = h1
    hb = h1.astype(BF16)
    h_lo = (h1 - hb.astype(F32)).astype(BF16)
    l2 = _dot(hb, wr_ref[...])
    logits = l2[:, 0:LANES] + (l2[:, LANES:] + _dot(h_lo, wr_ref[:, 0:LANES]))
    for wi in range(h.shape[0] // WIN):
        rs = slice(wi * WIN, (wi + 1) * WIN)
        xs, info, tab = _route_window(logits[rs], hb[rs])
        xs_ref[wi] = xs
        info_ref[wi] = info
        tab_ref[wi] = tab


def _mixer_prompt_body(first, alpha, nj, n_tiles, *refs):
    n_prev = 0 if first else 3
    h_ref, prev = refs[0], refs[1:1 + n_prev]
    (w_in_ref, ws_ref, bs_ref, veca_ref, cbw_ref, ccw_ref, w_out_ref, vecd_ref, wr_ref,
     xs_s_ref, info_s_ref, tab_s_ref,
     h1_ref, xs_ref, info_ref, tab_ref, nsb_ref, nsc_ref,
     xsh, cbuf, mixin, wb, cbs, hres) = refs[1 + n_prev:]
    i = pl.program_id(0)
    j = jnp.minimum(i, n_tiles - 1) % nj

    @pl.when(i == 0)
    def _():
        mixin[...] = jnp.zeros(mixin.shape, mixin.dtype)
        hres[...] = jnp.zeros(hres.shape, hres.dtype)

    @pl.when(j == 0)
    def _():
        xsh[0, 0:32, :] = jnp.zeros((32, B_WIDTH), F32)
        cbuf[0:8, :] = jnp.zeros((8, C_WIDTH), F32)
        for k in range(CONV_B_TAPS):
            wb[k] = jnp.broadcast_to(cbw_ref[k:k + 1, :], (SUBLANES, B_WIDTH))

    @pl.when(i <= n_tiles)
    def _():
        mix_prev, h_prev = mixin[...], hres[...]

        def back_prev(hf):
            rs = slice(hf * WIN, (hf + 1) * WIN)
            _post_mix(h_prev[rs], mix_prev[rs], w_out_ref, vecd_ref, wr_ref, alpha,
                      h1_ref.at[0, rs], xs_ref.at[hf:hf + 1], info_ref.at[hf:hf + 1],
                      tab_ref.at[hf:hf + 1])

        new_b, new_c = _mixer_prompt_tile(first, alpha, h_ref, prev, w_in_ref, ws_ref, bs_ref, veca_ref,
                                          vecd_ref, ccw_ref, xsh, cbuf, mixin, wb, cbs, hres, back_prev)

        @pl.when((i < n_tiles) & (j == nj - 1))
        def _():
            nsb_ref[0] = new_b
            nsc_ref[0] = new_c

    @pl.when(i > n_tiles)
    def _():
        xs_ref[...] = xs_s_ref[...]
        info_ref[...] = info_s_ref[...]
        tab_ref[...] = tab_s_ref[...]


def _mixer_prompt_tile(first, alpha, h_ref, prev, w_in_ref, ws_ref, bs_ref, veca_ref, vecd_ref,
                       ccw_ref, xsh, cbuf, mixin, wb, cbs, hres, back_prev):
    past_b = CONV_B_TAPS - 1
    past_c = CONV_C_TAPS - 1
    b0 = 32 - past_b
    c0 = 8 - past_c
    n_half = TM // WIN
    oa, ob = 2 * A_WIDTH, 2 * A_WIDTH + 2 * B_WIDTH

    tri =(_iota((2 * GMLP_CHUNK, GMLP_CHUNK), 0) % GMLP_CHUNK) >= _iota((2 * GMLP_CHUNK, GMLP_CHUNK), 1)
    wms = [jnp.where(tri, ws_ref[p], 0.0).astype(BF16) for p in range(A_WIDTH // LANES)]
    lane = _iota((GMLP_CHUNK, LANES), 1)

    bgs = []
    for hf in range(n_half):
        r0 = hf * WIN
        h = h_ref[0, r0:r0 + WIN, :]
        if first:
            h = _ln(h, vecd_ref[4:5, :], vecd_ref[5:6, :])
        else:
            info_p, ys_p, vecd_p = prev
            h = _unsort_norm(h, info_p[hf], ys_p[hf], vecd_p[2:3, :], vecd_p[3:4, :], alpha)
        hres[r0:r0 + WIN, :] = h
        hb = h.astype(BF16)
        za = _dot(hb, w_in_ref[:, 0:oa])
        zb = _dot(hb, w_in_ref[:, oa:ob])
        zc = _dot(hb, w_in_ref[:, ob:ob + 3 * C_WIDTH])

        ga = _gelu(za)
        u = ga[:, 0:A_WIDTH]
        vb = _ln(ga[:, A_WIDTH:], veca_ref[0:1, :], veca_ref[1:2, :]).astype(BF16)
        for c in range(WIN // GMLP_CHUNK):
            rs = slice(c * GMLP_CHUNK, (c + 1) * GMLP_CHUNK)
            parts = []
            for p in range(A_WIDTH // LANES):
                ab = _dot(wms[p], vb[rs, p * LANES:(p + 1) * LANES])
                parts.append(jnp.where(lane < HEAD_DIM, ab[:GMLP_CHUNK], ab[GMLP_CHUNK:]))
            mix = jnp.concatenate(parts, axis=1) + bs_ref[...]
            mixin[r0 + c * GMLP_CHUNK:r0 + (c + 1) * GMLP_CHUNK, 0:A_WIDTH] = (u[rs] * mix).astype(BF16)

        xsh[0, 32 + r0:32 + r0 + WIN, :] = zb[:, 0:B_WIDTH] * jax.nn.sigmoid(zb[:, B_WIDTH:])
        lo = 0 if hf == 0 else r0 + 24
        for r in range(1, SUBLANES):
            xsh[r, lo:r0 + WIN + 24, :] = xsh[0, lo + r:r0 + WIN + 24 + r, :]

        bgs.append(zc[:, 0:C_WIDTH])
        cbuf[8 + r0:8 + r0 + WIN, :] = zc[:, C_WIDTH:2 * C_WIDTH] * zc[:, 2 * C_WIDTH:]
        back_prev(hf)

    n_sub = CONV_RB // SUBLANES
    for hf in range(n_half):
        r0 = hf * WIN
        for rb in range(WIN // CONV_RB):
            base = r0 + rb * CONV_RB
            accs = [jnp.broadcast_to(veca_ref[2:3, :], (SUBLANES, B_WIDTH))] * n_sub
            for k in range(CONV_B_TAPS):
                s = k + b0
                w8 = wb[k]
                for a in range(n_sub):
                    row = base + (s // SUBLANES + a) * SUBLANES
                    accs[a] = accs[a] + w8 * xsh[s % SUBLANES, row:row + SUBLANES, :]
            for a in range(n_sub):
                cbs[base + a * SUBLANES:base + (a + 1) * SUBLANES, :] = accs[a]
        yb = _silu(_ln(cbs[r0:r0 + WIN, :], veca_ref[3:4, :], veca_ref[4:5, :]))
        mixin[r0:r0 + WIN, A_WIDTH:A_WIDTH + B_WIDTH] = yb.astype(BF16)

        cc = ccw_ref[0:1, :] * cbuf[c0 + r0:c0 + r0 + WIN, :]
        for k in range(1, CONV_C_TAPS):
            cc = cc + ccw_ref[k:k + 1, :] * cbuf[c0 + r0 + k:c0 + r0 + k + WIN, :]
        mixin[r0:r0 + WIN, A_WIDTH + B_WIDTH:] = (bgs[hf] * cc).astype(BF16)

    new_b = xsh[0, TM + b0:TM + 32, :]
    xsh[0, b0:32, :] = new_b
    new_c = cbuf[TM + c0:TM + 8, :]
    cbuf[c0:8, :] = new_c
    return new_b, new_c


def _mixer_prompt(h, prev, lw, first, alpha, xs_s, info_s, tab_s):
    nb, t, d = h.shape
    nj = t // TM
    wpt = TM // WIN
    n_tiles = nb * nj
    n_extra = xs_s.shape[0] // wpt
    n_win_total = (n_tiles + n_extra) * wpt
    const = lambda shape: pl.BlockSpec(shape, lambda i: (0,) * len(shape))
    tile = lambda i: jnp.minimum(i, n_tiles - 1)
    h_map = lambda i: (tile(i) // nj, tile(i) % nj, 0)
    state_map = lambda i: (tile(i) // nj, 0, 0)
    extra_map = lambda i: (jnp.maximum(i - n_tiles - 1, 0), 0, 0)
    lag = lambda i: jnp.maximum(i - 1, 0)
    h1_map = lambda i: (tile(lag(i)) // nj, tile(lag(i)) % nj, 0)
    prev_win_map = lambda i: (tile(i), 0, 0)
    prev_specs = [] if first else [pl.BlockSpec((wpt, WIN, LANES), prev_win_map),
                                   pl.BlockSpec((wpt, WIN_ROWS, d), prev_win_map),
                                   const(prev[2].shape)]
    out_shape = (
        jax.ShapeDtypeStruct((nb, t, d), F32),
        jax.ShapeDtypeStruct((n_win_total, WIN_ROWS, d), BF16),
        jax.ShapeDtypeStruct((n_win_total, WIN, LANES), F32),
        jax.ShapeDtypeStruct((n_win_total, SUBLANES, LANES), jnp.int32),
        jax.ShapeDtypeStruct((nb, CONV_B_TAPS - 1, B_WIDTH), F32),
        jax.ShapeDtypeStruct((nb, CONV_C_TAPS - 1, C_WIDTH), F32),
    )
    win_map = lambda i: (lag(i), 0, 0)
    return pl.pallas_call(
        functools.partial(_mixer_prompt_body, first, alpha, nj, n_tiles),
        grid=(n_tiles + 1 + n_extra,),
        in_specs=[pl.BlockSpec((1, TM, d), h_map)] + prev_specs + [
            const(lw["w_in"].shape), const(lw["ws2"].shape), const(lw["bs_exp"].shape),
            const(lw["veca"].shape), const(lw["cbw"].shape), const(lw["ccw"].shape),
            const(lw["w_out"].shape), const(lw["vecd"].shape), const(lw["wr"].shape),
            pl.BlockSpec((wpt, WIN_ROWS, d), extra_map),
            pl.BlockSpec((wpt, WIN, LANES), extra_map),
            pl.BlockSpec((wpt, SUBLANES, LANES), extra_map),
        ],
        out_specs=(
            pl.BlockSpec((1, TM, d), h1_map),
            pl.BlockSpec((wpt, WIN_ROWS, d), win_map),
            pl.BlockSpec((wpt, WIN, LANES), win_map),
            pl.BlockSpec((wpt, SUBLANES, LANES), win_map),
            pl.BlockSpec((1, CONV_B_TAPS - 1, B_WIDTH), state_map),
            pl.BlockSpec((1, CONV_C_TAPS - 1, C_WIDTH), state_map),
        ),
        out_shape=out_shape,
        scratch_shapes=[
            pltpu.VMEM((SUBLANES, TM + 32, B_WIDTH), F32),
            pltpu.VMEM((TM + 8, C_WIDTH), F32),
            pltpu.VMEM((TM, d), BF16),
            pltpu.VMEM((CONV_B_TAPS, SUBLANES, B_WIDTH), F32),
            pltpu.VMEM((TM, B_WIDTH), F32),
            pltpu.VMEM((TM, d), F32),
        ],
        compiler_params=pltpu.CompilerParams(
            dimension_semantics=("arbitrary",), vmem_limit_bytes=VMEM_LIMIT),
        name="mixer_prompt",
    )(h, *prev, lw["w_in"], lw["ws2"], lw["bs_exp"], lw["veca"], lw["cbw"], lw["ccw"],
      lw["w_out"], lw["vecd"], lw["wr"], xs_s, info_s, tab_s)


def _mixer_sample_body(first, alpha, n_seq, n_t,
                       h_ref, sb_ref, sc_ref, w_in_ref, wexp_ref, bs_ref, veca_ref, cbw_ref, ccw_ref,
                       w_out_ref, vecd_ref, wr_ref,
                       h1_ref, v_ref, nsb_ref, nsc_ref, xs_ref, info_ref, tab_ref,
                       mixin):
    h = h_ref[...]
    if first:
        h = _ln(h, vecd_ref[4:5, :], vecd_ref[5:6, :])
    hb = h.astype(BF16)
    rows = lambda t: slice(t * n_seq, (t + 1) * n_seq)

    u = _gelu(_dot(hb, w_in_ref[:, 0:A_WIDTH]))
    v = _ln(_gelu(_dot(hb, w_in_ref[:, A_WIDTH:2 * A_WIDTH])), veca_ref[0:1, :], veca_ref[1:2, :])
    v_ref[...] = v
    for t in range(n_t):
        mix = jnp.broadcast_to(bs_ref[t:t + 1, :], (n_seq, A_WIDTH))
        for s in range(t + 1):
            mix = mix + wexp_ref[t, s:s + 1, :] * v[rows(s)]
        mixin[rows(t), 0:A_WIDTH] = (u[rows(t)] * mix).astype(BF16)

    o = 2 * A_WIDTH
    glu = _dot(hb, w_in_ref[:, o:o + B_WIDTH]) * jax.nn.sigmoid(_dot(hb, w_in_ref[:, o + B_WIDTH:o + 2 * B_WIDTH]))
    past_b = CONV_B_TAPS - 1
    xp = lambda m: sb_ref[m] if m < past_b else glu[rows(m - past_b)]
    for t in range(n_t):
        acc = jnp.broadcast_to(veca_ref[2:3, :], (n_seq, B_WIDTH))
        for k in range(CONV_B_TAPS):
            acc = acc + cbw_ref[k:k + 1, :] * xp(t + k)
        yb = _silu(_ln(acc, veca_ref[3:4, :], veca_ref[4:5, :]))
        mixin[rows(t), A_WIDTH:A_WIDTH + B_WIDTH] = yb.astype(BF16)
    for r in range(past_b):
        nsb_ref[r] = xp(r + n_t)

    o = 2 * A_WIDTH + 2 * B_WIDTH
    bg = _dot(hb, w_in_ref[:, o:o + C_WIDTH])
    xc = _dot(hb, w_in_ref[:, o + C_WIDTH:o + 2 * C_WIDTH]) * _dot(hb, w_in_ref[:, o + 2 * C_WIDTH:o + 3 * C_WIDTH])
    past_c = CONV_C_TAPS - 1
    xq = lambda m: sc_ref[m] if m < past_c else xc[rows(m - past_c)]
    for t in range(n_t):
        cc = ccw_ref[0:1, :] * xq(t)
        for k in range(1, CONV_C_TAPS):
            cc = cc + ccw_ref[k:k + 1, :] * xq(t + k)
        mixin[rows(t), A_WIDTH + B_WIDTH:] = (bg[rows(t)] * cc).astype(BF16)
    for r in range(past_c):
        nsc_ref[r] = xq(r + n_t)

    _post_mix(h, mixin[...], w_out_ref, vecd_ref, wr_ref, alpha, h1_ref, xs_ref, info_ref, tab_ref)


def _mixer_sample(h, sb_t, sc_t, lw, first, alpha):
    m, d = h.shape
    n_seq = sb_t.shape[1]
    n_t = m // n_seq
    nw = m // WIN
    full = lambda a: pl.BlockSpec(a.shape, lambda i: (0,) * a.ndim)
    ins = [h, sb_t, sc_t, lw["w_in"], lw["wexp"], lw["bs_exp"], lw["veca"], lw["cbw"], lw["ccw"],
           lw["w_out"], lw["vecd"], lw["wr"]]
    out_shape = (
        jax.ShapeDtypeStruct((m, d), F32),
        jax.ShapeDtypeStruct((m, A_WIDTH), F32),
        jax.ShapeDtypeStruct(sb_t.shape, F32),
        jax.ShapeDtypeStruct(sc_t.shape, F32),
        jax.ShapeDtypeStruct((nw, WIN_ROWS, d), BF16),
        jax.ShapeDtypeStruct((nw, WIN, LANES), F32),
        jax.ShapeDtypeStruct((nw, SUBLANES, LANES), jnp.int32),
    )
    return pl.pallas_call(
        functools.partial(_mixer_sample_body, first, alpha, n_seq, n_t),
        grid=(1,),
        in_specs=[full(a) for a in ins],
        out_specs=tuple(full(o) for o in out_shape),
        out_shape=out_shape,
        scratch_shapes=[pltpu.VMEM((m, d), BF16)],
        compiler_params=pltpu.CompilerParams(
            dimension_semantics=("arbitrary",), vmem_limit_bytes=VMEM_LIMIT),
        name="mixer_sample",
    )(*ins)


def _experts_body(layer, tile_e_ref, fresh_ref, wslot_ref, next_e_ref, nvalid_ref, clist_ref, ntiles_ref,
                  xs_hbm, wg_hbm, wu_hbm, wd_hbm, ys_hbm,
                  lhs, obuf, sg, su, sd, wgb, wub, wdb, sem_in, sem_out, sem_w):
    nt = ntiles_ref[0]

    def copy_in(cid, slot, j):
        return pltpu.make_async_copy(xs_hbm.at[cid], lhs.at[slot, pl.ds(j * CH, CH)], sem_in.at[slot])

    def copy_out(cid, slot, j):
        return pltpu.make_async_copy(obuf.at[slot, pl.ds(j * CH, CH)], ys_hbm.at[cid], sem_out.at[slot])

    def weight_copies(e, ws):
        row = layer * N_EXPERTS + e
        return [pltpu.make_async_copy(src.at[row], dst.at[ws], sem_w.at[ws])
                for src, dst in ((wg_hbm, sg), (wu_hbm, su), (wd_hbm, sd))]

    def start_all(make, tt, slot):
        n = nvalid_ref[tt]
        base = tt * TILE_CHUNKS
        groups = n // DMA_UNROLL

        def group(g, carry):
            for k in range(DMA_UNROLL):
                j = g * DMA_UNROLL + k
                make(clist_ref[base + j], slot, j).start()
            return carry

        def single(j, carry):
            make(clist_ref[base + j], slot, j).start()
            return carry

        lax.fori_loop(0, groups, group, 0)
        lax.fori_loop(groups * DMA_UNROLL, n, single, 0)

    def wait_all(make, tt, slot):
        def body(j, carry):
            make(0, slot, j).wait()
            return carry
        lax.fori_loop(0, nvalid_ref[tt], body, 0)

    def start_full(make, tt, slot):
        for j in range(TILE_CHUNKS):
            make(clist_ref[tt * TILE_CHUNKS + j], slot, j).start()

    def wait_full(make, slot):
        for j in range(TILE_CHUNKS):
            make(0, slot, j).wait()

    lhs[...] = jnp.zeros(lhs.shape, lhs.dtype)
    for c in weight_copies(tile_e_ref[0], 0):
        c.start()
    for tt in range(GATHER_AHEAD):
        @pl.when(tt < nt)
        def _():
            start_all(copy_in, tt, tt % N_SLOTS)

    def tile(t, carry):
        slot = t % N_SLOTS
        t_gather = t + GATHER_AHEAD
        t_drain = t - N_SLOTS

        @pl.when(fresh_ref[t] == 1)
        def _():
            ws = wslot_ref[t]
            for c in weight_copies(0, ws):
                c.wait()
            wgb[...] = sg[ws].astype(BF16)
            wub[...] = su[ws].astype(BF16)
            wdb[...] = sd[ws].astype(BF16)

            @pl.when(next_e_ref[t] >= 0)
            def _():
                for c in weight_copies(next_e_ref[t], 1 - ws):
                    c.start()

        def ffn(rows):
            x = lhs[slot, 0:rows, :]
            act = _silu(_dot(x, wgb[...])) * _dot(x, wub[...])
            obuf[slot, 0:rows, :] = _dot(act.astype(BF16), wdb[...]).astype(BF16)

        full = lambda tt: nvalid_ref[jnp.clip(tt, 0, nt - 1)] == TILE_CHUNKS
        fast = ((t_gather < nt) & (t_drain >= 0) & full(t_gather) & full(t) & full(t - 1)
                & full(t_drain))

        @pl.when(fast)
        def _():
            wait_full(copy_in, slot)
            wait_full(copy_out, slot)
            start_full(copy_in, t_gather, t_gather % N_SLOTS)
            start_full(copy_out, t - 1, (t - 1) % N_SLOTS)
            ffn(TILE_ROWS)

        @pl.when(jnp.logical_not(fast))
        def _():
            wait_all(copy_in, t, slot)

            @pl.when(t_drain >= 0)
            def _():
                wait_all(copy_out, t_drain, slot)

            @pl.when(t_gather < nt)
            def _():
                start_all(copy_in, t_gather, t_gather % N_SLOTS)

            @pl.when(t >= 1)
            def _():
                start_all(copy_out, t - 1, (t - 1) % N_SLOTS)

            @pl.when(nvalid_ref[t] > SUB_ROWS // CH)
            def _():
                ffn(TILE_ROWS)

            @pl.when(nvalid_ref[t] <= SUB_ROWS // CH)
            def _():
                ffn(SUB_ROWS)

        @pl.when(t == nt - 1)
        def _():
            start_all(copy_out, t, slot)

        return carry

    lax.fori_loop(0, nt, tile, 0)
    for back in range(N_SLOTS, 0, -1):
        @pl.when(nt >= back)
        def _():
            wait_all(copy_out, nt - back, (nt - back) % N_SLOTS)


def _experts(xs, tables, wg, wu, wd, layer):
    n_win, _, d = xs.shape
    xc = xs.reshape(n_win * WIN_CHUNKS, CH, d)
    de = wg.shape[-1]
    anyspec = pl.BlockSpec(memory_space=pl.ANY)
    grid_spec = pltpu.PrefetchScalarGridSpec(
        num_scalar_prefetch=len(tables),
        grid=(1,),
        in_specs=[anyspec, anyspec, anyspec, anyspec],
        out_specs=anyspec,
        scratch_shapes=[
            pltpu.VMEM((N_SLOTS, TILE_ROWS, d), BF16),
            pltpu.VMEM((N_SLOTS, TILE_ROWS, d), BF16),
            pltpu.VMEM((2, d, de), F32),
            pltpu.VMEM((2, d, de), F32),
            pltpu.VMEM((2, de, d), F32),
            pltpu.VMEM((d, de), BF16),
            pltpu.VMEM((d, de), BF16),
            pltpu.VMEM((de, d), BF16),
            pltpu.SemaphoreType.DMA((N_SLOTS,)),
            pltpu.SemaphoreType.DMA((N_SLOTS,)),
            pltpu.SemaphoreType.DMA((2,)),
        ],
    )
    ys = pl.pallas_call(
        functools.partial(_experts_body, layer),
        grid_spec=grid_spec,
        out_shape=jax.ShapeDtypeStruct(xc.shape, xc.dtype),
        input_output_aliases={len(tables): 0},
        compiler_params=pltpu.CompilerParams(
            dimension_semantics=("arbitrary",), vmem_limit_bytes=VMEM_LIMIT),
        name="experts",
    )(*tables, xc, wg, wu, wd)
    return ys.reshape(xs.shape)


def _expert_tables(tab, t_max):
    i32 = jnp.int32
    nch = tab[:, 0, :N_EXPERTS]
    off = tab[:, 1, :N_EXPERTS]
    n_win = nch.shape[0]
    cum_incl = jnp.cumsum(nch, axis=0)
    cum_excl = cum_incl - nch
    ce = cum_incl[-1]
    te = (ce + TILE_CHUNKS - 1) // TILE_CHUNKS
    tile_end = jnp.cumsum(te)
    tile_start = tile_end - te
    nt = tile_end[-1]
    t_idx = jnp.arange(t_max, dtype=i32)
    tile_e = jnp.minimum(jnp.sum((tile_end[None, :] <= t_idx[:, None]).astype(i32), axis=1), N_EXPERTS - 1)
    sel_e = (tile_e[:, None] == jnp.arange(N_EXPERTS, dtype=i32)[None, :]).astype(i32)
    pick = lambda v: jnp.sum(sel_e * v[None, :], axis=1)
    q0 = (t_idx - pick(tile_start)) * TILE_CHUNKS
    nvalid = jnp.where(t_idx < nt, jnp.clip(pick(ce) - q0, 0, TILE_CHUNKS), 0).astype(i32)
    q = q0[:, None] + jnp.arange(TILE_CHUNKS, dtype=i32)[None, :]
    by_win = lambda m: jnp.sum(sel_e[:, :, None] * m.T[None, :, :], axis=1)
    cum_e = by_win(cum_incl)
    w = jnp.minimum(jnp.sum((cum_e[:, None, :] <= q[:, :, None]).astype(i32), axis=-1), n_win - 1)
    sel_w = (w[:, :, None] == jnp.arange(n_win, dtype=i32)[None, None, :]).astype(i32)
    shift = jnp.sum(sel_w * by_win(off - cum_excl)[:, None, :], axis=-1)
    cid = w * WIN_CHUNKS + shift + q
    valid = jnp.arange(TILE_CHUNKS, dtype=i32)[None, :] < nvalid[:, None]
    clist = jnp.where(valid, cid, 0).astype(i32).reshape(-1)
    prev_e = jnp.concatenate([jnp.full((1,), -1, i32), tile_e[:-1]])
    fresh = ((t_idx < nt) & (tile_e != prev_e)).astype(i32)
    wslot = (jnp.cumsum(fresh) - 1) % 2
    later_fresh = (t_idx[None, :] > t_idx[:, None]) & (fresh[None, :] == 1)
    nxt = jnp.min(jnp.where(later_fresh, t_idx[None, :], t_max), axis=1)
    next_e = jnp.where(nxt < t_max, jnp.sum((t_idx[None, :] == nxt[:, None]).astype(i32) * tile_e[None, :], axis=1), -1)
    return (tile_e.astype(i32), fresh, wslot.astype(i32), next_e.astype(i32), nvalid, clist,
            nt.astype(i32).reshape(1))


def _combine_body(alpha, h1_ref, info_ref, ys_ref, vecd_ref, out_ref):
    for wi in range(info_ref.shape[0]):
        rs = slice(wi * WIN, (wi + 1) * WIN)
        out_ref[rs, :] = _unsort_norm(h1_ref[rs, :], info_ref[wi], ys_ref[wi],
                                      vecd_ref[2:3, :], vecd_ref[3:4, :], alpha)


def _combine(h1, info, ys, vecd, alpha, wpb, block0):
    m, d = h1.shape
    win_map = lambda i: (block0 + i, 0, 0)
    return pl.pallas_call(
        functools.partial(_combine_body, alpha),
        grid=(m // (wpb * WIN),),
        in_specs=[
            pl.BlockSpec((wpb * WIN, d), lambda i: (i, 0)),
            pl.BlockSpec((wpb, WIN, LANES), win_map),
            pl.BlockSpec((wpb, WIN_ROWS, d), win_map),
            pl.BlockSpec(vecd.shape, lambda i: (0, 0)),
        ],
        out_specs=pl.BlockSpec((wpb * WIN, d), lambda i: (i, 0)),
        out_shape=jax.ShapeDtypeStruct((m, d), F32),
        compiler_params=pltpu.CompilerParams(
            dimension_semantics=("arbitrary",), vmem_limit_bytes=VMEM_LIMIT),
        name="combine",
    )(h1, info, ys, vecd)


def kernel(x_prompt, x_sample, state_conv_b, state_conv_c, ln_in_g, ln_in_b, w_in, w_s, b_s, ln_v_g, ln_v_b, conv_b_w, conv_b_bias, ln_conv_g, ln_conv_b, conv_c_w, w_out, ln1_g, ln1_b, w_router_group, w_router_expert, w_gate, w_up, w_down, ln2_g, ln2_b):
    depth = w_in.shape[0]
    nb, t, d = x_prompt.shape
    ns, nt_s, _ = x_sample.shape
    assert t % TM == 0 and (ns * nt_s) % WIN == 0 and TM % WIN == 0 and d % LANES == 0
    alpha = (2.0 * depth) ** 0.25
    n_win_p = nb * t // WIN
    n_win_s = ns * nt_s // WIN
    n_win = n_win_p + n_win_s
    assert n_win_p % n_win_s == 0 and n_win_s % (TM // WIN) == 0
    max_chunks = n_win * (2 * WIN // CH + N_EXPERTS)
    t_max = max_chunks // TILE_CHUNKS + N_EXPERTS

    hp = x_prompt
    hs = jnp.transpose(x_sample, (1, 0, 2)).reshape(nt_s * ns, d)
    sb_t = jnp.transpose(state_conv_b, (0, 2, 1, 3))
    sc_t = jnp.transpose(state_conv_c, (0, 2, 1, 3))
    cb_p, cc_p, cb_s, cc_s, v_s = [], [], [], [], []
    prev = ()
    for l in range(depth):
        wr = jnp.concatenate([w_router_group[l], w_router_expert[l].reshape(d, N_EXPERTS)], axis=1)
        wr = jnp.pad(wr, ((0, 0), (0, LANES - wr.shape[1])))
        wr_hi = wr.astype(BF16)
        wr = jnp.concatenate([wr_hi, (wr - wr_hi.astype(F32)).astype(BF16)], axis=1)
        zeros_a = jnp.zeros((A_WIDTH,), F32)
        zeros_d = jnp.zeros((d,), F32)
        lw = dict(
            w_in=w_in[l].astype(BF16),
            ws2=w_s[l].reshape(A_WIDTH // LANES, 2 * GMLP_CHUNK, GMLP_CHUNK),
            wexp=jnp.repeat(jnp.transpose(w_s[l][:, :nt_s, :nt_s], (1, 2, 0)), HEAD_DIM, axis=-1),
            bs_exp=jnp.repeat(b_s[l].T, HEAD_DIM, axis=-1),
            veca=jnp.stack([ln_v_g[l], ln_v_b[l], conv_b_bias[l], ln_conv_g[l], ln_conv_b[l],
                            zeros_a, zeros_a, zeros_a]),
            cbw=conv_b_w[l], ccw=conv_c_w[l],
            w_out=w_out[l].astype(BF16),
            vecd=jnp.stack([ln1_g[l], ln1_b[l], ln2_g[l], ln2_b[l], ln_in_g, ln_in_b, zeros_d, zeros_d]),
            wr=wr,
        )
        h1s, vs, nbs, ncs, xs_s, info_s, tab_s = _mixer_sample(hs, sb_t[l], sc_t[l], lw, l == 0, alpha)
        h1p, xs, info, tab, nbp, ncp = _mixer_prompt(hp, prev, lw, l == 0, alpha, xs_s, info_s, tab_s)
        ne = N_EXPERTS
        ys = _experts(xs, _expert_tables(tab, t_max), w_gate.reshape(depth * ne, d, -1),
                      w_up.reshape(depth * ne, d, -1), w_down.reshape(depth * ne, -1, d), l)
        hp, prev = h1p, (info, ys, lw["vecd"])
        hs = _combine(h1s, info, ys, lw["vecd"], alpha, n_win_s, n_win_p // n_win_s)
        cb_p.append(nbp)
        cc_p.append(ncp)
        cb_s.append(nbs)
        cc_s.append(ncs)
        v_s.append(vs)
    hp = _combine(hp.reshape(nb * t, d), *prev, alpha, TM // WIN, 0).reshape(nb, t, d)
    y_sample = jnp.transpose(hs.reshape(nt_s, ns, d), (1, 0, 2))
    untime = lambda xs_: jnp.transpose(jnp.stack(xs_), (0, 2, 1, 3))
    chunk_v = jnp.transpose(jnp.stack(v_s).reshape(depth, nt_s, ns, A_WIDTH), (0, 2, 1, 3))
    return (hp, y_sample, jnp.stack(cb_p), jnp.stack(cc_p), untime(cb_s), untime(cc_s), chunk_v)
```

```python
import functools
import math

import jax
import jax.numpy as jnp
from jax import lax
from jax.experimental import pallas as pl
from jax.experimental.pallas import tpu as pltpu

F32 = jnp.float32
BF16 = jnp.bfloat16

HEAD_DIM = 64
A_WIDTH = 384
B_WIDTH = 384
C_WIDTH = 256
GMLP_CHUNK = 128
CONV_B_TAPS = 31
CONV_C_TAPS = 3
N_GROUPS = 4
EXPERTS_PER_GROUP = 8
N_EXPERTS = N_GROUPS * EXPERTS_PER_GROUP
LN_EPS = 1e-5
INV_SQRT2 = 1.0 / math.sqrt(2.0)

LANES = 128
SUBLANES = 8
WIN = 256
CH = 16
WIN_ROWS = -(-(2 * WIN + N_EXPERTS * (CH - 1)) // WIN) * WIN
WIN_CHUNKS = WIN_ROWS // CH
TILE_CHUNKS = 32
TILE_ROWS = TILE_CHUNKS * CH
SUB_ROWS = 256
TM = 512
CONV_RB = 32
DMA_UNROLL = 4
N_SLOTS = 5
GATHER_AHEAD = N_SLOTS - 1
VMEM_LIMIT = 60 * 1024 * 1024


def _ln(x, g, b):
    mu = jnp.mean(x, axis=-1, keepdims=True)
    xc = x - mu
    var = jnp.mean(xc * xc, axis=-1, keepdims=True)
    return xc * lax.rsqrt(var + LN_EPS) * g + b


def _gelu(x):
    return 0.5 * x * (1.0 + lax.erf(x * INV_SQRT2))


def _silu(x):
    return x * jax.nn.sigmoid(x)


def _dot(a, b):
    return jnp.dot(a, b, preferred_element_type=F32)


def _iota(shape, dim):
    return lax.broadcasted_iota(jnp.int32, shape, dim)


def _route_window_steps(logits, hb, xs_ref, info_ref, tab_ref):
    w = logits.shape[0]
    lane = _iota((w, LANES), 1)
    neg = jnp.float32(-jnp.inf)
    gmask = lane < N_GROUPS
    gl = jnp.where(gmask, logits, neg)
    gmax = jnp.max(gl, axis=-1, keepdims=True)
    gsel = jnp.min(jnp.where(gl == gmax, lane, LANES), axis=-1, keepdims=True)
    den = jnp.sum(jnp.where(gmask, jnp.exp(gl - gmax), 0.0), axis=-1, keepdims=True)
    gw = 1.0 / den
    lo = N_GROUPS + EXPERTS_PER_GROUP * gsel
    el = jnp.where((lane >= lo) & (lane < lo + EXPERTS_PER_GROUP), logits, neg)
    v1 = jnp.max(el, axis=-1, keepdims=True)
    i1 = jnp.min(jnp.where(el == v1, lane, LANES), axis=-1, keepdims=True)
    el2 = jnp.where(lane == i1, neg, el)
    v2 = jnp.max(el2, axis=-1, keepdims=True)
    i2 = jnp.min(jnp.where(el2 == v2, lane, LANES), axis=-1, keepdims=True)
    e2x = jnp.exp(v2 - v1)
    w1 = gw / (1.0 + e2x)
    w2 = gw * e2x / (1.0 + e2x)
    e1 = i1 - N_GROUPS
    e2 = i2 - N_GROUPS
    yield

    oh = (lane == e1) | (lane == e2)
    ohf = jnp.where(oh, 1.0, 0.0)
    lstrict = jnp.where(_iota((w, w), 0) > _iota((w, w), 1), 1.0, 0.0).astype(BF16)
    rank = _dot(lstrict, ohf.astype(BF16))
    cnt = jnp.sum(ohf, axis=0, keepdims=True)
    nch = jnp.floor((cnt + (CH - 1.0)) * (1.0 / CH))
    upper = jnp.where(_iota((LANES, LANES), 0) < _iota((LANES, LANES), 1), 1.0, 0.0).astype(BF16)
    off = _dot(jnp.broadcast_to(nch, (SUBLANES, LANES)).astype(BF16), upper)[0:1]
    dest = off * CH + rank
    r1 = jnp.sum(jnp.where(lane == e1, dest, 0.0), axis=-1, keepdims=True)
    r2 = jnp.sum(jnp.where(lane == e2, dest, 0.0), axis=-1, keepdims=True)
    info = jnp.where(lane == 0, r1, jnp.where(lane == 1, r2,
                     jnp.where(lane == 2, w1, jnp.where(lane == 3, w2, 0.0))))
    info_ref[...] = info
    sub = _iota((SUBLANES, LANES), 0)
    tab_ref[...] = jnp.where(sub == 0, nch, jnp.where(sub == 1, off, 0.0)).astype(jnp.int32)
    yield

    info_t = info.T
    rows = _iota((WIN_ROWS, w), 0).astype(F32)
    perm = jnp.where((rows == info_t[0:1, :]) | (rows == info_t[1:2, :]), 1.0, 0.0).astype(BF16)
    yield

    xs_ref[...] = _dot(perm, hb).astype(BF16)
    yield


def _unsort_norm(h1, info, ys, g, b, alpha):
    col = _iota((WIN, WIN_ROWS), 1).astype(F32)
    pt = (jnp.where(col == info[:, 0:1], info[:, 2:3], 0.0)
          + jnp.where(col == info[:, 1:2], info[:, 3:4], 0.0)).astype(BF16)
    return _ln(alpha * h1 + _dot(pt, ys), g, b)


def _post_mix_steps(h, mixin, w_out_ref, vecd_ref, wr_ref, alpha, h1_ref, xs_ref, info_ref, tab_ref):
    mix_out = _dot(mixin, w_out_ref[...])
    yield
    h1 = _ln(alpha * h + mix_out, vecd_ref[0:1, :], vecd_ref[1:2, :])
    h1_ref[...] = h1
    hb = h1.astype(BF16)
    h_lo = (h1 - hb.astype(F32)).astype(BF16)
    yield
    l2 = _dot(hb, wr_ref[...])
    logits = l2[:, 0:LANES] + (l2[:, LANES:] + _dot(h_lo, wr_ref[:, 0:LANES]))
    yield
    yield from _route_window_steps(logits, hb, xs_ref, info_ref, tab_ref)


def _post_mix(h, mixin, w_out_ref, vecd_ref, wr_ref, alpha, h1_ref, xs_ref, info_ref, tab_ref):
    for wi in range(h.shape[0] // WIN):
        rs = slice(wi * WIN, (wi + 1) * WIN)
        for _ in _post_mix_steps(h[rs], mixin[rs], w_out_ref, vecd_ref, wr_ref, alpha,
                                 h1_ref.at[rs], xs_ref.at[wi], info_ref.at[wi], tab_ref.at[wi]):
            pass


def _mixer_prompt_body(first, alpha, nj, n_tiles, *refs):
    n_prev = 0 if first else 3
    h_ref, prev = refs[0], refs[1:1 + n_prev]
    (w_in_ref, ws_ref, bs_ref, veca_ref, cbw_ref, ccw_ref, w_out_ref, vecd_ref, wr_ref,
     xs_s_ref, info_s_ref, tab_s_ref,
     h1_ref, xs_ref, info_ref, tab_ref, nsb_ref, nsc_ref,
     xsh, cbuf, mixin, wb, cbs, hres) = refs[1 + n_prev:]
    i = pl.program_id(0)
    j = jnp.minimum(i, n_tiles - 1) % nj

    @pl.when(i == 0)
    def _():
        mixin[...] = jnp.zeros(mixin.shape, mixin.dtype)
        hres[...] = jnp.zeros(hres.shape, hres.dtype)

    @pl.when(j == 0)
    def _():
        xsh[0, 0:32, :] = jnp.zeros((32, B_WIDTH), F32)
        cbuf[0:8, :] = jnp.zeros((8, C_WIDTH), F32)
        for k in range(CONV_B_TAPS):
            wb[k] = jnp.broadcast_to(cbw_ref[k:k + 1, :], (SUBLANES, B_WIDTH))

    @pl.when(i <= n_tiles)
    def _():
        mix_prev, h_prev = mixin[...], hres[...]

        def back_prev(hf):
            rs = slice(hf * WIN, (hf + 1) * WIN)
            return _post_mix_steps(h_prev[rs], mix_prev[rs], w_out_ref, vecd_ref, wr_ref, alpha,
                                   h1_ref.at[0, rs], xs_ref.at[hf], info_ref.at[hf], tab_ref.at[hf])

        new_b, new_c = _mixer_prompt_tile(first, alpha, h_ref, prev, w_in_ref, ws_ref, bs_ref, veca_ref,
                                          vecd_ref, ccw_ref, xsh, cbuf, mixin, wb, cbs, hres, back_prev)

        @pl.when((i < n_tiles) & (j == nj - 1))
        def _():
            nsb_ref[0] = new_b
            nsc_ref[0] = new_c

    @pl.when(i > n_tiles)
    def _():
        xs_ref[...] = xs_s_ref[...]
        info_ref[...] = info_s_ref[...]
        tab_ref[...] = tab_s_ref[...]


def _mixer_prompt_tile(first, alpha, h_ref, prev, w_in_ref, ws_ref, bs_ref, veca_ref, vecd_ref,
                       ccw_ref, xsh, cbuf, mixin, wb, cbs, hres, back_prev):
    past_b = CONV_B_TAPS - 1
    past_c = CONV_C_TAPS - 1
    b0 = 32 - past_b
    c0 = 8 - past_c
    n_half = TM // WIN
    oa, ob = 2 * A_WIDTH, 2 * A_WIDTH + 2 * B_WIDTH

    tri =(_iota((2 * GMLP_CHUNK, GMLP_CHUNK), 0) % GMLP_CHUNK) >= _iota((2 * GMLP_CHUNK, GMLP_CHUNK), 1)
    wms = [jnp.where(tri, ws_ref[p], 0.0).astype(BF16) for p in range(A_WIDTH // LANES)]
    lane = _iota((GMLP_CHUNK, LANES), 1)

    bgs = []
    for hf in range(n_half):
        r0 = hf * WIN
        h = h_ref[0, r0:r0 + WIN, :]
        if first:
            h = _ln(h, vecd_ref[4:5, :], vecd_ref[5:6, :])
        else:
            info_p, ys_p, vecd_p = prev
            h = _unsort_norm(h, info_p[hf], ys_p[hf], vecd_p[2:3, :], vecd_p[3:4, :], alpha)
        hres[r0:r0 + WIN, :] = h
        hb = h.astype(BF16)
        za = _dot(hb, w_in_ref[:, 0:oa])
        zb = _dot(hb, w_in_ref[:, oa:ob])
        zc = _dot(hb, w_in_ref[:, ob:ob + 3 * C_WIDTH])

        ga = _gelu(za)
        u = ga[:, 0:A_WIDTH]
        vb = _ln(ga[:, A_WIDTH:], veca_ref[0:1, :], veca_ref[1:2, :]).astype(BF16)
        for c in range(WIN // GMLP_CHUNK):
            rs = slice(c * GMLP_CHUNK, (c + 1) * GMLP_CHUNK)
            parts = []
            for p in range(A_WIDTH // LANES):
                ab = _dot(wms[p], vb[rs, p * LANES:(p + 1) * LANES])
                parts.append(jnp.where(lane < HEAD_DIM, ab[:GMLP_CHUNK], ab[GMLP_CHUNK:]))
            mix = jnp.concatenate(parts, axis=1) + bs_ref[...]
            mixin[r0 + c * GMLP_CHUNK:r0 + (c + 1) * GMLP_CHUNK, 0:A_WIDTH] = (u[rs] * mix).astype(BF16)

        xsh[0, 32 + r0:32 + r0 + WIN, :] = zb[:, 0:B_WIDTH] * jax.nn.sigmoid(zb[:, B_WIDTH:])
        lo = 0 if hf == 0 else r0 + 24
        for r in range(1, SUBLANES):
            xsh[r, lo:r0 + WIN + 24, :] = xsh[0, lo + r:r0 + WIN + 24 + r, :]

        bgs.append(zc[:, 0:C_WIDTH])
        cbuf[8 + r0:8 + r0 + WIN, :] = zc[:, C_WIDTH:2 * C_WIDTH] * zc[:, 2 * C_WIDTH:]

    n_sub = CONV_RB // SUBLANES
    for hf in range(n_half):
        r0 = hf * WIN
        back = back_prev(hf)
        for rb in range(WIN // CONV_RB):
            next(back, None)
            base = r0 + rb * CONV_RB
            accs = [jnp.broadcast_to(veca_ref[2:3, :], (SUBLANES, B_WIDTH))] * n_sub
            for k in range(CONV_B_TAPS):
                s = k + b0
                w8 = wb[k]
                for a in range(n_sub):
                    row = base + (s // SUBLANES + a) * SUBLANES
                    accs[a] = accs[a] + w8 * xsh[s % SUBLANES, row:row + SUBLANES, :]
            for a in range(n_sub):
                cbs[base + a * SUBLANES:base + (a + 1) * SUBLANES, :] = accs[a]
        yb = _silu(_ln(cbs[r0:r0 + WIN, :], veca_ref[3:4, :], veca_ref[4:5, :]))
        mixin[r0:r0 + WIN, A_WIDTH:A_WIDTH + B_WIDTH] = yb.astype(BF16)

        cc = ccw_ref[0:1, :] * cbuf[c0 + r0:c0 + r0 + WIN, :]
        for k in range(1, CONV_C_TAPS):
            cc = cc + ccw_ref[k:k + 1, :] * cbuf[c0 + r0 + k:c0 + r0 + k + WIN, :]
        mixin[r0:r0 + WIN, A_WIDTH + B_WIDTH:] = (bgs[hf] * cc).astype(BF16)
        for _ in back:
            pass

    new_b = xsh[0, TM + b0:TM + 32, :]
    xsh[0, b0:32, :] = new_b
    new_c = cbuf[TM + c0:TM + 8, :]
    cbuf[c0:8, :] = new_c
    return new_b, new_c


def _mixer_prompt(h, prev, lw, first, alpha, xs_s, info_s, tab_s):
    nb, t, d = h.shape
    nj = t // TM
    wpt = TM // WIN
    n_tiles = nb * nj
    n_extra = xs_s.shape[0] // wpt
    n_win_total = (n_tiles + n_extra) * wpt
    const = lambda shape: pl.BlockSpec(shape, lambda i: (0,) * len(shape), pipeline_mode=pl.Buffered(1))
    tile = lambda i: jnp.minimum(i, n_tiles - 1)
    h_map = lambda i: (tile(i) // nj, tile(i) % nj, 0)
    state_map = lambda i: (tile(i) // nj, 0, 0)
    extra_map = lambda i: (jnp.maximum(i - n_tiles - 1, 0), 0, 0)
    lag = lambda i: jnp.maximum(i - 1, 0)
    h1_map = lambda i: (tile(lag(i)) // nj, tile(lag(i)) % nj, 0)
    prev_win_map = lambda i: (tile(i), 0, 0)
    prev_specs = [] if first else [pl.BlockSpec((wpt, WIN, LANES), prev_win_map),
                                   pl.BlockSpec((wpt, WIN_ROWS, d), prev_win_map),
                                   const(prev[2].shape)]
    out_shape = (
        jax.ShapeDtypeStruct((nb, t, d), F32),
        jax.ShapeDtypeStruct((n_win_total, WIN_ROWS, d), BF16),
        jax.ShapeDtypeStruct((n_win_total, WIN, LANES), F32),
        jax.ShapeDtypeStruct((n_win_total, SUBLANES, LANES), jnp.int32),
        jax.ShapeDtypeStruct((nb, CONV_B_TAPS - 1, B_WIDTH), F32),
        jax.ShapeDtypeStruct((nb, CONV_C_TAPS - 1, C_WIDTH), F32),
    )
    win_map = lambda i: (lag(i), 0, 0)
    return pl.pallas_call(
        functools.partial(_mixer_prompt_body, first, alpha, nj, n_tiles),
        grid=(n_tiles + 1 + n_extra,),
        in_specs=[pl.BlockSpec((1, TM, d), h_map)] + prev_specs + [
            const(lw["w_in"].shape), const(lw["ws2"].shape), const(lw["bs_exp"].shape),
            const(lw["veca"].shape), const(lw["cbw"].shape), const(lw["ccw"].shape),
            const(lw["w_out"].shape), const(lw["vecd"].shape), const(lw["wr"].shape),
            pl.BlockSpec((wpt, WIN_ROWS, d), extra_map),
            pl.BlockSpec((wpt, WIN, LANES), extra_map),
            pl.BlockSpec((wpt, SUBLANES, LANES), extra_map),
        ],
        out_specs=(
            pl.BlockSpec((1, TM, d), h1_map),
            pl.BlockSpec((wpt, WIN_ROWS, d), win_map),
            pl.BlockSpec((wpt, WIN, LANES), win_map),
            pl.BlockSpec((wpt, SUBLANES, LANES), win_map),
            pl.BlockSpec((1, CONV_B_TAPS - 1, B_WIDTH), state_map),
            pl.BlockSpec((1, CONV_C_TAPS - 1, C_WIDTH), state_map),
        ),
        out_shape=out_shape,
        scratch_shapes=[
            pltpu.VMEM((SUBLANES, TM + 32, B_WIDTH), F32),
            pltpu.VMEM((TM + 8, C_WIDTH), F32),
            pltpu.VMEM((TM, d), BF16),
            pltpu.VMEM((CONV_B_TAPS, SUBLANES, B_WIDTH), F32),
            pltpu.VMEM((TM, B_WIDTH), F32),
            pltpu.VMEM((TM, d), F32),
        ],
        compiler_params=pltpu.CompilerParams(
            dimension_semantics=("arbitrary",), vmem_limit_bytes=VMEM_LIMIT),
        name="mixer_prompt",
    )(h, *prev, lw["w_in"], lw["ws2"], lw["bs_exp"], lw["veca"], lw["cbw"], lw["ccw"],
      lw["w_out"], lw["vecd"], lw["wr"], xs_s, info_s, tab_s)


def _mixer_sample_body(first, alpha, n_seq, n_t,
                       h_ref, sb_ref, sc_ref, w_in_ref, wexp_ref, bs_ref, veca_ref, cbw_ref, ccw_ref,
                       w_out_ref, vecd_ref, wr_ref,
                       h1_ref, v_ref, nsb_ref, nsc_ref, xs_ref, info_ref, tab_ref,
                       mixin):
    h = h_ref[...]
    if first:
        h = _ln(h, vecd_ref[4:5, :], vecd_ref[5:6, :])
    hb = h.astype(BF16)
    rows = lambda t: slice(t * n_seq, (t + 1) * n_seq)

    u = _gelu(_dot(hb, w_in_ref[:, 0:A_WIDTH]))
    v = _ln(_gelu(_dot(hb, w_in_ref[:, A_WIDTH:2 * A_WIDTH])), veca_ref[0:1, :], veca_ref[1:2, :])
    v_ref[...] = v
    for t in range(n_t):
        mix = jnp.broadcast_to(bs_ref[t:t + 1, :], (n_seq, A_WIDTH))
        for s in range(t + 1):
            mix = mix + wexp_ref[t, s:s + 1, :] * v[rows(s)]
        mixin[rows(t), 0:A_WIDTH] = (u[rows(t)] * mix).astype(BF16)

    o = 2 * A_WIDTH
    glu = _dot(hb, w_in_ref[:, o:o + B_WIDTH]) * jax.nn.sigmoid(_dot(hb, w_in_ref[:, o + B_WIDTH:o + 2 * B_WIDTH]))
    past_b = CONV_B_TAPS - 1
    xp = lambda m: sb_ref[m] if m < past_b else glu[rows(m - past_b)]
    for t in range(n_t):
        acc = jnp.broadcast_to(veca_ref[2:3, :], (n_seq, B_WIDTH))
        for k in range(CONV_B_TAPS):
            acc = acc + cbw_ref[k:k + 1, :] * xp(t + k)
        yb = _silu(_ln(acc, veca_ref[3:4, :], veca_ref[4:5, :]))
        mixin[rows(t), A_WIDTH:A_WIDTH + B_WIDTH] = yb.astype(BF16)
    for r in range(past_b):
        nsb_ref[r] = xp(r + n_t)

    o = 2 * A_WIDTH + 2 * B_WIDTH
    bg = _dot(hb, w_in_ref[:, o:o + C_WIDTH])
    xc = _dot(hb, w_in_ref[:, o + C_WIDTH:o + 2 * C_WIDTH]) * _dot(hb, w_in_ref[:, o + 2 * C_WIDTH:o + 3 * C_WIDTH])
    past_c = CONV_C_TAPS - 1
    xq = lambda m: sc_ref[m] if m < past_c else xc[rows(m - past_c)]
    for t in range(n_t):
        cc = ccw_ref[0:1, :] * xq(t)
        for k in range(1, CONV_C_TAPS):
            cc = cc + ccw_ref[k:k + 1, :] * xq(t + k)
        mixin[rows(t), A_WIDTH + B_WIDTH:] = (bg[rows(t)] * cc).astype(BF16)
    for r in range(past_c):
        nsc_ref[r] = xq(r + n_t)

    _post_mix(h, mixin[...], w_out_ref, vecd_ref, wr_ref, alpha, h1_ref, xs_ref, info_ref, tab_ref)


def _mixer_sample(h, sb_t, sc_t, lw, first, alpha):
    m, d = h.shape
    n_seq = sb_t.shape[1]
    n_t = m // n_seq
    nw = m // WIN
    full = lambda a: pl.BlockSpec(a.shape, lambda i: (0,) * a.ndim)
    ins = [h, sb_t, sc_t, lw["w_in"], lw["wexp"], lw["bs_exp"], lw["veca"], lw["cbw"], lw["ccw"],
           lw["w_out"], lw["vecd"], lw["wr"]]
    out_shape = (
        jax.ShapeDtypeStruct((m, d), F32),
        jax.ShapeDtypeStruct((m, A_WIDTH), F32),
        jax.ShapeDtypeStruct(sb_t.shape, F32),
        jax.ShapeDtypeStruct(sc_t.shape, F32),
        jax.ShapeDtypeStruct((nw, WIN_ROWS, d), BF16),
        jax.ShapeDtypeStruct((nw, WIN, LANES), F32),
        jax.ShapeDtypeStruct((nw, SUBLANES, LANES), jnp.int32),
    )
    return pl.pallas_call(
        functools.partial(_mixer_sample_body, first, alpha, n_seq, n_t),
        grid=(1,),
        in_specs=[full(a) for a in ins],
        out_specs=tuple(full(o) for o in out_shape),
        out_shape=out_shape,
        scratch_shapes=[pltpu.VMEM((m, d), BF16)],
        compiler_params=pltpu.CompilerParams(
            dimension_semantics=("arbitrary",), vmem_limit_bytes=VMEM_LIMIT),
        name="mixer_sample",
    )(*ins)


def _experts_body(layer, tile_e_ref, fresh_ref, wslot_ref, next_e_ref, nvalid_ref, clist_ref, ntiles_ref,
                  xs_hbm, wg_hbm, wu_hbm, wd_hbm, ys_hbm,
                  lhs, obuf, sg, su, sd, wgb, wub, wdb, sem_in, sem_out, sem_w):
    nt = ntiles_ref[0]

    def copy_in(cid, slot, j):
        return pltpu.make_async_copy(xs_hbm.at[cid], lhs.at[slot, pl.ds(j * CH, CH)], sem_in.at[slot])

    def copy_out(cid, slot, j):
        return pltpu.make_async_copy(obuf.at[slot, pl.ds(j * CH, CH)], ys_hbm.at[cid], sem_out.at[slot])

    def weight_copies(e, ws):
        row = layer * N_EXPERTS + e
        return [pltpu.make_async_copy(src.at[row], dst.at[ws], sem_w.at[ws])
                for src, dst in ((wg_hbm, sg), (wu_hbm, su), (wd_hbm, sd))]

    def start_all(make, tt, slot):
        n = nvalid_ref[tt]
        base = tt * TILE_CHUNKS
        groups = n // DMA_UNROLL

        def group(g, carry):
            for k in range(DMA_UNROLL):
                j = g * DMA_UNROLL + k
                make(clist_ref[base + j], slot, j).start()
            return carry

        def single(j, carry):
            make(clist_ref[base + j], slot, j).start()
            return carry

        lax.fori_loop(0, groups, group, 0)
        lax.fori_loop(groups * DMA_UNROLL, n, single, 0)

    def wait_all(make, tt, slot):
        def body(j, carry):
            make(0, slot, j).wait()
            return carry
        lax.fori_loop(0, nvalid_ref[tt], body, 0)

    lhs[...] = jnp.zeros(lhs.shape, lhs.dtype)
    for c in weight_copies(tile_e_ref[0], 0):
        c.start()
    for tt in range(GATHER_AHEAD):
        @pl.when(tt < nt)
        def _():
            start_all(copy_in, tt, tt % N_SLOTS)

    def tile(t, carry):
        slot = t % N_SLOTS
        t_gather = t + GATHER_AHEAD
        t_drain = t - N_SLOTS

        @pl.when(fresh_ref[t] == 1)
        def _():
            ws = wslot_ref[t]
            for c in weight_copies(0, ws):
                c.wait()
            wgb[...] = sg[ws].astype(BF16)
            wub[...] = su[ws].astype(BF16)
            wdb[...] = sd[ws].astype(BF16)

            @pl.when(next_e_ref[t] >= 0)
            def _():
                for c in weight_copies(next_e_ref[t], 1 - ws):
                    c.start()

        def ffn(rows):
            x = lhs[slot, 0:rows, :]
            act = _silu(_dot(x, wgb[...])) * _dot(x, wub[...])
            obuf[slot, 0:rows, :] = _dot(act.astype(BF16), wdb[...]).astype(BF16)

        wait_all(copy_in, t, slot)

        @pl.when(t_drain >= 0)
        def _():
            wait_all(copy_out, t_drain, slot)

        @pl.when(t_gather < nt)
        def _():
            start_all(copy_in, t_gather, t_gather % N_SLOTS)

        @pl.when(t >= 1)
        def _():
            start_all(copy_out, t - 1, (t - 1) % N_SLOTS)

        @pl.when(nvalid_ref[t] > SUB_ROWS // CH)
        def _():
            ffn(TILE_ROWS)

        @pl.when(nvalid_ref[t] <= SUB_ROWS // CH)
        def _():
            ffn(SUB_ROWS)

        @pl.when(t == nt - 1)
        def _():
            start_all(copy_out, t, slot)

        return carry

    lax.fori_loop(0, nt, tile, 0)
    for back in range(N_SLOTS, 0, -1):
        @pl.when(nt >= back)
        def _():
            wait_all(copy_out, nt - back, (nt - back) % N_SLOTS)


def _experts(xs, tables, wg, wu, wd, layer):
    n_win, _, d = xs.shape
    xc = xs.reshape(n_win * WIN_CHUNKS, CH, d)
    de = wg.shape[-1]
    anyspec = pl.BlockSpec(memory_space=pl.ANY)
    grid_spec = pltpu.PrefetchScalarGridSpec(
        num_scalar_prefetch=len(tables),
        grid=(1,),
        in_specs=[anyspec, anyspec, anyspec, anyspec],
        out_specs=anyspec,
        scratch_shapes=[
            pltpu.VMEM((N_SLOTS, TILE_ROWS, d), BF16),
            pltpu.VMEM((N_SLOTS, TILE_ROWS, d), BF16),
            pltpu.VMEM((2, d, de), F32),
            pltpu.VMEM((2, d, de), F32),
            pltpu.VMEM((2, de, d), F32),
            pltpu.VMEM((d, de), BF16),
            pltpu.VMEM((d, de), BF16),
            pltpu.VMEM((de, d), BF16),
            pltpu.SemaphoreType.DMA((N_SLOTS,)),
            pltpu.SemaphoreType.DMA((N_SLOTS,)),
            pltpu.SemaphoreType.DMA((2,)),
        ],
    )
    ys = pl.pallas_call(
        functools.partial(_experts_body, layer),
        grid_spec=grid_spec,
        out_shape=jax.ShapeDtypeStruct(xc.shape, xc.dtype),
        input_output_aliases={len(tables): 0},
        compiler_params=pltpu.CompilerParams(
            dimension_semantics=("arbitrary",), vmem_limit_bytes=VMEM_LIMIT),
        name="experts",
    )(*tables, xc, wg, wu, wd)
    return ys.reshape(xs.shape)


def _expert_tables(tab, t_max):
    i32 = jnp.int32
    nch = tab[:, 0, :N_EXPERTS]
    off = tab[:, 1, :N_EXPERTS]
    n_win = nch.shape[0]
    cum_incl = jnp.cumsum(nch, axis=0)
    cum_excl = cum_incl - nch
    ce = cum_incl[-1]
    te = (ce + TILE_CHUNKS - 1) // TILE_CHUNKS
    tile_end = jnp.cumsum(te)
    tile_start = tile_end - te
    nt = tile_end[-1]
    t_idx = jnp.arange(t_max, dtype=i32)
    tile_e = jnp.minimum(jnp.sum((tile_end[None, :] <= t_idx[:, None]).astype(i32), axis=1), N_EXPERTS - 1)
    sel_e = (tile_e[:, None] == jnp.arange(N_EXPERTS, dtype=i32)[None, :]).astype(i32)
    pick = lambda v: jnp.sum(sel_e * v[None, :], axis=1)
    q0 = (t_idx - pick(tile_start)) * TILE_CHUNKS
    nvalid = jnp.where(t_idx < nt, jnp.clip(pick(ce) - q0, 0, TILE_CHUNKS), 0).astype(i32)
    q = q0[:, None] + jnp.arange(TILE_CHUNKS, dtype=i32)[None, :]
    by_win = lambda m: jnp.sum(sel_e[:, :, None] * m.T[None, :, :], axis=1)
    cum_e = by_win(cum_incl)
    w = jnp.minimum(jnp.sum((cum_e[:, None, :] <= q[:, :, None]).astype(i32), axis=-1), n_win - 1)
    sel_w = (w[:, :, None] == jnp.arange(n_win, dtype=i32)[None, None, :]).astype(i32)
    shift = jnp.sum(sel_w * by_win(off - cum_excl)[:, None, :], axis=-1)
    cid = w * WIN_CHUNKS + shift + q
    valid = jnp.arange(TILE_CHUNKS, dtype=i32)[None, :] < nvalid[:, None]
    clist = jnp.where(valid, cid, 0).astype(i32).reshape(-1)
    prev_e = jnp.concatenate([jnp.full((1,), -1, i32), tile_e[:-1]])
    fresh = ((t_idx < nt) & (tile_e != prev_e)).astype(i32)
    wslot = (jnp.cumsum(fresh) - 1) % 2
    later_fresh = (t_idx[None, :] > t_idx[:, None]) & (fresh[None, :] == 1)
    nxt = jnp.min(jnp.where(later_fresh, t_idx[None, :], t_max), axis=1)
    next_e = jnp.where(nxt < t_max, jnp.sum((t_idx[None, :] == nxt[:, None]).astype(i32) * tile_e[None, :], axis=1), -1)
    return (tile_e.astype(i32), fresh, wslot.astype(i32), next_e.astype(i32), nvalid, clist,
            nt.astype(i32).reshape(1))


def _combine_body(alpha, h1_ref, info_ref, ys_ref, vecd_ref, out_ref):
    for wi in range(info_ref.shape[0]):
        rs = slice(wi * WIN, (wi + 1) * WIN)
        out_ref[rs, :] = _unsort_norm(h1_ref[rs, :], info_ref[wi], ys_ref[wi],
                                      vecd_ref[2:3, :], vecd_ref[3:4, :], alpha)


def _combine(h1, info, ys, vecd, alpha, wpb, block0):
    m, d = h1.shape
    win_map = lambda i: (block0 + i, 0, 0)
    return pl.pallas_call(
        functools.partial(_combine_body, alpha),
        grid=(m // (wpb * WIN),),
        in_specs=[
            pl.BlockSpec((wpb * WIN, d), lambda i: (i, 0)),
            pl.BlockSpec((wpb, WIN, LANES), win_map),
            pl.BlockSpec((wpb, WIN_ROWS, d), win_map),
            pl.BlockSpec(vecd.shape, lambda i: (0, 0)),
        ],
        out_specs=pl.BlockSpec((wpb * WIN, d), lambda i: (i, 0)),
        out_shape=jax.ShapeDtypeStruct((m, d), F32),
        compiler_params=pltpu.CompilerParams(
            dimension_semantics=("arbitrary",), vmem_limit_bytes=VMEM_LIMIT),
        name="combine",
    )(h1, info, ys, vecd)


def kernel(x_prompt, x_sample, state_conv_b, state_conv_c, ln_in_g, ln_in_b, w_in, w_s, b_s, ln_v_g, ln_v_b, conv_b_w, conv_b_bias, ln_conv_g, ln_conv_b, conv_c_w, w_out, ln1_g, ln1_b, w_router_group, w_router_expert, w_gate, w_up, w_down, ln2_g, ln2_b):
    depth = w_in.shape[0]
    nb, t, d = x_prompt.shape
    ns, nt_s, _ = x_sample.shape
    assert t % TM == 0 and (ns * nt_s) % WIN == 0 and TM % WIN == 0 and d % LANES == 0
    alpha = (2.0 * depth) ** 0.25
    n_win_p = nb * t // WIN
    n_win_s = ns * nt_s // WIN
    n_win = n_win_p + n_win_s
    assert n_win_p % n_win_s == 0 and n_win_s % (TM // WIN) == 0
    max_chunks = n_win * (2 * WIN // CH + N_EXPERTS)
    t_max = max_chunks // TILE_CHUNKS + N_EXPERTS

    hp = x_prompt
    hs = jnp.transpose(x_sample, (1, 0, 2)).reshape(nt_s * ns, d)
    sb_t = jnp.transpose(state_conv_b, (0, 2, 1, 3))
    sc_t = jnp.transpose(state_conv_c, (0, 2, 1, 3))
    cb_p, cc_p, cb_s, cc_s, v_s = [], [], [], [], []
    prev = ()
    for l in range(depth):
        wr = jnp.concatenate([w_router_group[l], w_router_expert[l].reshape(d, N_EXPERTS)], axis=1)
        wr = jnp.pad(wr, ((0, 0), (0, LANES - wr.shape[1])))
        wr_hi = wr.astype(BF16)
        wr = jnp.concatenate([wr_hi, (wr - wr_hi.astype(F32)).astype(BF16)], axis=1)
        zeros_a = jnp.zeros((A_WIDTH,), F32)
        zeros_d = jnp.zeros((d,), F32)
        lw = dict(
            w_in=w_in[l].astype(BF16),
            ws2=w_s[l].reshape(A_WIDTH // LANES, 2 * GMLP_CHUNK, GMLP_CHUNK),
            wexp=jnp.repeat(jnp.transpose(w_s[l][:, :nt_s, :nt_s], (1, 2, 0)), HEAD_DIM, axis=-1),
            bs_exp=jnp.repeat(b_s[l].T, HEAD_DIM, axis=-1),
            veca=jnp.stack([ln_v_g[l], ln_v_b[l], conv_b_bias[l], ln_conv_g[l], ln_conv_b[l],
                            zeros_a, zeros_a, zeros_a]),
            cbw=conv_b_w[l], ccw=conv_c_w[l],
            w_out=w_out[l].astype(BF16),
            vecd=jnp.stack([ln1_g[l], ln1_b[l], ln2_g[l], ln2_b[l], ln_in_g, ln_in_b, zeros_d, zeros_d]),
            wr=wr,
        )
        h1s, vs, nbs, ncs, xs_s, info_s, tab_s = _mixer_sample(hs, sb_t[l], sc_t[l], lw, l == 0, alpha)
        h1p, xs, info, tab, nbp, ncp = _mixer_prompt(hp, prev, lw, l == 0, alpha, xs_s, info_s, tab_s)
        ne = N_EXPERTS
        ys = _experts(xs, _expert_tables(tab, t_max), w_gate.reshape(depth * ne, d, -1),
                      w_up.reshape(depth * ne, d, -1), w_down.reshape(depth * ne, -1, d), l)
        hp, prev = h1p, (info, ys, lw["vecd"])
        hs = _combine(h1s, info, ys, lw["vecd"], alpha, n_win_s, n_win_p // n_win_s)
        cb_p.append(nbp)
        cc_p.append(ncp)
        cb_s.append(nbs)
        cc_s.append(ncs)
        v_s.append(vs)
    hp = _combine(hp.reshape(nb * t, d), *prev, alpha, TM // WIN, 0).reshape(nb, t, d)
    y_sample = jnp.transpose(hs.reshape(nt_s, ns, d), (1, 0, 2))
    untime = lambda xs_: jnp.transpose(jnp.stack(xs_), (0, 2, 1, 3))
    chunk_v = jnp.transpose(jnp.stack(v_s).reshape(depth, nt_s, ns, A_WIDTH), (0, 2, 1, 3))
    return (hp, y_sample, jnp.stack(cb_p), jnp.stack(cc_p), untime(cb_s), untime(cc_s), chunk_v)
```

```python
import functools
import math

import jax
import jax.numpy as jnp
from jax import lax
from jax.experimental import pallas as pl
from jax.experimental.pallas import tpu as pltpu

F32 = jnp.float32
BF16 = jnp.bfloat16

HEAD_DIM = 64
A_WIDTH = 384
B_WIDTH = 384
C_WIDTH = 256
GMLP_CHUNK = 128
CONV_B_TAPS = 31
CONV_C_TAPS = 3
N_GROUPS = 4
EXPERTS_PER_GROUP = 8
N_EXPERTS = N_GROUPS * EXPERTS_PER_GROUP
LN_EPS = 1e-5
INV_SQRT2 = 1.0 / math.sqrt(2.0)

LANES = 128
SUBLANES = 8
WIN = 256
CH = 16
WIN_ROWS = -(-(2 * WIN + N_EXPERTS * (CH - 1)) // WIN) * WIN
WIN_CHUNKS = WIN_ROWS // CH
TILE_CHUNKS = 32
TILE_ROWS = TILE_CHUNKS * CH
TM = 512
CONV_RB = 32
DMA_UNROLL = 4
N_SLOTS = 3
GATHER_AHEAD = N_SLOTS - 1
VMEM_LIMIT = 56 * 1024 * 1024


def _ln(x, g, b):
    mu = jnp.mean(x, axis=-1, keepdims=True)
    xc = x - mu
    var = jnp.mean(xc * xc, axis=-1, keepdims=True)
    return xc * lax.rsqrt(var + LN_EPS) * g + b


def _gelu(x):
    return 0.5 * x * (1.0 + lax.erf(x * INV_SQRT2))


def _silu(x):
    return x * jax.nn.sigmoid(x)


def _dot(a, b):
    return jnp.dot(a, b, preferred_element_type=F32)


def _iota(shape, dim):
    return lax.broadcasted_iota(jnp.int32, shape, dim)


def _route_window(logits, hb):
    w = logits.shape[0]
    lane = _iota((w, LANES), 1)
    neg = jnp.float32(-jnp.inf)
    gmask = lane < N_GROUPS
    gl = jnp.where(gmask, logits, neg)
    gmax = jnp.max(gl, axis=-1, keepdims=True)
    gsel = jnp.min(jnp.where(gl == gmax, lane, LANES), axis=-1, keepdims=True)
    den = jnp.sum(jnp.where(gmask, jnp.exp(gl - gmax), 0.0), axis=-1, keepdims=True)
    gw = 1.0 / den
    lo = N_GROUPS + EXPERTS_PER_GROUP * gsel
    el = jnp.where((lane >= lo) & (lane < lo + EXPERTS_PER_GROUP), logits, neg)
    v1 = jnp.max(el, axis=-1, keepdims=True)
    i1 = jnp.min(jnp.where(el == v1, lane, LANES), axis=-1, keepdims=True)
    el2 = jnp.where(lane == i1, neg, el)
    v2 = jnp.max(el2, axis=-1, keepdims=True)
    i2 = jnp.min(jnp.where(el2 == v2, lane, LANES), axis=-1, keepdims=True)
    e2x = jnp.exp(v2 - v1)
    w1 = gw / (1.0 + e2x)
    w2 = gw * e2x / (1.0 + e2x)
    e1 = i1 - N_GROUPS
    e2 = i2 - N_GROUPS

    oh = (lane == e1) | (lane == e2)
    ohf = jnp.where(oh, 1.0, 0.0)
    lstrict = jnp.where(_iota((w, w), 0) > _iota((w, w), 1), 1.0, 0.0).astype(BF16)
    rank = _dot(lstrict, ohf.astype(BF16))
    cnt = jnp.sum(ohf, axis=0, keepdims=True)
    nch = jnp.floor((cnt + (CH - 1.0)) * (1.0 / CH))
    upper = jnp.where(_iota((LANES, LANES), 0) < _iota((LANES, LANES), 1), 1.0, 0.0).astype(BF16)
    off = _dot(jnp.broadcast_to(nch, (SUBLANES, LANES)).astype(BF16), upper)[0:1]
    dest = off * CH + rank
    r1 = jnp.sum(jnp.where(lane == e1, dest, 0.0), axis=-1, keepdims=True)
    r2 = jnp.sum(jnp.where(lane == e2, dest, 0.0), axis=-1, keepdims=True)
    info = jnp.where(lane == 0, r1, jnp.where(lane == 1, r2,
                     jnp.where(lane == 2, w1, jnp.where(lane == 3, w2, 0.0))))
    info_t = info.T
    rows = _iota((WIN_ROWS, w), 0).astype(F32)
    perm = jnp.where((rows == info_t[0:1, :]) | (rows == info_t[1:2, :]), 1.0, 0.0).astype(BF16)
    xs = _dot(perm, hb).astype(BF16)
    sub = _iota((SUBLANES, LANES), 0)
    tab = jnp.where(sub == 0, nch, jnp.where(sub == 1, off, 0.0)).astype(jnp.int32)
    return xs, info, tab


def _unsort_norm(h1, info, ys, g, b, alpha):
    col = _iota((WIN, WIN_ROWS), 1).astype(F32)
    pt = (jnp.where(col == info[:, 0:1], info[:, 2:3], 0.0)
          + jnp.where(col == info[:, 1:2], info[:, 3:4], 0.0)).astype(BF16)
    return _ln(alpha * h1 + _dot(pt, ys), g, b)


def _post_mix(h, mixin, w_out_ref, vecd_ref, wr_ref, alpha, h1_ref, xs_ref, info_ref, tab_ref):
    mix_out = _dot(mixin, w_out_ref[...])
    h1 = _ln(alpha * h + mix_out, vecd_ref[0:1, :], vecd_ref[1:2, :])
    h1_ref[...] = h1
    hb = h1.astype(BF16)
    h_lo = (h1 - hb.astype(F32)).astype(BF16)
    l2 = _dot(hb, wr_ref[...])
    logits = l2[:, 0:LANES] + (l2[:, LANES:] + _dot(h_lo, wr_ref[:, 0:LANES]))
    for wi in range(h.shape[0] // WIN):
        rs = slice(wi * WIN, (wi + 1) * WIN)
        xs, info, tab = _route_window(logits[rs], hb[rs])
        xs_ref[wi] = xs
        info_ref[wi] = info
        tab_ref[wi] = tab


def _mixer_prompt_body(first, alpha, nj, n_tiles, *refs):
    n_prev = 0 if first else 3
    h_ref, prev = refs[0], refs[1:1 + n_prev]
    (w_in_ref, ws_ref, bs_ref, veca_ref, cbw_ref, ccw_ref, w_out_ref, vecd_ref, wr_ref,
     xs_s_ref, info_s_ref, tab_s_ref,
     h1_ref, xs_ref, info_ref, tab_ref, nsb_ref, nsc_ref,
     xsh, cbuf, mixin, wb, cbs) = refs[1 + n_prev:]
    i = pl.program_id(0)
    j = i % nj

    @pl.when((i < n_tiles) & (j == 0))
    def _():
        xsh[0, 0:32, :] = jnp.zeros((32, B_WIDTH), F32)
        cbuf[0:8, :] = jnp.zeros((8, C_WIDTH), F32)
        for k in range(CONV_B_TAPS):
            wb[k] = jnp.broadcast_to(cbw_ref[k:k + 1, :], (SUBLANES, B_WIDTH))

    @pl.when(i < n_tiles)
    def _():
        new_b, new_c = _mixer_prompt_tile(first, alpha, h_ref, prev, w_in_ref, ws_ref, bs_ref, veca_ref,
                                          ccw_ref, w_out_ref, vecd_ref, wr_ref,
                                          h1_ref, xs_ref, info_ref, tab_ref, xsh, cbuf, mixin, wb, cbs)

        @pl.when(j == nj - 1)
        def _():
            nsb_ref[0] = new_b
            nsc_ref[0] = new_c

    @pl.when(i >= n_tiles)
    def _():
        xs_ref[...] = xs_s_ref[...]
        info_ref[...] = info_s_ref[...]
        tab_ref[...] = tab_s_ref[...]


def _mixer_prompt_tile(first, alpha, h_ref, prev, w_in_ref, ws_ref, bs_ref, veca_ref, ccw_ref,
                       w_out_ref, vecd_ref, wr_ref, h1_ref, xs_ref, info_ref, tab_ref,
                       xsh, cbuf, mixin, wb, cbs):
    past_b = CONV_B_TAPS - 1
    past_c = CONV_C_TAPS - 1
    b0 = 32 - past_b
    c0 = 8 - past_c
    n_half = TM // WIN
    oa, ob = 2 * A_WIDTH, 2 * A_WIDTH + 2 * B_WIDTH

    tri = (_iota((2 * GMLP_CHUNK, GMLP_CHUNK), 0) % GMLP_CHUNK) >= _iota((2 * GMLP_CHUNK, GMLP_CHUNK), 1)
    wms = [jnp.where(tri, ws_ref[p], 0.0).astype(BF16) for p in range(A_WIDTH // LANES)]
    lane = _iota((GMLP_CHUNK, LANES), 1)

    hs, bgs = [], []
    for hf in range(n_half):
        r0 = hf * WIN
        h = h_ref[0, r0:r0 + WIN, :]
        if first:
            h = _ln(h, vecd_ref[4:5, :], vecd_ref[5:6, :])
        else:
            info_p, ys_p, vecd_p = prev
            h = _unsort_norm(h, info_p[hf], ys_p[hf], vecd_p[2:3, :], vecd_p[3:4, :], alpha)
        hs.append(h)
        hb = h.astype(BF16)
        za = _dot(hb, w_in_ref[:, 0:oa])
        zb = _dot(hb, w_in_ref[:, oa:ob])
        zc = _dot(hb, w_in_ref[:, ob:ob + 3 * C_WIDTH])

        ga = _gelu(za)
        u = ga[:, 0:A_WIDTH]
        vb = _ln(ga[:, A_WIDTH:], veca_ref[0:1, :], veca_ref[1:2, :]).astype(BF16)
        for c in range(WIN // GMLP_CHUNK):
            rs = slice(c * GMLP_CHUNK, (c + 1) * GMLP_CHUNK)
            parts = []
            for p in range(A_WIDTH // LANES):
                ab = _dot(wms[p], vb[rs, p * LANES:(p + 1) * LANES])
                parts.append(jnp.where(lane < HEAD_DIM, ab[:GMLP_CHUNK], ab[GMLP_CHUNK:]))
            mix = jnp.concatenate(parts, axis=1) + bs_ref[...]
            mixin[r0 + c * GMLP_CHUNK:r0 + (c + 1) * GMLP_CHUNK, 0:A_WIDTH] = (u[rs] * mix).astype(BF16)

        xsh[0, 32 + r0:32 + r0 + WIN, :] = zb[:, 0:B_WIDTH] * jax.nn.sigmoid(zb[:, B_WIDTH:])
        lo = 0 if hf == 0 else r0 + 24
        for r in range(1, SUBLANES):
            xsh[r, lo:r0 + WIN + 24, :] = xsh[0, lo + r:r0 + WIN + 24 + r, :]

        bgs.append(zc[:, 0:C_WIDTH])
        cbuf[8 + r0:8 + r0 + WIN, :] = zc[:, C_WIDTH:2 * C_WIDTH] * zc[:, 2 * C_WIDTH:]

    n_sub = CONV_RB // SUBLANES
    for hf in range(n_half):
        r0 = hf * WIN
        for rb in range(WIN // CONV_RB):
            base = r0 + rb * CONV_RB
            accs = [jnp.broadcast_to(veca_ref[2:3, :], (SUBLANES, B_WIDTH))] * n_sub
            for k in range(CONV_B_TAPS):
                s = k + b0
                w8 = wb[k]
                for a in range(n_sub):
                    row = base + (s // SUBLANES + a) * SUBLANES
                    accs[a] = accs[a] + w8 * xsh[s % SUBLANES, row:row + SUBLANES, :]
            for a in range(n_sub):
                cbs[base + a * SUBLANES:base + (a + 1) * SUBLANES, :] = accs[a]
        yb = _silu(_ln(cbs[r0:r0 + WIN, :], veca_ref[3:4, :], veca_ref[4:5, :]))
        mixin[r0:r0 + WIN, A_WIDTH:A_WIDTH + B_WIDTH] = yb.astype(BF16)

        cc = ccw_ref[0:1, :] * cbuf[c0 + r0:c0 + r0 + WIN, :]
        for k in range(1, CONV_C_TAPS):
            cc = cc + ccw_ref[k:k + 1, :] * cbuf[c0 + r0 + k:c0 + r0 + k + WIN, :]
        mixin[r0:r0 + WIN, A_WIDTH + B_WIDTH:] = (bgs[hf] * cc).astype(BF16)

        _post_mix(hs[hf], mixin[r0:r0 + WIN, :], w_out_ref, vecd_ref, wr_ref, alpha,
                  h1_ref.at[0, r0:r0 + WIN], xs_ref.at[hf:hf + 1], info_ref.at[hf:hf + 1],
                  tab_ref.at[hf:hf + 1])

    new_b = xsh[0, TM + b0:TM + 32, :]
    xsh[0, b0:32, :] = new_b
    new_c = cbuf[TM + c0:TM + 8, :]
    cbuf[c0:8, :] = new_c
    return new_b, new_c


def _mixer_prompt(h, prev, lw, first, alpha, xs_s, info_s, tab_s):
    nb, t, d = h.shape
    nj = t // TM
    wpt = TM // WIN
    n_tiles = nb * nj
    n_extra = xs_s.shape[0] // wpt
    n_win_total = (n_tiles + n_extra) * wpt
    const = lambda shape: pl.BlockSpec(shape, lambda i: (0,) * len(shape))
    tile = lambda i: jnp.minimum(i, n_tiles - 1)
    h_map = lambda i: (tile(i) // nj, tile(i) % nj, 0)
    state_map = lambda i: (tile(i) // nj, 0, 0)
    extra_map = lambda i: (jnp.maximum(i - n_tiles, 0), 0, 0)
    prev_win_map = lambda i: (tile(i), 0, 0)
    prev_specs = [] if first else [pl.BlockSpec((wpt, WIN, LANES), prev_win_map),
                                   pl.BlockSpec((wpt, WIN_ROWS, d), prev_win_map),
                                   const(prev[2].shape)]
    out_shape = (
        jax.ShapeDtypeStruct((nb, t, d), F32),
        jax.ShapeDtypeStruct((n_win_total, WIN_ROWS, d), BF16),
        jax.ShapeDtypeStruct((n_win_total, WIN, LANES), F32),
        jax.ShapeDtypeStruct((n_win_total, SUBLANES, LANES), jnp.int32),
        jax.ShapeDtypeStruct((nb, CONV_B_TAPS - 1, B_WIDTH), F32),
        jax.ShapeDtypeStruct((nb, CONV_C_TAPS - 1, C_WIDTH), F32),
    )
    win_map = lambda i: (i, 0, 0)
    return pl.pallas_call(
        functools.partial(_mixer_prompt_body, first, alpha, nj, n_tiles),
        grid=(n_tiles + n_extra,),
        in_specs=[pl.BlockSpec((1, TM, d), h_map)] + prev_specs + [
            const(lw["w_in"].shape), const(lw["ws2"].shape), const(lw["bs_exp"].shape),
            const(lw["veca"].shape), const(lw["cbw"].shape), const(lw["ccw"].shape),
            const(lw["w_out"].shape), const(lw["vecd"].shape), const(lw["wr"].shape),
            pl.BlockSpec((wpt, WIN_ROWS, d), extra_map),
            pl.BlockSpec((wpt, WIN, LANES), extra_map),
            pl.BlockSpec((wpt, SUBLANES, LANES), extra_map),
        ],
        out_specs=(
            pl.BlockSpec((1, TM, d), h_map),
            pl.BlockSpec((wpt, WIN_ROWS, d), win_map),
            pl.BlockSpec((wpt, WIN, LANES), win_map),
            pl.BlockSpec((wpt, SUBLANES, LANES), win_map),
            pl.BlockSpec((1, CONV_B_TAPS - 1, B_WIDTH), state_map),
            pl.BlockSpec((1, CONV_C_TAPS - 1, C_WIDTH), state_map),
        ),
        out_shape=out_shape,
        scratch_shapes=[
            pltpu.VMEM((SUBLANES, TM + 32, B_WIDTH), F32),
            pltpu.VMEM((TM + 8, C_WIDTH), F32),
            pltpu.VMEM((TM, d), BF16),
            pltpu.VMEM((CONV_B_TAPS, SUBLANES, B_WIDTH), F32),
            pltpu.VMEM((TM, B_WIDTH), F32),
        ],
        compiler_params=pltpu.CompilerParams(
            dimension_semantics=("arbitrary",), vmem_limit_bytes=VMEM_LIMIT),
        name="mixer_prompt",
    )(h, *prev, lw["w_in"], lw["ws2"], lw["bs_exp"], lw["veca"], lw["cbw"], lw["ccw"],
      lw["w_out"], lw["vecd"], lw["wr"], xs_s, info_s, tab_s)


def _mixer_sample_body(first, alpha, n_seq, n_t,
                       h_ref, sb_ref, sc_ref, w_in_ref, wexp_ref, bs_ref, veca_ref, cbw_ref, ccw_ref,
                       w_out_ref, vecd_ref, wr_ref,
                       h1_ref, v_ref, nsb_ref, nsc_ref, xs_ref, info_ref, tab_ref,
                       mixin):
    h = h_ref[...]
    if first:
        h = _ln(h, vecd_ref[4:5, :], vecd_ref[5:6, :])
    hb = h.astype(BF16)
    rows = lambda t: slice(t * n_seq, (t + 1) * n_seq)

    u = _gelu(_dot(hb, w_in_ref[:, 0:A_WIDTH]))
    v = _ln(_gelu(_dot(hb, w_in_ref[:, A_WIDTH:2 * A_WIDTH])), veca_ref[0:1, :], veca_ref[1:2, :])
    v_ref[...] = v
    for t in range(n_t):
        mix = jnp.broadcast_to(bs_ref[t:t + 1, :], (n_seq, A_WIDTH))
        for s in range(t + 1):
            mix = mix + wexp_ref[t, s:s + 1, :] * v[rows(s)]
        mixin[rows(t), 0:A_WIDTH] = (u[rows(t)] * mix).astype(BF16)

    o = 2 * A_WIDTH
    glu = _dot(hb, w_in_ref[:, o:o + B_WIDTH]) * jax.nn.sigmoid(_dot(hb, w_in_ref[:, o + B_WIDTH:o + 2 * B_WIDTH]))
    past_b = CONV_B_TAPS - 1
    xp = lambda m: sb_ref[m] if m < past_b else glu[rows(m - past_b)]
    for t in range(n_t):
        acc = jnp.broadcast_to(veca_ref[2:3, :], (n_seq, B_WIDTH))
        for k in range(CONV_B_TAPS):
            acc = acc + cbw_ref[k:k + 1, :] * xp(t + k)
        yb = _silu(_ln(acc, veca_ref[3:4, :], veca_ref[4:5, :]))
        mixin[rows(t), A_WIDTH:A_WIDTH + B_WIDTH] = yb.astype(BF16)
    for r in range(past_b):
        nsb_ref[r] = xp(r + n_t)

    o = 2 * A_WIDTH + 2 * B_WIDTH
    bg = _dot(hb, w_in_ref[:, o:o + C_WIDTH])
    xc = _dot(hb, w_in_ref[:, o + C_WIDTH:o + 2 * C_WIDTH]) * _dot(hb, w_in_ref[:, o + 2 * C_WIDTH:o + 3 * C_WIDTH])
    past_c = CONV_C_TAPS - 1
    xq = lambda m: sc_ref[m] if m < past_c else xc[rows(m - past_c)]
    for t in range(n_t):
        cc = ccw_ref[0:1, :] * xq(t)
        for k in range(1, CONV_C_TAPS):
            cc = cc + ccw_ref[k:k + 1, :] * xq(t + k)
        mixin[rows(t), A_WIDTH + B_WIDTH:] = (bg[rows(t)] * cc).astype(BF16)
    for r in range(past_c):
        nsc_ref[r] = xq(r + n_t)

    _post_mix(h, mixin[...], w_out_ref, vecd_ref, wr_ref, alpha, h1_ref, xs_ref, info_ref, tab_ref)


def _mixer_sample(h, sb_t, sc_t, lw, first, alpha):
    m, d = h.shape
    n_seq = sb_t.shape[1]
    n_t = m // n_seq
    nw = m // WIN
    full = lambda a: pl.BlockSpec(a.shape, lambda i: (0,) * a.ndim)
    ins = [h, sb_t, sc_t, lw["w_in"], lw["wexp"], lw["bs_exp"], lw["veca"], lw["cbw"], lw["ccw"],
           lw["w_out"], lw["vecd"], lw["wr"]]
    out_shape = (
        jax.ShapeDtypeStruct((m, d), F32),
        jax.ShapeDtypeStruct((m, A_WIDTH), F32),
        jax.ShapeDtypeStruct(sb_t.shape, F32),
        jax.ShapeDtypeStruct(sc_t.shape, F32),
        jax.ShapeDtypeStruct((nw, WIN_ROWS, d), BF16),
        jax.ShapeDtypeStruct((nw, WIN, LANES), F32),
        jax.ShapeDtypeStruct((nw, SUBLANES, LANES), jnp.int32),
    )
    return pl.pallas_call(
        functools.partial(_mixer_sample_body, first, alpha, n_seq, n_t),
        grid=(1,),
        in_specs=[full(a) for a in ins],
        out_specs=tuple(full(o) for o in out_shape),
        out_shape=out_shape,
        scratch_shapes=[pltpu.VMEM((m, d), BF16)],
        compiler_params=pltpu.CompilerParams(
            dimension_semantics=("arbitrary",), vmem_limit_bytes=VMEM_LIMIT),
        name="mixer_sample",
    )(*ins)


def _experts_body(layer, tile_e_ref, fresh_ref, wslot_ref, next_e_ref, nvalid_ref, clist_ref, ntiles_ref,
                  xs_hbm, wg_hbm, wu_hbm, wd_hbm, ys_hbm,
                  lhs, obuf, sg, su, sd, wgb, wub, wdb, sem_in, sem_out, sem_w):
    nt = ntiles_ref[0]

    def copy_in(cid, slot, j):
        return pltpu.make_async_copy(xs_hbm.at[cid], lhs.at[slot, pl.ds(j * CH, CH)], sem_in.at[slot])

    def copy_out(cid, slot, j):
        return pltpu.make_async_copy(obuf.at[slot, pl.ds(j * CH, CH)], ys_hbm.at[cid], sem_out.at[slot])

    def weight_copies(e, ws):
        row = layer * N_EXPERTS + e
        return [pltpu.make_async_copy(src.at[row], dst.at[ws], sem_w.at[ws])
                for src, dst in ((wg_hbm, sg), (wu_hbm, su), (wd_hbm, sd))]

    def start_all(make, tt, slot):
        n = nvalid_ref[tt]
        base = tt * TILE_CHUNKS
        groups = n // DMA_UNROLL

        def group(g, carry):
            for k in range(DMA_UNROLL):
                j = g * DMA_UNROLL + k
                make(clist_ref[base + j], slot, j).start()
            return carry

        def single(j, carry):
            make(clist_ref[base + j], slot, j).start()
            return carry

        lax.fori_loop(0, groups, group, 0)
        lax.fori_loop(groups * DMA_UNROLL, n, single, 0)

    def wait_all(make, tt, slot):
        def body(j, carry):
            make(0, slot, j).wait()
            return carry
        lax.fori_loop(0, nvalid_ref[tt], body, 0)

    def start_full(make, tt, slot):
        for j in range(TILE_CHUNKS):
            make(clist_ref[tt * TILE_CHUNKS + j], slot, j).start()

    def wait_full(make, slot):
        for j in range(TILE_CHUNKS):
            make(0, slot, j).wait()

    for c in weight_copies(tile_e_ref[0], 0):
        c.start()
    for tt in range(GATHER_AHEAD):
        @pl.when(tt < nt)
        def _():
            start_full(copy_in, tt, tt % N_SLOTS)

    def tile(t, carry):
        slot = t % N_SLOTS
        t_gather = t + GATHER_AHEAD
        t_drain = t - N_SLOTS

        @pl.when(fresh_ref[t] == 1)
        def _():
            ws = wslot_ref[t]
            for c in weight_copies(0, ws):
                c.wait()
            wgb[...] = sg[ws].astype(BF16)
            wub[...] = su[ws].astype(BF16)
            wdb[...] = sd[ws].astype(BF16)

            @pl.when(next_e_ref[t] >= 0)
            def _():
                for c in weight_copies(next_e_ref[t], 1 - ws):
                    c.start()

        def ffn():
            x = lhs[slot]
            act = _silu(_dot(x, wgb[...])) * _dot(x, wub[...])
            obuf[slot] = _dot(act.astype(BF16), wdb[...]).astype(BF16)

        wait_full(copy_in, slot)

        @pl.when(t_drain >= 0)
        def _():
            wait_all(copy_out, t_drain, slot)

        gather_next = t_gather < nt
        scatter_full = (t >= 1) & (nvalid_ref[jnp.maximum(t - 1, 0)] == TILE_CHUNKS)

        @pl.when((t >= 1) & jnp.logical_not(gather_next & scatter_full))
        def _():
            start_all(copy_out, t - 1, (t - 1) % N_SLOTS)

        @pl.when(gather_next & scatter_full)
        def _():
            start_full(copy_in, t_gather, t_gather % N_SLOTS)
            start_full(copy_out, t - 1, (t - 1) % N_SLOTS)
            ffn()

        @pl.when(gather_next & jnp.logical_not(scatter_full))
        def _():
            start_full(copy_in, t_gather, t_gather % N_SLOTS)
            ffn()

        @pl.when(jnp.logical_not(gather_next))
        def _():
            ffn()

        @pl.when(t == nt - 1)
        def _():
            start_all(copy_out, t, slot)

        return carry

    lax.fori_loop(0, nt, tile, 0)
    for back in range(N_SLOTS, 0, -1):
        @pl.when(nt >= back)
        def _():
            wait_all(copy_out, nt - back, (nt - back) % N_SLOTS)


def _experts(xs, tables, wg, wu, wd, layer):
    n_win, _, d = xs.shape
    xc = xs.reshape(n_win * WIN_CHUNKS, CH, d)
    de = wg.shape[-1]
    anyspec = pl.BlockSpec(memory_space=pl.ANY)
    grid_spec = pltpu.PrefetchScalarGridSpec(
        num_scalar_prefetch=len(tables),
        grid=(1,),
        in_specs=[anyspec, anyspec, anyspec, anyspec],
        out_specs=anyspec,
        scratch_shapes=[
            pltpu.VMEM((N_SLOTS, TILE_ROWS, d), BF16),
            pltpu.VMEM((N_SLOTS, TILE_ROWS, d), BF16),
            pltpu.VMEM((2, d, de), F32),
            pltpu.VMEM((2, d, de), F32),
            pltpu.VMEM((2, de, d), F32),
            pltpu.VMEM((d, de), BF16),
            pltpu.VMEM((d, de), BF16),
            pltpu.VMEM((de, d), BF16),
            pltpu.SemaphoreType.DMA((N_SLOTS,)),
            pltpu.SemaphoreType.DMA((N_SLOTS,)),
            pltpu.SemaphoreType.DMA((2,)),
        ],
    )
    ys = pl.pallas_call(
        functools.partial(_experts_body, layer),
        grid_spec=grid_spec,
        out_shape=jax.ShapeDtypeStruct(xc.shape, xc.dtype),
        input_output_aliases={len(tables): 0},
        compiler_params=pltpu.CompilerParams(
            dimension_semantics=("arbitrary",), vmem_limit_bytes=VMEM_LIMIT),
        name="experts",
    )(*tables, xc, wg, wu, wd)
    return ys.reshape(xs.shape)


def _expert_tables(tab, t_max):
    i32 = jnp.int32
    nch = tab[:, 0, :N_EXPERTS]
    off = tab[:, 1, :N_EXPERTS]
    n_win = nch.shape[0]
    cum_incl = jnp.cumsum(nch, axis=0)
    cum_excl = cum_incl - nch
    ce = cum_incl[-1]
    te = (ce + TILE_CHUNKS - 1) // TILE_CHUNKS
    tile_end = jnp.cumsum(te)
    tile_start = tile_end - te
    nt = tile_end[-1]
    t_idx = jnp.arange(t_max, dtype=i32)
    tile_e = jnp.minimum(jnp.sum((tile_end[None, :] <= t_idx[:, None]).astype(i32), axis=1), N_EXPERTS - 1)
    sel_e = (tile_e[:, None] == jnp.arange(N_EXPERTS, dtype=i32)[None, :]).astype(i32)
    pick = lambda v: jnp.sum(sel_e * v[None, :], axis=1)
    q0 = (t_idx - pick(tile_start)) * TILE_CHUNKS
    nvalid = jnp.where(t_idx < nt, jnp.clip(pick(ce) - q0, 0, TILE_CHUNKS), 0).astype(i32)
    q = q0[:, None] + jnp.arange(TILE_CHUNKS, dtype=i32)[None, :]
    by_win = lambda m: jnp.sum(sel_e[:, :, None] * m.T[None, :, :], axis=1)
    cum_e = by_win(cum_incl)
    w = jnp.minimum(jnp.sum((cum_e[:, None, :] <= q[:, :, None]).astype(i32), axis=-1), n_win - 1)
    sel_w = (w[:, :, None] == jnp.arange(n_win, dtype=i32)[None, None, :]).astype(i32)
    shift = jnp.sum(sel_w * by_win(off - cum_excl)[:, None, :], axis=-1)
    cid = w * WIN_CHUNKS + shift + q
    valid = jnp.arange(TILE_CHUNKS, dtype=i32)[None, :] < nvalid[:, None]
    clist = jnp.where(valid, cid, cid[:, 0:1]).astype(i32).reshape(-1)
    prev_e = jnp.concatenate([jnp.full((1,), -1, i32), tile_e[:-1]])
    fresh = ((t_idx < nt) & (tile_e != prev_e)).astype(i32)
    wslot = (jnp.cumsum(fresh) - 1) % 2
    later_fresh = (t_idx[None, :] > t_idx[:, None]) & (fresh[None, :] == 1)
    nxt = jnp.min(jnp.where(later_fresh, t_idx[None, :], t_max), axis=1)
    next_e = jnp.where(nxt < t_max, jnp.sum((t_idx[None, :] == nxt[:, None]).astype(i32) * tile_e[None, :], axis=1), -1)
    return (tile_e.astype(i32), fresh, wslot.astype(i32), next_e.astype(i32), nvalid, clist,
            nt.astype(i32).reshape(1))


def _combine_body(alpha, h1_ref, info_ref, ys_ref, vecd_ref, out_ref):
    for wi in range(info_ref.shape[0]):
        rs = slice(wi * WIN, (wi + 1) * WIN)
        out_ref[rs, :] = _unsort_norm(h1_ref[rs, :], info_ref[wi], ys_ref[wi],
                                      vecd_ref[2:3, :], vecd_ref[3:4, :], alpha)


def _combine(h1, info, ys, vecd, alpha, wpb, block0):
    m, d = h1.shape
    win_map = lambda i: (block0 + i, 0, 0)
    return pl.pallas_call(
        functools.partial(_combine_body, alpha),
        grid=(m // (wpb * WIN),),
        in_specs=[
            pl.BlockSpec((wpb * WIN, d), lambda i: (i, 0)),
            pl.BlockSpec((wpb, WIN, LANES), win_map),
            pl.BlockSpec((wpb, WIN_ROWS, d), win_map),
            pl.BlockSpec(vecd.shape, lambda i: (0, 0)),
        ],
        out_specs=pl.BlockSpec((wpb * WIN, d), lambda i: (i, 0)),
        out_shape=jax.ShapeDtypeStruct((m, d), F32),
        compiler_params=pltpu.CompilerParams(
            dimension_semantics=("arbitrary",), vmem_limit_bytes=VMEM_LIMIT),
        name="combine",
    )(h1, info, ys, vecd)


def kernel(x_prompt, x_sample, state_conv_b, state_conv_c, ln_in_g, ln_in_b, w_in, w_s, b_s, ln_v_g, ln_v_b, conv_b_w, conv_b_bias, ln_conv_g, ln_conv_b, conv_c_w, w_out, ln1_g, ln1_b, w_router_group, w_router_expert, w_gate, w_up, w_down, ln2_g, ln2_b):
    depth = w_in.shape[0]
    nb, t, d = x_prompt.shape
    ns, nt_s, _ = x_sample.shape
    assert t % TM == 0 and (ns * nt_s) % WIN == 0 and TM % WIN == 0 and d % LANES == 0
    alpha = (2.0 * depth) ** 0.25
    n_win_p = nb * t // WIN
    n_win_s = ns * nt_s // WIN
    n_win = n_win_p + n_win_s
    assert n_win_p % n_win_s == 0 and n_win_s % (TM // WIN) == 0
    max_chunks = n_win * (2 * WIN // CH + N_EXPERTS)
    t_max = max_chunks // TILE_CHUNKS + N_EXPERTS

    hp = x_prompt
    hs = jnp.transpose(x_sample, (1, 0, 2)).reshape(nt_s * ns, d)
    sb_t = jnp.transpose(state_conv_b, (0, 2, 1, 3))
    sc_t = jnp.transpose(state_conv_c, (0, 2, 1, 3))
    cb_p, cc_p, cb_s, cc_s, v_s = [], [], [], [], []
    prev = ()
    for l in range(depth):
        wr = jnp.concatenate([w_router_group[l], w_router_expert[l].reshape(d, N_EXPERTS)], axis=1)
        wr = jnp.pad(wr, ((0, 0), (0, LANES - wr.shape[1])))
        wr_hi = wr.astype(BF16)
        wr = jnp.concatenate([wr_hi, (wr - wr_hi.astype(F32)).astype(BF16)], axis=1)
        zeros_a = jnp.zeros((A_WIDTH,), F32)
        zeros_d = jnp.zeros((d,), F32)
        lw = dict(
            w_in=w_in[l].astype(BF16),
            ws2=w_s[l].reshape(A_WIDTH // LANES, 2 * GMLP_CHUNK, GMLP_CHUNK),
            wexp=jnp.repeat(jnp.transpose(w_s[l][:, :nt_s, :nt_s], (1, 2, 0)), HEAD_DIM, axis=-1),
            bs_exp=jnp.repeat(b_s[l].T, HEAD_DIM, axis=-1),
            veca=jnp.stack([ln_v_g[l], ln_v_b[l], conv_b_bias[l], ln_conv_g[l], ln_conv_b[l],
                            zeros_a, zeros_a, zeros_a]),
            cbw=conv_b_w[l], ccw=conv_c_w[l],
            w_out=w_out[l].astype(BF16),
            vecd=jnp.stack([ln1_g[l], ln1_b[l], ln2_g[l], ln2_b[l], ln_in_g, ln_in_b, zeros_d, zeros_d]),
            wr=wr,
        )
        h1s, vs, nbs, ncs, xs_s, info_s, tab_s = _mixer_sample(hs, sb_t[l], sc_t[l], lw, l == 0, alpha)
        h1p, xs, info, tab, nbp, ncp = _mixer_prompt(hp, prev, lw, l == 0, alpha, xs_s, info_s, tab_s)
        ne = N_EXPERTS
        ys = _experts(xs, _expert_tables(tab, t_max), w_gate.reshape(depth * ne, d, -1),
                      w_up.reshape(depth * ne, d, -1), w_down.reshape(depth * ne, -1, d), l)
        hp, prev = h1p, (info, ys, lw["vecd"])
        hs = _combine(h1s, info, ys, lw["vecd"], alpha, n_win_s, n_win_p // n_win_s)
        cb_p.append(nbp)
        cc_p.append(ncp)
        cb_s.append(nbs)
        cc_s.append(ncs)
        v_s.append(vs)
    hp = _combine(hp.reshape(nb * t, d), *prev, alpha, TM // WIN, 0).reshape(nb, t, d)
    y_sample = jnp.transpose(hs.reshape(nt_s, ns, d), (1, 0, 2))
    untime = lambda xs_: jnp.transpose(jnp.stack(xs_), (0, 2, 1, 3))
    chunk_v = jnp.transpose(jnp.stack(v_s).reshape(depth, nt_s, ns, A_WIDTH), (0, 2, 1, 3))
    return (hp, y_sample, jnp.stack(cb_p), jnp.stack(cc_p), untime(cb_s), untime(cc_s), chunk_v)
```

```python
import functools
import math

import jax
import jax.numpy as jnp
from jax import lax
from jax.experimental import pallas as pl
from jax.experimental.pallas import tpu as pltpu

F32 = jnp.float32
BF16 = jnp.bfloat16

HEAD_DIM = 64
A_WIDTH = 384
B_WIDTH = 384
C_WIDTH = 256
GMLP_CHUNK = 128
CONV_B_TAPS = 31
CONV_C_TAPS = 3
N_GROUPS = 4
EXPERTS_PER_GROUP = 8
N_EXPERTS = N_GROUPS * EXPERTS_PER_GROUP
LN_EPS = 1e-5
INV_SQRT2 = 1.0 / math.sqrt(2.0)

LANES = 128
SUBLANES = 8
WIN = 256
CH = 16
WIN_ROWS = -(-(2 * WIN + N_EXPERTS * (CH - 1)) // WIN) * WIN
WIN_CHUNKS = WIN_ROWS // CH
TILE_CHUNKS = 32
TILE_ROWS = TILE_CHUNKS * CH
TM = 512
CONV_RB = 32
DMA_UNROLL = 4
N_SLOTS = 3
GATHER_AHEAD = N_SLOTS - 1
VMEM_LIMIT = 56 * 1024 * 1024


def _ln(x, g, b):
    mu = jnp.mean(x, axis=-1, keepdims=True)
    xc = x - mu
    var = jnp.mean(xc * xc, axis=-1, keepdims=True)
    return xc * lax.rsqrt(var + LN_EPS) * g + b


def _gelu(x):
    return 0.5 * x * (1.0 + lax.erf(x * INV_SQRT2))


def _silu(x):
    return x * jax.nn.sigmoid(x)


def _dot(a, b):
    return jnp.dot(a, b, preferred_element_type=F32)


def _iota(shape, dim):
    return lax.broadcasted_iota(jnp.int32, shape, dim)


def _route_window(logits, hb):
    w = logits.shape[0]
    lane = _iota((w, LANES), 1)
    neg = jnp.float32(-jnp.inf)
    gmask = lane < N_GROUPS
    gl = jnp.where(gmask, logits, neg)
    gmax = jnp.max(gl, axis=-1, keepdims=True)
    gsel = jnp.min(jnp.where(gl == gmax, lane, LANES), axis=-1, keepdims=True)
    den = jnp.sum(jnp.where(gmask, jnp.exp(gl - gmax), 0.0), axis=-1, keepdims=True)
    gw = 1.0 / den
    lo = N_GROUPS + EXPERTS_PER_GROUP * gsel
    el = jnp.where((lane >= lo) & (lane < lo + EXPERTS_PER_GROUP), logits, neg)
    v1 = jnp.max(el, axis=-1, keepdims=True)
    i1 = jnp.min(jnp.where(el == v1, lane, LANES), axis=-1, keepdims=True)
    el2 = jnp.where(lane == i1, neg, el)
    v2 = jnp.max(el2, axis=-1, keepdims=True)
    i2 = jnp.min(jnp.where(el2 == v2, lane, LANES), axis=-1, keepdims=True)
    e2x = jnp.exp(v2 - v1)
    w1 = gw / (1.0 + e2x)
    w2 = gw * e2x / (1.0 + e2x)
    e1 = i1 - N_GROUPS
    e2 = i2 - N_GROUPS

    oh = (lane == e1) | (lane == e2)
    ohf = jnp.where(oh, 1.0, 0.0)
    lstrict = jnp.where(_iota((w, w), 0) > _iota((w, w), 1), 1.0, 0.0).astype(BF16)
    rank = _dot(lstrict, ohf.astype(BF16))
    cnt = jnp.sum(ohf, axis=0, keepdims=True)
    nch = jnp.floor((cnt + (CH - 1.0)) * (1.0 / CH))
    upper = jnp.where(_iota((LANES, LANES), 0) < _iota((LANES, LANES), 1), 1.0, 0.0).astype(BF16)
    off = _dot(jnp.broadcast_to(nch, (SUBLANES, LANES)).astype(BF16), upper)[0:1]
    dest = off * CH + rank
    r1 = jnp.sum(jnp.where(lane == e1, dest, 0.0), axis=-1, keepdims=True)
    r2 = jnp.sum(jnp.where(lane == e2, dest, 0.0), axis=-1, keepdims=True)
    info = jnp.where(lane == 0, r1, jnp.where(lane == 1, r2,
                     jnp.where(lane == 2, w1, jnp.where(lane == 3, w2, 0.0))))
    info_t = info.T
    rows = _iota((WIN_ROWS, w), 0).astype(F32)
    perm = jnp.where((rows == info_t[0:1, :]) | (rows == info_t[1:2, :]), 1.0, 0.0).astype(BF16)
    xs = _dot(perm, hb).astype(BF16)
    sub = _iota((SUBLANES, LANES), 0)
    tab = jnp.where(sub == 0, nch, jnp.where(sub == 1, off, 0.0)).astype(jnp.int32)
    return xs, info, tab


def _unsort_norm(h1, info, ys, g, b, alpha):
    col = _iota((WIN, WIN_ROWS), 1).astype(F32)
    pt = jnp.where(col == info[:, 0:1], info[:, 2:3],
                   jnp.where(col == info[:, 1:2], info[:, 3:4], 0.0)).astype(BF16)
    return _ln(alpha * h1 + _dot(pt, ys), g, b)


def _post_mix(h, mixin, w_out_ref, vecd_ref, wr_ref, alpha, h1_ref, xs_ref, info_ref, tab_ref):
    mix_out = _dot(mixin, w_out_ref[...])
    h1 = _ln(alpha * h + mix_out, vecd_ref[0:1, :], vecd_ref[1:2, :])
    h1_ref[...] = h1
    hb = h1.astype(BF16)
    h_lo = (h1 - hb.astype(F32)).astype(BF16)
    l2 = _dot(hb, wr_ref[...])
    logits = l2[:, 0:LANES] + (l2[:, LANES:] + _dot(h_lo, wr_ref[:, 0:LANES]))
    for wi in range(h.shape[0] // WIN):
        rs = slice(wi * WIN, (wi + 1) * WIN)
        xs, info, tab = _route_window(logits[rs], hb[rs])
        xs_ref[wi] = xs
        info_ref[wi] = info
        tab_ref[wi] = tab


def _mixer_prompt_body(first, alpha, nj, n_tiles, *refs):
    n_prev = 0 if first else 3
    h_ref, prev = refs[0], refs[1:1 + n_prev]
    (w_in_ref, ws_ref, bs_ref, veca_ref, cbw_ref, ccw_ref, w_out_ref, vecd_ref, wr_ref,
     xs_s_ref, info_s_ref, tab_s_ref,
     h1_ref, xs_ref, info_ref, tab_ref, nsb_ref, nsc_ref,
     xsh, cbuf, mixin, wb, cbs) = refs[1 + n_prev:]
    i = pl.program_id(0)
    j = i % nj

    @pl.when((i < n_tiles) & (j == 0))
    def _():
        xsh[0, 0:32, :] = jnp.zeros((32, B_WIDTH), F32)
        cbuf[0:8, :] = jnp.zeros((8, C_WIDTH), F32)
        for k in range(CONV_B_TAPS):
            wb[k] = jnp.broadcast_to(cbw_ref[k:k + 1, :], (SUBLANES, B_WIDTH))

    @pl.when(i < n_tiles)
    def _():
        new_b, new_c = _mixer_prompt_tile(first, alpha, h_ref, prev, w_in_ref, ws_ref, bs_ref, veca_ref,
                                          ccw_ref, w_out_ref, vecd_ref, wr_ref,
                                          h1_ref, xs_ref, info_ref, tab_ref, xsh, cbuf, mixin, wb, cbs)

        @pl.when(j == nj - 1)
        def _():
            nsb_ref[0] = new_b
            nsc_ref[0] = new_c

    @pl.when(i >= n_tiles)
    def _():
        xs_ref[...] = xs_s_ref[...]
        info_ref[...] = info_s_ref[...]
        tab_ref[...] = tab_s_ref[...]


def _mixer_prompt_tile(first, alpha, h_ref, prev, w_in_ref, ws_ref, bs_ref, veca_ref, ccw_ref,
                       w_out_ref, vecd_ref, wr_ref, h1_ref, xs_ref, info_ref, tab_ref,
                       xsh, cbuf, mixin, wb, cbs):
    past_b = CONV_B_TAPS - 1
    past_c = CONV_C_TAPS - 1
    b0 = 32 - past_b
    c0 = 8 - past_c
    n_half = TM // WIN
    oa, ob = 2 * A_WIDTH, 2 * A_WIDTH + 2 * B_WIDTH

    tri = (_iota((2 * GMLP_CHUNK, GMLP_CHUNK), 0) % GMLP_CHUNK) >= _iota((2 * GMLP_CHUNK, GMLP_CHUNK), 1)
    wms = [jnp.where(tri, ws_ref[p], 0.0).astype(BF16) for p in range(A_WIDTH // LANES)]
    lane = _iota((GMLP_CHUNK, LANES), 1)

    hs, bgs = [], []
    for hf in range(n_half):
        r0 = hf * WIN
        h = h_ref[0, r0:r0 + WIN, :]
        if first:
            h = _ln(h, vecd_ref[4:5, :], vecd_ref[5:6, :])
        else:
            info_p, ys_p, vecd_p = prev
            h = _unsort_norm(h, info_p[hf], ys_p[hf], vecd_p[2:3, :], vecd_p[3:4, :], alpha)
        hs.append(h)
        hb = h.astype(BF16)
        za = _dot(hb, w_in_ref[:, 0:oa])
        zb = _dot(hb, w_in_ref[:, oa:ob])
        zc = _dot(hb, w_in_ref[:, ob:ob + 3 * C_WIDTH])

        ga = _gelu(za)
        u = ga[:, 0:A_WIDTH]
        vb = _ln(ga[:, A_WIDTH:], veca_ref[0:1, :], veca_ref[1:2, :]).astype(BF16)
        for c in range(WIN // GMLP_CHUNK):
            rs = slice(c * GMLP_CHUNK, (c + 1) * GMLP_CHUNK)
            parts = []
            for p in range(A_WIDTH // LANES):
                ab = _dot(wms[p], vb[rs, p * LANES:(p + 1) * LANES])
                parts.append(jnp.where(lane < HEAD_DIM, ab[:GMLP_CHUNK], ab[GMLP_CHUNK:]))
            mix = jnp.concatenate(parts, axis=1) + bs_ref[...]
            mixin[r0 + c * GMLP_CHUNK:r0 + (c + 1) * GMLP_CHUNK, 0:A_WIDTH] = (u[rs] * mix).astype(BF16)

        xsh[0, 32 + r0:32 + r0 + WIN, :] = zb[:, 0:B_WIDTH] * jax.nn.sigmoid(zb[:, B_WIDTH:])
        lo = 0 if hf == 0 else r0 + 24
        for r in range(1, SUBLANES):
            xsh[r, lo:r0 + WIN + 24, :] = xsh[0, lo + r:r0 + WIN + 24 + r, :]

        bgs.append(zc[:, 0:C_WIDTH])
        cbuf[8 + r0:8 + r0 + WIN, :] = zc[:, C_WIDTH:2 * C_WIDTH] * zc[:, 2 * C_WIDTH:]

    n_sub = CONV_RB // SUBLANES
    for hf in range(n_half):
        r0 = hf * WIN
        for rb in range(WIN // CONV_RB):
            base = r0 + rb * CONV_RB
            accs = [jnp.broadcast_to(veca_ref[2:3, :], (SUBLANES, B_WIDTH))] * n_sub
            for k in range(CONV_B_TAPS):
                s = k + b0
                w8 = wb[k]
                for a in range(n_sub):
                    row = base + (s // SUBLANES + a) * SUBLANES
                    accs[a] = accs[a] + w8 * xsh[s % SUBLANES, row:row + SUBLANES, :]
            for a in range(n_sub):
                cbs[base + a * SUBLANES:base + (a + 1) * SUBLANES, :] = accs[a]
        yb = _silu(_ln(cbs[r0:r0 + WIN, :], veca_ref[3:4, :], veca_ref[4:5, :]))
        mixin[r0:r0 + WIN, A_WIDTH:A_WIDTH + B_WIDTH] = yb.astype(BF16)

        cc = ccw_ref[0:1, :] * cbuf[c0 + r0:c0 + r0 + WIN, :]
        for k in range(1, CONV_C_TAPS):
            cc = cc + ccw_ref[k:k + 1, :] * cbuf[c0 + r0 + k:c0 + r0 + k + WIN, :]
        mixin[r0:r0 + WIN, A_WIDTH + B_WIDTH:] = (bgs[hf] * cc).astype(BF16)

        _post_mix(hs[hf], mixin[r0:r0 + WIN, :], w_out_ref, vecd_ref, wr_ref, alpha,
                  h1_ref.at[0, r0:r0 + WIN], xs_ref.at[hf:hf + 1], info_ref.at[hf:hf + 1],
                  tab_ref.at[hf:hf + 1])

    new_b = xsh[0, TM + b0:TM + 32, :]
    xsh[0, b0:32, :] = new_b
    new_c = cbuf[TM + c0:TM + 8, :]
    cbuf[c0:8, :] = new_c
    return new_b, new_c


def _mixer_prompt(h, prev, lw, first, alpha, xs_s, info_s, tab_s):
    nb, t, d = h.shape
    nj = t // TM
    wpt = TM // WIN
    n_tiles = nb * nj
    n_extra = xs_s.shape[0] // wpt
    n_win_total = (n_tiles + n_extra) * wpt
    const = lambda shape: pl.BlockSpec(shape, lambda i: (0,) * len(shape))
    tile = lambda i: jnp.minimum(i, n_tiles - 1)
    h_map = lambda i: (tile(i) // nj, tile(i) % nj, 0)
    state_map = lambda i: (tile(i) // nj, 0, 0)
    extra_map = lambda i: (jnp.maximum(i - n_tiles, 0), 0, 0)
    prev_win_map = lambda i: (tile(i), 0, 0)
    prev_specs = [] if first else [pl.BlockSpec((wpt, WIN, LANES), prev_win_map),
                                   pl.BlockSpec((wpt, WIN_ROWS, d), prev_win_map),
                                   const(prev[2].shape)]
    out_shape = (
        jax.ShapeDtypeStruct((nb, t, d), F32),
        jax.ShapeDtypeStruct((n_win_total, WIN_ROWS, d), BF16),
        jax.ShapeDtypeStruct((n_win_total, WIN, LANES), F32),
        jax.ShapeDtypeStruct((n_win_total, SUBLANES, LANES), jnp.int32),
        jax.ShapeDtypeStruct((nb, CONV_B_TAPS - 1, B_WIDTH), F32),
        jax.ShapeDtypeStruct((nb, CONV_C_TAPS - 1, C_WIDTH), F32),
    )
    win_map = lambda i: (i, 0, 0)
    return pl.pallas_call(
        functools.partial(_mixer_prompt_body, first, alpha, nj, n_tiles),
        grid=(n_tiles + n_extra,),
        in_specs=[pl.BlockSpec((1, TM, d), h_map)] + prev_specs + [
            const(lw["w_in"].shape), const(lw["ws2"].shape), const(lw["bs_exp"].shape),
            const(lw["veca"].shape), const(lw["cbw"].shape), const(lw["ccw"].shape),
            const(lw["w_out"].shape), const(lw["vecd"].shape), const(lw["wr"].shape),
            pl.BlockSpec((wpt, WIN_ROWS, d), extra_map),
            pl.BlockSpec((wpt, WIN, LANES), extra_map),
            pl.BlockSpec((wpt, SUBLANES, LANES), extra_map),
        ],
        out_specs=(
            pl.BlockSpec((1, TM, d), h_map),
            pl.BlockSpec((wpt, WIN_ROWS, d), win_map),
            pl.BlockSpec((wpt, WIN, LANES), win_map),
            pl.BlockSpec((wpt, SUBLANES, LANES), win_map),
            pl.BlockSpec((1, CONV_B_TAPS - 1, B_WIDTH), state_map),
            pl.BlockSpec((1, CONV_C_TAPS - 1, C_WIDTH), state_map),
        ),
        out_shape=out_shape,
        scratch_shapes=[
            pltpu.VMEM((SUBLANES, TM + 32, B_WIDTH), F32),
            pltpu.VMEM((TM + 8, C_WIDTH), F32),
            pltpu.VMEM((TM, d), BF16),
            pltpu.VMEM((CONV_B_TAPS, SUBLANES, B_WIDTH), F32),
            pltpu.VMEM((TM, B_WIDTH), F32),
        ],
        compiler_params=pltpu.CompilerParams(
            dimension_semantics=("arbitrary",), vmem_limit_bytes=VMEM_LIMIT),
        name="mixer_prompt",
    )(h, *prev, lw["w_in"], lw["ws2"], lw["bs_exp"], lw["veca"], lw["cbw"], lw["ccw"],
      lw["w_out"], lw["vecd"], lw["wr"], xs_s, info_s, tab_s)


def _mixer_sample_body(first, alpha, n_seq, n_t,
                       h_ref, sb_ref, sc_ref, w_in_ref, wexp_ref, bs_ref, veca_ref, cbw_ref, ccw_ref,
                       w_out_ref, vecd_ref, wr_ref,
                       h1_ref, v_ref, nsb_ref, nsc_ref, xs_ref, info_ref, tab_ref,
                       mixin):
    h = h_ref[...]
    if first:
        h = _ln(h, vecd_ref[4:5, :], vecd_ref[5:6, :])
    hb = h.astype(BF16)
    rows = lambda t: slice(t * n_seq, (t + 1) * n_seq)

    oa, ob = 2 * A_WIDTH, 2 * A_WIDTH + 2 * B_WIDTH
    ga = _gelu(_dot(hb, w_in_ref[:, 0:oa]))
    zb = _dot(hb, w_in_ref[:, oa:ob])
    zc = _dot(hb, w_in_ref[:, ob:ob + 3 * C_WIDTH])

    u = ga[:, 0:A_WIDTH]
    v = _ln(ga[:, A_WIDTH:], veca_ref[0:1, :], veca_ref[1:2, :])
    v_ref[...] = v
    for t in range(n_t):
        mix = jnp.broadcast_to(bs_ref[t:t + 1, :], (n_seq, A_WIDTH))
        for s in range(t + 1):
            mix = mix + wexp_ref[t, s:s + 1, :] * v[rows(s)]
        mixin[rows(t), 0:A_WIDTH] = (u[rows(t)] * mix).astype(BF16)

    glu = zb[:, 0:B_WIDTH] * jax.nn.sigmoid(zb[:, B_WIDTH:])
    past_b = CONV_B_TAPS - 1
    xp = lambda m: sb_ref[m] if m < past_b else glu[rows(m - past_b)]
    for t in range(n_t):
        acc = jnp.broadcast_to(veca_ref[2:3, :], (n_seq, B_WIDTH))
        for k in range(CONV_B_TAPS):
            acc = acc + cbw_ref[k:k + 1, :] * xp(t + k)
        yb = _silu(_ln(acc, veca_ref[3:4, :], veca_ref[4:5, :]))
        mixin[rows(t), A_WIDTH:A_WIDTH + B_WIDTH] = yb.astype(BF16)
    for r in range(past_b):
        nsb_ref[r] = xp(r + n_t)

    bg = zc[:, 0:C_WIDTH]
    xc = zc[:, C_WIDTH:2 * C_WIDTH] * zc[:, 2 * C_WIDTH:]
    past_c = CONV_C_TAPS - 1
    xq = lambda m: sc_ref[m] if m < past_c else xc[rows(m - past_c)]
    for t in range(n_t):
        cc = ccw_ref[0:1, :] * xq(t)
        for k in range(1, CONV_C_TAPS):
            cc = cc + ccw_ref[k:k + 1, :] * xq(t + k)
        mixin[rows(t), A_WIDTH + B_WIDTH:] = (bg[rows(t)] * cc).astype(BF16)
    for r in range(past_c):
        nsc_ref[r] = xq(r + n_t)

    _post_mix(h, mixin[...], w_out_ref, vecd_ref, wr_ref, alpha, h1_ref, xs_ref, info_ref, tab_ref)


def _mixer_sample(h, sb_t, sc_t, lw, first, alpha):
    m, d = h.shape
    n_seq = sb_t.shape[1]
    n_t = m // n_seq
    nw = m // WIN
    full = lambda a: pl.BlockSpec(a.shape, lambda i: (0,) * a.ndim)
    ins = [h, sb_t, sc_t, lw["w_in"], lw["wexp"], lw["bs_exp"], lw["veca"], lw["cbw"], lw["ccw"],
           lw["w_out"], lw["vecd"], lw["wr"]]
    out_shape = (
        jax.ShapeDtypeStruct((m, d), F32),
        jax.ShapeDtypeStruct((m, A_WIDTH), F32),
        jax.ShapeDtypeStruct(sb_t.shape, F32),
        jax.ShapeDtypeStruct(sc_t.shape, F32),
        jax.ShapeDtypeStruct((nw, WIN_ROWS, d), BF16),
        jax.ShapeDtypeStruct((nw, WIN, LANES), F32),
        jax.ShapeDtypeStruct((nw, SUBLANES, LANES), jnp.int32),
    )
    return pl.pallas_call(
        functools.partial(_mixer_sample_body, first, alpha, n_seq, n_t),
        grid=(1,),
        in_specs=[full(a) for a in ins],
        out_specs=tuple(full(o) for o in out_shape),
        out_shape=out_shape,
        scratch_shapes=[pltpu.VMEM((m, d), BF16)],
        compiler_params=pltpu.CompilerParams(
            dimension_semantics=("arbitrary",), vmem_limit_bytes=VMEM_LIMIT),
        name="mixer_sample",
    )(*ins)


def _experts_body(layer, tile_e_ref, fresh_ref, wslot_ref, next_e_ref, nvalid_ref, clist_ref, ntiles_ref,
                  xs_hbm, wg_hbm, wu_hbm, wd_hbm, ys_hbm,
                  lhs, obuf, sg, su, sd, wgb, wub, wdb, sem_in, sem_out, sem_w):
    nt = ntiles_ref[0]

    def copy_in(cid, slot, j):
        return pltpu.make_async_copy(xs_hbm.at[cid], lhs.at[slot, pl.ds(j * CH, CH)], sem_in.at[slot])

    def copy_out(cid, slot, j):
        return pltpu.make_async_copy(obuf.at[slot, pl.ds(j * CH, CH)], ys_hbm.at[cid], sem_out.at[slot])

    def weight_copies(e, ws):
        row = layer * N_EXPERTS + e
        return [pltpu.make_async_copy(src.at[row], dst.at[ws], sem_w.at[ws])
                for src, dst in ((wg_hbm, sg), (wu_hbm, su), (wd_hbm, sd))]

    def start_all(make, tt, slot):
        n = nvalid_ref[tt]
        base = tt * TILE_CHUNKS
        groups = n // DMA_UNROLL

        def group(g, carry):
            for k in range(DMA_UNROLL):
                j = g * DMA_UNROLL + k
                make(clist_ref[base + j], slot, j).start()
            return carry

        def single(j, carry):
            make(clist_ref[base + j], slot, j).start()
            return carry

        lax.fori_loop(0, groups, group, 0)
        lax.fori_loop(groups * DMA_UNROLL, n, single, 0)

    def wait_all(make, tt, slot):
        def body(j, carry):
            make(0, slot, j).wait()
            return carry
        lax.fori_loop(0, nvalid_ref[tt], body, 0)

    def start_full(make, tt, slot):
        for j in range(TILE_CHUNKS):
            make(clist_ref[tt * TILE_CHUNKS + j], slot, j).start()

    def wait_full(make, slot):
        for j in range(TILE_CHUNKS):
            make(0, slot, j).wait()

    for c in weight_copies(tile_e_ref[0], 0):
        c.start()
    for tt in range(GATHER_AHEAD):
        @pl.when(tt < nt)
        def _():
            start_full(copy_in, tt, tt % N_SLOTS)

    def tile(t, carry):
        slot = t % N_SLOTS
        t_gather = t + GATHER_AHEAD
        t_drain = t - N_SLOTS

        @pl.when(fresh_ref[t] == 1)
        def _():
            ws = wslot_ref[t]
            for c in weight_copies(0, ws):
                c.wait()
            wgb[...] = sg[ws].astype(BF16)
            wub[...] = su[ws].astype(BF16)
            wdb[...] = sd[ws].astype(BF16)

            @pl.when(next_e_ref[t] >= 0)
            def _():
                for c in weight_copies(next_e_ref[t], 1 - ws):
                    c.start()

        def ffn():
            x = lhs[slot]
            act = _silu(_dot(x, wgb[...])) * _dot(x, wub[...])
            obuf[slot] = _dot(act.astype(BF16), wdb[...]).astype(BF16)

        wait_full(copy_in, slot)

        drain_full = (t_drain >= 0) & (nvalid_ref[jnp.maximum(t_drain, 0)] == TILE_CHUNKS)

        @pl.when(drain_full)
        def _():
            wait_full(copy_out, slot)

        @pl.when((t_drain >= 0) & jnp.logical_not(drain_full))
        def _():
            wait_all(copy_out, t_drain, slot)

        gather_next = t_gather < nt
        scatter_full = (t >= 1) & (nvalid_ref[jnp.maximum(t - 1, 0)] == TILE_CHUNKS)

        @pl.when((t >= 1) & jnp.logical_not(gather_next & scatter_full))
        def _():
            start_all(copy_out, t - 1, (t - 1) % N_SLOTS)

        @pl.when(gather_next & scatter_full)
        def _():
            start_full(copy_in, t_gather, t_gather % N_SLOTS)
            start_full(copy_out, t - 1, (t - 1) % N_SLOTS)
            ffn()

        @pl.when(gather_next & jnp.logical_not(scatter_full))
        def _():
            start_full(copy_in, t_gather, t_gather % N_SLOTS)
            ffn()

        @pl.when(jnp.logical_not(gather_next))
        def _():
            ffn()

        @pl.when(t == nt - 1)
        def _():
            start_all(copy_out, t, slot)

        return carry

    lax.fori_loop(0, nt, tile, 0)
    for back in range(N_SLOTS, 0, -1):
        @pl.when(nt >= back)
        def _():
            wait_all(copy_out, nt - back, (nt - back) % N_SLOTS)


def _experts(xs, tables, wg, wu, wd, layer):
    n_win, _, d = xs.shape
    xc = xs.reshape(n_win * WIN_CHUNKS, CH, d)
    de = wg.shape[-1]
    anyspec = pl.BlockSpec(memory_space=pl.ANY)
    grid_spec = pltpu.PrefetchScalarGridSpec(
        num_scalar_prefetch=len(tables),
        grid=(1,),
        in_specs=[anyspec, anyspec, anyspec, anyspec],
        out_specs=anyspec,
        scratch_shapes=[
            pltpu.VMEM((N_SLOTS, TILE_ROWS, d), BF16),
            pltpu.VMEM((N_SLOTS, TILE_ROWS, d), BF16),
            pltpu.VMEM((2, d, de), F32),
            pltpu.VMEM((2, d, de), F32),
            pltpu.VMEM((2, de, d), F32),
            pltpu.VMEM((d, de), BF16),
            pltpu.VMEM((d, de), BF16),
            pltpu.VMEM((de, d), BF16),
            pltpu.SemaphoreType.DMA((N_SLOTS,)),
            pltpu.SemaphoreType.DMA((N_SLOTS,)),
            pltpu.SemaphoreType.DMA((2,)),
        ],
    )
    ys = pl.pallas_call(
        functools.partial(_experts_body, layer),
        grid_spec=grid_spec,
        out_shape=jax.ShapeDtypeStruct(xc.shape, xc.dtype),
        input_output_aliases={len(tables): 0},
        compiler_params=pltpu.CompilerParams(
            dimension_semantics=("arbitrary",), vmem_limit_bytes=VMEM_LIMIT),
        name="experts",
    )(*tables, xc, wg, wu, wd)
    return ys.reshape(xs.shape)


def _expert_tables(tab, t_max):
    i32 = jnp.int32
    nch = tab[:, 0, :N_EXPERTS]
    off = tab[:, 1, :N_EXPERTS]
    n_win = nch.shape[0]
    cum_incl = jnp.cumsum(nch, axis=0)
    cum_excl = cum_incl - nch
    ce = cum_incl[-1]
    te = (ce + TILE_CHUNKS - 1) // TILE_CHUNKS
    tile_end = jnp.cumsum(te)
    tile_start = tile_end - te
    nt = tile_end[-1]
    t_idx = jnp.arange(t_max, dtype=i32)
    tile_e = jnp.minimum(jnp.sum((tile_end[None, :] <= t_idx[:, None]).astype(i32), axis=1), N_EXPERTS - 1)
    sel_e = (tile_e[:, None] == jnp.arange(N_EXPERTS, dtype=i32)[None, :]).astype(i32)
    pick = lambda v: jnp.sum(sel_e * v[None, :], axis=1)
    q0 = (t_idx - pick(tile_start)) * TILE_CHUNKS
    nvalid = jnp.where(t_idx < nt, jnp.clip(pick(ce) - q0, 0, TILE_CHUNKS), 0).astype(i32)
    q = q0[:, None] + jnp.arange(TILE_CHUNKS, dtype=i32)[None, :]
    by_win = lambda m: jnp.sum(sel_e[:, :, None] * m.T[None, :, :], axis=1)
    cum_e = by_win(cum_incl)
    w = jnp.minimum(jnp.sum((cum_e[:, None, :] <= q[:, :, None]).astype(i32), axis=-1), n_win - 1)
    sel_w = (w[:, :, None] == jnp.arange(n_win, dtype=i32)[None, None, :]).astype(i32)
    shift = jnp.sum(sel_w * by_win(off - cum_excl)[:, None, :], axis=-1)
    cid = w * WIN_CHUNKS + shift + q
    valid = jnp.arange(TILE_CHUNKS, dtype=i32)[None, :] < nvalid[:, None]
    clist = jnp.where(valid, cid, cid[:, 0:1]).astype(i32).reshape(-1)
    prev_e = jnp.concatenate([jnp.full((1,), -1, i32), tile_e[:-1]])
    fresh = ((t_idx < nt) & (tile_e != prev_e)).astype(i32)
    wslot = (jnp.cumsum(fresh) - 1) % 2
    later_fresh = (t_idx[None, :] > t_idx[:, None]) & (fresh[None, :] == 1)
    nxt = jnp.min(jnp.where(later_fresh, t_idx[None, :], t_max), axis=1)
    next_e = jnp.where(nxt < t_max, jnp.sum((t_idx[None, :] == nxt[:, None]).astype(i32) * tile_e[None, :], axis=1), -1)
    return (tile_e.astype(i32), fresh, wslot.astype(i32), next_e.astype(i32), nvalid, clist,
            nt.astype(i32).reshape(1))


def _combine_body(alpha, h1_ref, info_ref, ys_ref, vecd_ref, out_ref):
    for wi in range(info_ref.shape[0]):
        rs = slice(wi * WIN, (wi + 1) * WIN)
        out_ref[rs, :] = _unsort_norm(h1_ref[rs, :], info_ref[wi], ys_ref[wi],
                                      vecd_ref[2:3, :], vecd_ref[3:4, :], alpha)


def _combine(h1, info, ys, vecd, alpha, wpb, block0):
    m, d = h1.shape
    win_map = lambda i: (block0 + i, 0, 0)
    return pl.pallas_call(
        functools.partial(_combine_body, alpha),
        grid=(m // (wpb * WIN),),
        in_specs=[
            pl.BlockSpec((wpb * WIN, d), lambda i: (i, 0)),
            pl.BlockSpec((wpb, WIN, LANES), win_map),
            pl.BlockSpec((wpb, WIN_ROWS, d), win_map),
            pl.BlockSpec(vecd.shape, lambda i: (0, 0)),
        ],
        out_specs=pl.BlockSpec((wpb * WIN, d), lambda i: (i, 0)),
        out_shape=jax.ShapeDtypeStruct((m, d), F32),
        compiler_params=pltpu.CompilerParams(
            dimension_semantics=("arbitrary",), vmem_limit_bytes=VMEM_LIMIT),
        name="combine",
    )(h1, info, ys, vecd)


def kernel(x_prompt, x_sample, state_conv_b, state_conv_c, ln_in_g, ln_in_b, w_in, w_s, b_s, ln_v_g, ln_v_b, conv_b_w, conv_b_bias, ln_conv_g, ln_conv_b, conv_c_w, w_out, ln1_g, ln1_b, w_router_group, w_router_expert, w_gate, w_up, w_down, ln2_g, ln2_b):
    depth = w_in.shape[0]
    nb, t, d = x_prompt.shape
    ns, nt_s, _ = x_sample.shape
    assert t % TM == 0 and (ns * nt_s) % WIN == 0 and TM % WIN == 0 and d % LANES == 0
    alpha = (2.0 * depth) ** 0.25
    n_win_p = nb * t // WIN
    n_win_s = ns * nt_s // WIN
    n_win = n_win_p + n_win_s
    assert n_win_p % n_win_s == 0 and n_win_s % (TM // WIN) == 0
    max_chunks = n_win * (2 * WIN // CH + N_EXPERTS)
    t_max = max_chunks // TILE_CHUNKS + N_EXPERTS

    hp = x_prompt
    hs = jnp.transpose(x_sample, (1, 0, 2)).reshape(nt_s * ns, d)
    sb_t = jnp.transpose(state_conv_b, (0, 2, 1, 3))
    sc_t = jnp.transpose(state_conv_c, (0, 2, 1, 3))
    cb_p, cc_p, cb_s, cc_s, v_s = [], [], [], [], []
    prev = ()
    for l in range(depth):
        wr = jnp.concatenate([w_router_group[l], w_router_expert[l].reshape(d, N_EXPERTS)], axis=1)
        wr = jnp.pad(wr, ((0, 0), (0, LANES - wr.shape[1])))
        wr_hi = wr.astype(BF16)
        wr = jnp.concatenate([wr_hi, (wr - wr_hi.astype(F32)).astype(BF16)], axis=1)
        zeros_a = jnp.zeros((A_WIDTH,), F32)
        zeros_d = jnp.zeros((d,), F32)
        lw = dict(
            w_in=w_in[l].astype(BF16),
            ws2=w_s[l].reshape(A_WIDTH // LANES, 2 * GMLP_CHUNK, GMLP_CHUNK),
            wexp=jnp.repeat(jnp.transpose(w_s[l][:, :nt_s, :nt_s], (1, 2, 0)), HEAD_DIM, axis=-1),
            bs_exp=jnp.repeat(b_s[l].T, HEAD_DIM, axis=-1),
            veca=jnp.stack([ln_v_g[l], ln_v_b[l], conv_b_bias[l], ln_conv_g[l], ln_conv_b[l],
                            zeros_a, zeros_a, zeros_a]),
            cbw=conv_b_w[l], ccw=conv_c_w[l],
            w_out=w_out[l].astype(BF16),
            vecd=jnp.stack([ln1_g[l], ln1_b[l], ln2_g[l], ln2_b[l], ln_in_g, ln_in_b, zeros_d, zeros_d]),
            wr=wr,
        )
        h1s, vs, nbs, ncs, xs_s, info_s, tab_s = _mixer_sample(hs, sb_t[l], sc_t[l], lw, l == 0, alpha)
        h1p, xs, info, tab, nbp, ncp = _mixer_prompt(hp, prev, lw, l == 0, alpha, xs_s, info_s, tab_s)
        ne = N_EXPERTS
        ys = _experts(xs, _expert_tables(tab, t_max), w_gate.reshape(depth * ne, d, -1),
                      w_up.reshape(depth * ne, d, -1), w_down.reshape(depth * ne, -1, d), l)
        hp, prev = h1p, (info, ys, lw["vecd"])
        hs = _combine(h1s, info, ys, lw["vecd"], alpha, n_win_s, n_win_p // n_win_s)
        cb_p.append(nbp)
        cc_p.append(ncp)
        cb_s.append(nbs)
        cc_s.append(ncs)
        v_s.append(vs)
    hp = _combine(hp.reshape(nb * t, d), *prev, alpha, TM // WIN, 0).reshape(nb, t, d)
    y_sample = jnp.transpose(hs.reshape(nt_s, ns, d), (1, 0, 2))
    untime = lambda xs_: jnp.transpose(jnp.stack(xs_), (0, 2, 1, 3))
    chunk_v = jnp.transpose(jnp.stack(v_s).reshape(depth, nt_s, ns, A_WIDTH), (0, 2, 1, 3))
    return (hp, y_sample, jnp.stack(cb_p), jnp.stack(cc_p), untime(cb_s), untime(cc_s), chunk_v)
```

```python
import functools
import math

import jax
import jax.numpy as jnp
from jax import lax
from jax.experimental import pallas as pl
from jax.experimental.pallas import tpu as pltpu

F32 = jnp.float32
BF16 = jnp.bfloat16

HEAD_DIM = 64
A_WIDTH = 384
B_WIDTH = 384
C_WIDTH = 256
GMLP_CHUNK = 128
CONV_B_TAPS = 31
CONV_C_TAPS = 3
N_GROUPS = 4
EXPERTS_PER_GROUP = 8
N_EXPERTS = N_GROUPS * EXPERTS_PER_GROUP
LN_EPS = 1e-5
INV_SQRT2 = 1.0 / math.sqrt(2.0)

LANES = 128
SUBLANES = 8
WIN = 256
CH = 16
WIN_ROWS = -(-(2 * WIN + N_EXPERTS * (CH - 1)) // WIN) * WIN
WIN_CHUNKS = WIN_ROWS // CH
TILE_CHUNKS = 32
TILE_ROWS = TILE_CHUNKS * CH
TM = 512
CONV_RB = 32
DMA_UNROLL = 4
N_SLOTS = 3
GATHER_AHEAD = N_SLOTS - 1
VMEM_LIMIT = 56 * 1024 * 1024


def _ln(x, g, b):
    mu = jnp.mean(x, axis=-1, keepdims=True)
    xc = x - mu
    var = jnp.mean(xc * xc, axis=-1, keepdims=True)
    return xc * lax.rsqrt(var + LN_EPS) * g + b


def _gelu(x):
    return 0.5 * x * (1.0 + lax.erf(x * INV_SQRT2))


def _silu(x):
    return x * jax.nn.sigmoid(x)


def _dot(a, b):
    return jnp.dot(a, b, preferred_element_type=F32)


def _iota(shape, dim):
    return lax.broadcasted_iota(jnp.int32, shape, dim)


def _layer_spec(arr, layer):
    rest = arr.shape[1:]
    return pl.BlockSpec((None,) + rest, lambda *_: (layer,) + (0,) * len(rest))


def _route_window(logits, hb):
    w = logits.shape[0]
    lane = _iota((w, LANES), 1)
    neg = jnp.float32(-jnp.inf)
    gmask = lane < N_GROUPS
    gl = jnp.where(gmask, logits, neg)
    gmax = jnp.max(gl, axis=-1, keepdims=True)
    gsel = jnp.min(jnp.where(gl == gmax, lane, LANES), axis=-1, keepdims=True)
    den = jnp.sum(jnp.where(gmask, jnp.exp(gl - gmax), 0.0), axis=-1, keepdims=True)
    gw = 1.0 / den
    lo = N_GROUPS + EXPERTS_PER_GROUP * gsel
    el = jnp.where((lane >= lo) & (lane < lo + EXPERTS_PER_GROUP), logits, neg)
    v1 = jnp.max(el, axis=-1, keepdims=True)
    i1 = jnp.min(jnp.where(el == v1, lane, LANES), axis=-1, keepdims=True)
    el2 = jnp.where(lane == i1, neg, el)
    v2 = jnp.max(el2, axis=-1, keepdims=True)
    i2 = jnp.min(jnp.where(el2 == v2, lane, LANES), axis=-1, keepdims=True)
    e2x = jnp.exp(v2 - v1)
    w1 = gw / (1.0 + e2x)
    w2 = gw * e2x / (1.0 + e2x)
    e1 = i1 - N_GROUPS
    e2 = i2 - N_GROUPS

    oh = (lane == e1) | (lane == e2)
    ohf = jnp.where(oh, 1.0, 0.0)
    lstrict = jnp.where(_iota((w, w), 0) > _iota((w, w), 1), 1.0, 0.0).astype(BF16)
    rank = _dot(lstrict, ohf.astype(BF16))
    cnt = jnp.sum(ohf, axis=0, keepdims=True)
    nch = jnp.floor((cnt + (CH - 1.0)) * (1.0 / CH))
    upper = jnp.where(_iota((LANES, LANES), 0) < _iota((LANES, LANES), 1), 1.0, 0.0).astype(BF16)
    off = _dot(jnp.broadcast_to(nch, (SUBLANES, LANES)).astype(BF16), upper)[0:1]
    dest = off * CH + rank
    r1 = jnp.sum(jnp.where(lane == e1, dest, 0.0), axis=-1, keepdims=True)
    r2 = jnp.sum(jnp.where(lane == e2, dest, 0.0), axis=-1, keepdims=True)
    info = jnp.where(lane == 0, r1, jnp.where(lane == 1, r2,
                     jnp.where(lane == 2, w1, jnp.where(lane == 3, w2, 0.0))))
    info_t = info.T
    rows = _iota((WIN_ROWS, w), 0).astype(F32)
    perm = jnp.where((rows == info_t[0:1, :]) | (rows == info_t[1:2, :]), 1.0, 0.0).astype(BF16)
    xs = _dot(perm, hb).astype(BF16)
    sub = _iota((SUBLANES, LANES), 0)
    tab = jnp.where(sub == 0, nch, jnp.where(sub == 1, off, 0.0)).astype(jnp.int32)
    return xs, info, tab


def _unsort_norm(h1, info, ys, g, b, alpha):
    col = _iota((WIN, WIN_ROWS), 1).astype(F32)
    pt = jnp.where(col == info[:, 0:1], info[:, 2:3],
                   jnp.where(col == info[:, 1:2], info[:, 3:4], 0.0)).astype(BF16)
    return _ln(alpha * h1 + _dot(pt, ys), g, b)


def _post_mix(h, mixin, w_out_ref, vecd_ref, wr_ref, alpha, h1_ref, xs_ref, info_ref, tab_ref):
    mix_out = _dot(mixin, w_out_ref[...])
    h1 = _ln(alpha * h + mix_out, vecd_ref[0:1, :], vecd_ref[1:2, :])
    h1_ref[...] = h1
    hb = h1.astype(BF16)
    h_lo = (h1 - hb.astype(F32)).astype(BF16)
    l2 = _dot(hb, wr_ref[...])
    logits = l2[:, 0:LANES] + (l2[:, LANES:] + _dot(h_lo, wr_ref[:, 0:LANES]))
    for wi in range(h.shape[0] // WIN):
        rs = slice(wi * WIN, (wi + 1) * WIN)
        xs, info, tab = _route_window(logits[rs], hb[rs])
        xs_ref[wi] = xs
        info_ref[wi] = info
        tab_ref[wi] = tab


def _mixer_prompt_body(first, alpha, nj, n_tiles, *refs):
    n_prev = 0 if first else 3
    h_ref, prev = refs[0], refs[1:1 + n_prev]
    (w_in_ref, ws_ref, bs_ref, veca_ref, cbw_ref, ccw_ref, w_out_ref, vecd_ref, wr_ref,
     xs_s_ref, info_s_ref, tab_s_ref,
     h1_ref, xs_ref, info_ref, tab_ref, nsb_ref, nsc_ref,
     xsh, cbuf, mixin, wb, cbs) = refs[1 + n_prev:]
    i = pl.program_id(0)
    j = i % nj

    @pl.when((i < n_tiles) & (j == 0))
    def _():
        xsh[0, 0:32, :] = jnp.zeros((32, B_WIDTH), F32)
        cbuf[0:8, :] = jnp.zeros((8, C_WIDTH), F32)
        for k in range(CONV_B_TAPS):
            wb[k] = jnp.broadcast_to(cbw_ref[k:k + 1, :], (SUBLANES, B_WIDTH))

    @pl.when(i < n_tiles)
    def _():
        new_b, new_c = _mixer_prompt_tile(first, alpha, h_ref, prev, w_in_ref, ws_ref, bs_ref, veca_ref,
                                          ccw_ref, w_out_ref, vecd_ref, wr_ref,
                                          h1_ref, xs_ref, info_ref, tab_ref, xsh, cbuf, mixin, wb, cbs)

        @pl.when(j == nj - 1)
        def _():
            nsb_ref[0] = new_b
            nsc_ref[0] = new_c

    @pl.when(i >= n_tiles)
    def _():
        xs_ref[...] = xs_s_ref[...]
        info_ref[...] = info_s_ref[...]
        tab_ref[...] = tab_s_ref[...]


def _mixer_prompt_tile(first, alpha, h_ref, prev, w_in_ref, ws_ref, bs_ref, veca_ref, ccw_ref,
                       w_out_ref, vecd_ref, wr_ref, h1_ref, xs_ref, info_ref, tab_ref,
                       xsh, cbuf, mixin, wb, cbs):
    past_b = CONV_B_TAPS - 1
    past_c = CONV_C_TAPS - 1
    b0 = 32 - past_b
    c0 = 8 - past_c
    n_half = TM // WIN
    oa, ob = 2 * A_WIDTH, 2 * A_WIDTH + 2 * B_WIDTH

    tri = (_iota((2 * GMLP_CHUNK, GMLP_CHUNK), 0) % GMLP_CHUNK) >= _iota((2 * GMLP_CHUNK, GMLP_CHUNK), 1)
    wms = [jnp.where(tri, ws_ref[p], 0.0).astype(BF16) for p in range(A_WIDTH // LANES)]
    lane = _iota((GMLP_CHUNK, LANES), 1)

    hs, bgs = [], []
    for hf in range(n_half):
        r0 = hf * WIN
        h = h_ref[0, r0:r0 + WIN, :]
        if first:
            h = _ln(h, vecd_ref[4:5, :], vecd_ref[5:6, :])
        else:
            info_p, ys_p, vecd_p = prev
            h = _unsort_norm(h, info_p[hf], ys_p[hf], vecd_p[2:3, :], vecd_p[3:4, :], alpha)
        hs.append(h)
        hb = h.astype(BF16)
        za = _dot(hb, w_in_ref[:, 0:oa])
        zb = _dot(hb, w_in_ref[:, oa:ob])
        zc = _dot(hb, w_in_ref[:, ob:ob + 3 * C_WIDTH])

        ga = _gelu(za)
        u = ga[:, 0:A_WIDTH]
        vb = _ln(ga[:, A_WIDTH:], veca_ref[0:1, :], veca_ref[1:2, :]).astype(BF16)
        for c in range(WIN // GMLP_CHUNK):
            rs = slice(c * GMLP_CHUNK, (c + 1) * GMLP_CHUNK)
            parts = []
            for p in range(A_WIDTH // LANES):
                ab = _dot(wms[p], vb[rs, p * LANES:(p + 1) * LANES])
                parts.append(jnp.where(lane < HEAD_DIM, ab[:GMLP_CHUNK], ab[GMLP_CHUNK:]))
            mix = jnp.concatenate(parts, axis=1) + bs_ref[...]
            mixin[r0 + c * GMLP_CHUNK:r0 + (c + 1) * GMLP_CHUNK, 0:A_WIDTH] = (u[rs] * mix).astype(BF16)

        xsh[0, 32 + r0:32 + r0 + WIN, :] = zb[:, 0:B_WIDTH] * jax.nn.sigmoid(zb[:, B_WIDTH:])
        lo = 0 if hf == 0 else r0 + 24
        for r in range(1, SUBLANES):
            xsh[r, lo:r0 + WIN + 24, :] = xsh[0, lo + r:r0 + WIN + 24 + r, :]

        bgs.append(zc[:, 0:C_WIDTH])
        cbuf[8 + r0:8 + r0 + WIN, :] = zc[:, C_WIDTH:2 * C_WIDTH] * zc[:, 2 * C_WIDTH:]

    n_sub = CONV_RB // SUBLANES
    for hf in range(n_half):
        r0 = hf * WIN
        for rb in range(WIN // CONV_RB):
            base = r0 + rb * CONV_RB
            accs = [jnp.broadcast_to(veca_ref[2:3, :], (SUBLANES, B_WIDTH))] * n_sub
            for k in range(CONV_B_TAPS):
                s = k + b0
                w8 = wb[k]
                for a in range(n_sub):
                    row = base + (s // SUBLANES + a) * SUBLANES
                    accs[a] = accs[a] + w8 * xsh[s % SUBLANES, row:row + SUBLANES, :]
            for a in range(n_sub):
                cbs[base + a * SUBLANES:base + (a + 1) * SUBLANES, :] = accs[a]
        yb = _silu(_ln(cbs[r0:r0 + WIN, :], veca_ref[3:4, :], veca_ref[4:5, :]))
        mixin[r0:r0 + WIN, A_WIDTH:A_WIDTH + B_WIDTH] = yb.astype(BF16)

        cc = ccw_ref[0:1, :] * cbuf[c0 + r0:c0 + r0 + WIN, :]
        for k in range(1, CONV_C_TAPS):
            cc = cc + ccw_ref[k:k + 1, :] * cbuf[c0 + r0 + k:c0 + r0 + k + WIN, :]
        mixin[r0:r0 + WIN, A_WIDTH + B_WIDTH:] = (bgs[hf] * cc).astype(BF16)

        _post_mix(hs[hf], mixin[r0:r0 + WIN, :], w_out_ref, vecd_ref, wr_ref, alpha,
                  h1_ref.at[0, r0:r0 + WIN], xs_ref.at[hf:hf + 1], info_ref.at[hf:hf + 1],
                  tab_ref.at[hf:hf + 1])

    new_b = xsh[0, TM + b0:TM + 32, :]
    xsh[0, b0:32, :] = new_b
    new_c = cbuf[TM + c0:TM + 8, :]
    cbuf[c0:8, :] = new_c
    return new_b, new_c


def _mixer_prompt(h, prev, lw, layer, alpha, xs_s, info_s, tab_s):
    first = layer == 0
    nb, t, d = h.shape
    nj = t // TM
    wpt = TM // WIN
    n_tiles = nb * nj
    n_extra = xs_s.shape[0] // wpt
    n_win_total = (n_tiles + n_extra) * wpt
    tile = lambda i: jnp.minimum(i, n_tiles - 1)
    h_map = lambda i: (tile(i) // nj, tile(i) % nj, 0)
    state_map = lambda i: (tile(i) // nj, 0, 0)
    extra_map = lambda i: (jnp.maximum(i - n_tiles, 0), 0, 0)
    prev_win_map = lambda i: (tile(i), 0, 0)
    prev_specs = [] if first else [pl.BlockSpec((wpt, WIN, LANES), prev_win_map),
                                   pl.BlockSpec((wpt, WIN_ROWS, d), prev_win_map),
                                   _layer_spec(lw["vecd"], layer - 1)]
    prev = prev if first else (*prev, lw["vecd"])
    weights = [lw[k] for k in ("w_in", "ws2", "bs_exp", "veca", "cbw", "ccw", "w_out", "vecd", "wr")]
    out_shape = (
        jax.ShapeDtypeStruct((nb, t, d), F32),
        jax.ShapeDtypeStruct((n_win_total, WIN_ROWS, d), BF16),
        jax.ShapeDtypeStruct((n_win_total, WIN, LANES), F32),
        jax.ShapeDtypeStruct((n_win_total, SUBLANES, LANES), jnp.int32),
        jax.ShapeDtypeStruct((nb, CONV_B_TAPS - 1, B_WIDTH), F32),
        jax.ShapeDtypeStruct((nb, CONV_C_TAPS - 1, C_WIDTH), F32),
    )
    win_map = lambda i: (i, 0, 0)
    return pl.pallas_call(
        functools.partial(_mixer_prompt_body, first, alpha, nj, n_tiles),
        grid=(n_tiles + n_extra,),
        in_specs=[pl.BlockSpec((1, TM, d), h_map)] + prev_specs + [_layer_spec(a, layer) for a in weights] + [
            pl.BlockSpec((wpt, WIN_ROWS, d), extra_map),
            pl.BlockSpec((wpt, WIN, LANES), extra_map),
            pl.BlockSpec((wpt, SUBLANES, LANES), extra_map),
        ],
        out_specs=(
            pl.BlockSpec((1, TM, d), h_map),
            pl.BlockSpec((wpt, WIN_ROWS, d), win_map),
            pl.BlockSpec((wpt, WIN, LANES), win_map),
            pl.BlockSpec((wpt, SUBLANES, LANES), win_map),
            pl.BlockSpec((1, CONV_B_TAPS - 1, B_WIDTH), state_map),
            pl.BlockSpec((1, CONV_C_TAPS - 1, C_WIDTH), state_map),
        ),
        out_shape=out_shape,
        scratch_shapes=[
            pltpu.VMEM((SUBLANES, TM + 32, B_WIDTH), F32),
            pltpu.VMEM((TM + 8, C_WIDTH), F32),
            pltpu.VMEM((TM, d), BF16),
            pltpu.VMEM((CONV_B_TAPS, SUBLANES, B_WIDTH), F32),
            pltpu.VMEM((TM, B_WIDTH), F32),
        ],
        compiler_params=pltpu.CompilerParams(
            dimension_semantics=("arbitrary",), vmem_limit_bytes=VMEM_LIMIT),
        name="mixer_prompt",
    )(h, *prev, *weights, xs_s, info_s, tab_s)


def _mixer_sample_body(first, alpha, n_seq, n_t,
                       h_ref, sb_ref, sc_ref, w_in_ref, wexp_ref, bs_ref, veca_ref, cbw_ref, ccw_ref,
                       w_out_ref, vecd_ref, wr_ref,
                       h1_ref, v_ref, nsb_ref, nsc_ref, xs_ref, info_ref, tab_ref,
                       mixin):
    h = h_ref[...]
    if first:
        h = _ln(h, vecd_ref[4:5, :], vecd_ref[5:6, :])
    hb = h.astype(BF16)
    rows = lambda t: slice(t * n_seq, (t + 1) * n_seq)

    oa, ob = 2 * A_WIDTH, 2 * A_WIDTH + 2 * B_WIDTH
    ga = _gelu(_dot(hb, w_in_ref[:, 0:oa]))
    zb = _dot(hb, w_in_ref[:, oa:ob])
    zc = _dot(hb, w_in_ref[:, ob:ob + 3 * C_WIDTH])

    u = ga[:, 0:A_WIDTH]
    v = _ln(ga[:, A_WIDTH:], veca_ref[0:1, :], veca_ref[1:2, :])
    v_ref[...] = v
    for t in range(n_t):
        mix = jnp.broadcast_to(bs_ref[t:t + 1, :], (n_seq, A_WIDTH))
        for s in range(t + 1):
            mix = mix + wexp_ref[t, s:s + 1, :] * v[rows(s)]
        mixin[rows(t), 0:A_WIDTH] = (u[rows(t)] * mix).astype(BF16)

    glu = zb[:, 0:B_WIDTH] * jax.nn.sigmoid(zb[:, B_WIDTH:])
    past_b = CONV_B_TAPS - 1
    xp = lambda m: sb_ref[m] if m < past_b else glu[rows(m - past_b)]
    for t in range(n_t):
        acc = jnp.broadcast_to(veca_ref[2:3, :], (n_seq, B_WIDTH))
        for k in range(CONV_B_TAPS):
            acc = acc + cbw_ref[k:k + 1, :] * xp(t + k)
        yb = _silu(_ln(acc, veca_ref[3:4, :], veca_ref[4:5, :]))
        mixin[rows(t), A_WIDTH:A_WIDTH + B_WIDTH] = yb.astype(BF16)
    for r in range(past_b):
        nsb_ref[r] = xp(r + n_t)

    bg = zc[:, 0:C_WIDTH]
    xc = zc[:, C_WIDTH:2 * C_WIDTH] * zc[:, 2 * C_WIDTH:]
    past_c = CONV_C_TAPS - 1
    xq = lambda m: sc_ref[m] if m < past_c else xc[rows(m - past_c)]
    for t in range(n_t):
        cc = ccw_ref[0:1, :] * xq(t)
        for k in range(1, CONV_C_TAPS):
            cc = cc + ccw_ref[k:k + 1, :] * xq(t + k)
        mixin[rows(t), A_WIDTH + B_WIDTH:] = (bg[rows(t)] * cc).astype(BF16)
    for r in range(past_c):
        nsc_ref[r] = xq(r + n_t)

    _post_mix(h, mixin[...], w_out_ref, vecd_ref, wr_ref, alpha, h1_ref, xs_ref, info_ref, tab_ref)


def _mixer_sample(h, sb_t, sc_t, lw, layer, alpha):
    m, d = h.shape
    n_seq = sb_t.shape[2]
    n_t = m // n_seq
    nw = m // WIN
    full = lambda a: pl.BlockSpec(a.shape, lambda i: (0,) * a.ndim)
    stacked = [sb_t, sc_t] + [lw[k] for k in ("w_in", "wexp", "bs_exp", "veca", "cbw", "ccw", "w_out", "vecd", "wr")]
    out_shape = (
        jax.ShapeDtypeStruct((m, d), F32),
        jax.ShapeDtypeStruct((m, A_WIDTH), F32),
        jax.ShapeDtypeStruct(sb_t.shape[1:], F32),
        jax.ShapeDtypeStruct(sc_t.shape[1:], F32),
        jax.ShapeDtypeStruct((nw, WIN_ROWS, d), BF16),
        jax.ShapeDtypeStruct((nw, WIN, LANES), F32),
        jax.ShapeDtypeStruct((nw, SUBLANES, LANES), jnp.int32),
    )
    return pl.pallas_call(
        functools.partial(_mixer_sample_body, layer == 0, alpha, n_seq, n_t),
        grid=(1,),
        in_specs=[full(h)] + [_layer_spec(a, layer) for a in stacked],
        out_specs=tuple(full(o) for o in out_shape),
        out_shape=out_shape,
        scratch_shapes=[pltpu.VMEM((m, d), BF16)],
        compiler_params=pltpu.CompilerParams(
            dimension_semantics=("arbitrary",), vmem_limit_bytes=VMEM_LIMIT),
        name="mixer_sample",
    )(h, *stacked)


def _experts_body(layer, tile_e_ref, fresh_ref, wslot_ref, next_e_ref, nvalid_ref, clist_ref, ntiles_ref,
                  xs_hbm, wg_hbm, wu_hbm, wd_hbm, ys_hbm,
                  lhs, obuf, sg, su, sd, wgb, wub, wdb, sem_in, sem_out, sem_w):
    nt = ntiles_ref[0]

    def copy_in(cid, slot, j):
        return pltpu.make_async_copy(xs_hbm.at[cid], lhs.at[slot, pl.ds(j * CH, CH)], sem_in.at[slot])

    def copy_out(cid, slot, j):
        return pltpu.make_async_copy(obuf.at[slot, pl.ds(j * CH, CH)], ys_hbm.at[cid], sem_out.at[slot])

    def weight_copies(e, ws):
        row = layer * N_EXPERTS + e
        return [pltpu.make_async_copy(src.at[row], dst.at[ws], sem_w.at[ws])
                for src, dst in ((wg_hbm, sg), (wu_hbm, su), (wd_hbm, sd))]

    def start_all(make, tt, slot):
        n = nvalid_ref[tt]
        base = tt * TILE_CHUNKS
        groups = n // DMA_UNROLL

        def group(g, carry):
            for k in range(DMA_UNROLL):
                j = g * DMA_UNROLL + k
                make(clist_ref[base + j], slot, j).start()
            return carry

        def single(j, carry):
            make(clist_ref[base + j], slot, j).start()
            return carry

        lax.fori_loop(0, groups, group, 0)
        lax.fori_loop(groups * DMA_UNROLL, n, single, 0)

    def wait_all(make, tt, slot):
        def body(j, carry):
            make(0, slot, j).wait()
            return carry
        lax.fori_loop(0, nvalid_ref[tt], body, 0)

    def start_full(make, tt, slot):
        for j in range(TILE_CHUNKS):
            make(clist_ref[tt * TILE_CHUNKS + j], slot, j).start()

    def wait_full(make, slot):
        for j in range(TILE_CHUNKS):
            make(0, slot, j).wait()

    for c in weight_copies(tile_e_ref[0], 0):
        c.start()
    for tt in range(GATHER_AHEAD):
        @pl.when(tt < nt)
        def _():
            start_full(copy_in, tt, tt % N_SLOTS)

    def tile(t, carry):
        slot = t % N_SLOTS
        t_gather = t + GATHER_AHEAD
        t_drain = t - N_SLOTS

        @pl.when(fresh_ref[t] == 1)
        def _():
            ws = wslot_ref[t]
            for c in weight_copies(0, ws):
                c.wait()
            wgb[...] = sg[ws].astype(BF16)
            wub[...] = su[ws].astype(BF16)
            wdb[...] = sd[ws].astype(BF16)

            @pl.when(next_e_ref[t] >= 0)
            def _():
                for c in weight_copies(next_e_ref[t], 1 - ws):
                    c.start()

        def ffn():
            x = lhs[slot]
            act = _silu(_dot(x, wgb[...])) * _dot(x, wub[...])
            obuf[slot] = _dot(act.astype(BF16), wdb[...]).astype(BF16)

        wait_full(copy_in, slot)

        drain_full = (t_drain >= 0) & (nvalid_ref[jnp.maximum(t_drain, 0)] == TILE_CHUNKS)

        @pl.when(drain_full)
        def _():
            wait_full(copy_out, slot)

        @pl.when((t_drain >= 0) & jnp.logical_not(drain_full))
        def _():
            wait_all(copy_out, t_drain, slot)

        gather_next = t_gather < nt
        scatter_full = (t >= 1) & (nvalid_ref[jnp.maximum(t - 1, 0)] == TILE_CHUNKS)

        @pl.when((t >= 1) & jnp.logical_not(gather_next & scatter_full))
        def _():
            start_all(copy_out, t - 1, (t - 1) % N_SLOTS)

        @pl.when(gather_next & scatter_full)
        def _():
            start_full(copy_in, t_gather, t_gather % N_SLOTS)
            start_full(copy_out, t - 1, (t - 1) % N_SLOTS)
            ffn()

        @pl.when(gather_next & jnp.logical_not(scatter_full))
        def _():
            start_full(copy_in, t_gather, t_gather % N_SLOTS)
            ffn()

        @pl.when(jnp.logical_not(gather_next))
        def _():
            ffn()

        @pl.when(t == nt - 1)
        def _():
            start_all(copy_out, t, slot)

        return carry

    lax.fori_loop(0, nt, tile, 0)
    for back in range(N_SLOTS, 0, -1):
        @pl.when(nt >= back)
        def _():
            wait_all(copy_out, nt - back, (nt - back) % N_SLOTS)


def _experts(xs, tables, wg, wu, wd, layer):
    n_win, _, d = xs.shape
    xc = xs.reshape(n_win * WIN_CHUNKS, CH, d)
    de = wg.shape[-1]
    anyspec = pl.BlockSpec(memory_space=pl.ANY)
    grid_spec = pltpu.PrefetchScalarGridSpec(
        num_scalar_prefetch=len(tables),
        grid=(1,),
        in_specs=[anyspec, anyspec, anyspec, anyspec],
        out_specs=anyspec,
        scratch_shapes=[
            pltpu.VMEM((N_SLOTS, TILE_ROWS, d), BF16),
            pltpu.VMEM((N_SLOTS, TILE_ROWS, d), BF16),
            pltpu.VMEM((2, d, de), F32),
            pltpu.VMEM((2, d, de), F32),
            pltpu.VMEM((2, de, d), F32),
            pltpu.VMEM((d, de), BF16),
            pltpu.VMEM((d, de), BF16),
            pltpu.VMEM((de, d), BF16),
            pltpu.SemaphoreType.DMA((N_SLOTS,)),
            pltpu.SemaphoreType.DMA((N_SLOTS,)),
            pltpu.SemaphoreType.DMA((2,)),
        ],
    )
    ys = pl.pallas_call(
        functools.partial(_experts_body, layer),
        grid_spec=grid_spec,
        out_shape=jax.ShapeDtypeStruct(xc.shape, xc.dtype),
        input_output_aliases={len(tables): 0},
        compiler_params=pltpu.CompilerParams(
            dimension_semantics=("arbitrary",), vmem_limit_bytes=VMEM_LIMIT),
        name="experts",
    )(*tables, xc, wg, wu, wd)
    return ys.reshape(xs.shape)


def _expert_tables(tab, t_max):
    i32 = jnp.int32
    nch = tab[:, 0, :N_EXPERTS]
    off = tab[:, 1, :N_EXPERTS]
    n_win = nch.shape[0]
    cum_incl = jnp.cumsum(nch, axis=0)
    cum_excl = cum_incl - nch
    ce = cum_incl[-1]
    te = (ce + TILE_CHUNKS - 1) // TILE_CHUNKS
    tile_end = jnp.cumsum(te)
    tile_start = tile_end - te
    nt = tile_end[-1]
    t_idx = jnp.arange(t_max, dtype=i32)
    tile_e = jnp.minimum(jnp.sum((tile_end[None, :] <= t_idx[:, None]).astype(i32), axis=1), N_EXPERTS - 1)
    sel_e = (tile_e[:, None] == jnp.arange(N_EXPERTS, dtype=i32)[None, :]).astype(i32)
    pick = lambda v: jnp.sum(sel_e * v[None, :], axis=1)
    q0 = (t_idx - pick(tile_start)) * TILE_CHUNKS
    nvalid = jnp.where(t_idx < nt, jnp.clip(pick(ce) - q0, 0, TILE_CHUNKS), 0).astype(i32)
    q = q0[:, None] + jnp.arange(TILE_CHUNKS, dtype=i32)[None, :]
    by_win = lambda m: jnp.sum(sel_e[:, :, None] * m.T[None, :, :], axis=1)
    cum_e = by_win(cum_incl)
    w = jnp.minimum(jnp.sum((cum_e[:, None, :] <= q[:, :, None]).astype(i32), axis=-1), n_win - 1)
    sel_w = (w[:, :, None] == jnp.arange(n_win, dtype=i32)[None, None, :]).astype(i32)
    shift = jnp.sum(sel_w * by_win(off - cum_excl)[:, None, :], axis=-1)
    cid = w * WIN_CHUNKS + shift + q
    valid = jnp.arange(TILE_CHUNKS, dtype=i32)[None, :] < nvalid[:, None]
    clist = jnp.where(valid, cid, cid[:, 0:1]).astype(i32).reshape(-1)
    prev_e = jnp.concatenate([jnp.full((1,), -1, i32), tile_e[:-1]])
    fresh = ((t_idx < nt) & (tile_e != prev_e)).astype(i32)
    wslot = (jnp.cumsum(fresh) - 1) % 2
    later_fresh = (t_idx[None, :] > t_idx[:, None]) & (fresh[None, :] == 1)
    nxt = jnp.min(jnp.where(later_fresh, t_idx[None, :], t_max), axis=1)
    next_e = jnp.where(nxt < t_max, jnp.sum((t_idx[None, :] == nxt[:, None]).astype(i32) * tile_e[None, :], axis=1), -1)
    return (tile_e.astype(i32), fresh, wslot.astype(i32), next_e.astype(i32), nvalid, clist,
            nt.astype(i32).reshape(1))


def _combine_body(alpha, h1_ref, info_ref, ys_ref, vecd_ref, out_ref):
    for wi in range(info_ref.shape[0]):
        rs = slice(wi * WIN, (wi + 1) * WIN)
        out_ref[rs, :] = _unsort_norm(h1_ref[rs, :], info_ref[wi], ys_ref[wi],
                                      vecd_ref[2:3, :], vecd_ref[3:4, :], alpha)


def _combine(h1, info, ys, vecd, layer, alpha, wpb, block0):
    m, d = h1.shape
    win_map = lambda i: (block0 + i, 0, 0)
    return pl.pallas_call(
        functools.partial(_combine_body, alpha),
        grid=(m // (wpb * WIN),),
        in_specs=[
            pl.BlockSpec((wpb * WIN, d), lambda i: (i, 0)),
            pl.BlockSpec((wpb, WIN, LANES), win_map),
            pl.BlockSpec((wpb, WIN_ROWS, d), win_map),
            _layer_spec(vecd, layer),
        ],
        out_specs=pl.BlockSpec((wpb * WIN, d), lambda i: (i, 0)),
        out_shape=jax.ShapeDtypeStruct((m, d), F32),
        compiler_params=pltpu.CompilerParams(
            dimension_semantics=("arbitrary",), vmem_limit_bytes=VMEM_LIMIT),
        name="combine",
    )(h1, info, ys, vecd)


def kernel(x_prompt, x_sample, state_conv_b, state_conv_c, ln_in_g, ln_in_b, w_in, w_s, b_s, ln_v_g, ln_v_b, conv_b_w, conv_b_bias, ln_conv_g, ln_conv_b, conv_c_w, w_out, ln1_g, ln1_b, w_router_group, w_router_expert, w_gate, w_up, w_down, ln2_g, ln2_b):
    depth = w_in.shape[0]
    nb, t, d = x_prompt.shape
    ns, nt_s, _ = x_sample.shape
    assert t % TM == 0 and (ns * nt_s) % WIN == 0 and TM % WIN == 0 and d % LANES == 0
    alpha = (2.0 * depth) ** 0.25
    n_win_p = nb * t // WIN
    n_win_s = ns * nt_s // WIN
    n_win = n_win_p + n_win_s
    assert n_win_p % n_win_s == 0 and n_win_s % (TM // WIN) == 0
    max_chunks = n_win * (2 * WIN // CH + N_EXPERTS)
    t_max = max_chunks // TILE_CHUNKS + N_EXPERTS

    wr = jnp.concatenate([w_router_group, w_router_expert.reshape(depth, d, N_EXPERTS)], axis=2)
    wr = jnp.pad(wr, ((0, 0), (0, 0), (0, LANES - wr.shape[2])))
    wr_hi = wr.astype(BF16)
    zeros_a = jnp.zeros((depth, A_WIDTH), F32)
    zeros_d = jnp.zeros((depth, d), F32)
    over_layers = lambda v: jnp.broadcast_to(v, (depth,) + v.shape)
    lw = dict(
        w_in=w_in.astype(BF16),
        ws2=w_s.reshape(depth, A_WIDTH // LANES, 2 * GMLP_CHUNK, GMLP_CHUNK),
        wexp=jnp.repeat(jnp.transpose(w_s[:, :, :nt_s, :nt_s], (0, 2, 3, 1)), HEAD_DIM, axis=-1),
        bs_exp=jnp.repeat(jnp.transpose(b_s, (0, 2, 1)), HEAD_DIM, axis=-1),
        veca=jnp.stack([ln_v_g, ln_v_b, conv_b_bias, ln_conv_g, ln_conv_b, zeros_a, zeros_a, zeros_a], axis=1),
        cbw=conv_b_w, ccw=conv_c_w,
        w_out=w_out.astype(BF16),
        vecd=jnp.stack([ln1_g, ln1_b, ln2_g, ln2_b, over_layers(ln_in_g), over_layers(ln_in_b),
                        zeros_d, zeros_d], axis=1),
        wr=jnp.concatenate([wr_hi, (wr - wr_hi.astype(F32)).astype(BF16)], axis=2),
    )
    ne = N_EXPERTS
    wg, wu, wd = (w_gate.reshape(depth * ne, d, -1), w_up.reshape(depth * ne, d, -1),
                  w_down.reshape(depth * ne, -1, d))

    hp = x_prompt
    hs = jnp.transpose(x_sample, (1, 0, 2)).reshape(nt_s * ns, d)
    sb_t = jnp.transpose(state_conv_b, (0, 2, 1, 3))
    sc_t = jnp.transpose(state_conv_c, (0, 2, 1, 3))
    cb_p, cc_p, cb_s, cc_s, v_s = [], [], [], [], []
    prev = ()
    for l in range(depth):
        h1s, vs, nbs, ncs, xs_s, info_s, tab_s = _mixer_sample(hs, sb_t, sc_t, lw, l, alpha)
        h1p, xs, info, tab, nbp, ncp = _mixer_prompt(hp, prev, lw, l, alpha, xs_s, info_s, tab_s)
        ys = _experts(xs, _expert_tables(tab, t_max), wg, wu, wd, l)
        hp, prev = h1p, (info, ys)
        hs = _combine(h1s, info, ys, lw["vecd"], l, alpha, n_win_s, n_win_p // n_win_s)
        cb_p.append(nbp)
        cc_p.append(ncp)
        cb_s.append(nbs)
        cc_s.append(ncs)
        v_s.append(vs)
    hp = _combine(hp.reshape(nb * t, d), *prev, lw["vecd"], depth - 1, alpha, TM // WIN, 0).reshape(nb, t, d)
    y_sample = jnp.transpose(hs.reshape(nt_s, ns, d), (1, 0, 2))
    untime = lambda xs_: jnp.transpose(jnp.stack(xs_), (0, 2, 1, 3))
    chunk_v = jnp.transpose(jnp.stack(v_s).reshape(depth, nt_s, ns, A_WIDTH), (0, 2, 1, 3))
    return (hp, y_sample, jnp.stack(cb_p), jnp.stack(cc_p), untime(cb_s), untime(cc_s), chunk_v)
```

```python
import functools
import math

import jax
import jax.numpy as jnp
from jax import lax
from jax.experimental import pallas as pl
from jax.experimental.pallas import tpu as pltpu

F32 = jnp.float32
BF16 = jnp.bfloat16

HEAD_DIM = 64
A_WIDTH = 384
B_WIDTH = 384
C_WIDTH = 256
GMLP_CHUNK = 128
CONV_B_TAPS = 31
CONV_C_TAPS = 3
N_GROUPS = 4
EXPERTS_PER_GROUP = 8
N_EXPERTS = N_GROUPS * EXPERTS_PER_GROUP
LN_EPS = 1e-5
INV_SQRT2 = 1.0 / math.sqrt(2.0)

LANES = 128
SUBLANES = 8
WIN = 256
CH = 16
WIN_ROWS = -(-(2 * WIN + N_EXPERTS * (CH - 1)) // WIN) * WIN
WIN_CHUNKS = WIN_ROWS // CH
TILE_CHUNKS = 32
TILE_ROWS = TILE_CHUNKS * CH
TM = 512
CONV_RB = 32
DMA_UNROLL = 4
N_SLOTS = 3
GATHER_AHEAD = N_SLOTS - 1
VMEM_LIMIT = 56 * 1024 * 1024


def _ln(x, g, b):
    mu = jnp.mean(x, axis=-1, keepdims=True)
    xc = x - mu
    var = jnp.mean(xc * xc, axis=-1, keepdims=True)
    return xc * lax.rsqrt(var + LN_EPS) * g + b


def _gelu(x):
    return 0.5 * x * (1.0 + lax.erf(x * INV_SQRT2))


def _silu(x):
    return x * jax.nn.sigmoid(x)


def _dot(a, b):
    return jnp.dot(a, b, preferred_element_type=F32)


def _iota(shape, dim):
    return lax.broadcasted_iota(jnp.int32, shape, dim)


def _layer_spec(arr, layer):
    rest = arr.shape[1:]
    return pl.BlockSpec((None,) + rest, lambda *_: (layer,) + (0,) * len(rest))


def _route_window(logits, hb):
    w = logits.shape[0]
    lane_i = _iota((w, LANES), 1)
    lane = lane_i.astype(F32)
    neg = jnp.float32(-jnp.inf)
    none = jnp.float32(LANES)
    gmask = lane_i < N_GROUPS
    gl = jnp.where(gmask, logits, neg)
    gmax = jnp.max(gl, axis=-1, keepdims=True)
    gsel = jnp.min(jnp.where(gl == gmax, lane, none), axis=-1, keepdims=True)
    den = jnp.sum(jnp.where(gmask, jnp.exp(gl - gmax), 0.0), axis=-1, keepdims=True)
    gw = 1.0 / den
    lo = N_GROUPS + EXPERTS_PER_GROUP * gsel
    el = jnp.where((lane >= lo) & (lane < lo + EXPERTS_PER_GROUP), logits, neg)
    v1 = jnp.max(el, axis=-1, keepdims=True)
    i1 = jnp.min(jnp.where(el == v1, lane, none), axis=-1, keepdims=True)
    el2 = jnp.where(lane == i1, neg, el)
    v2 = jnp.max(el2, axis=-1, keepdims=True)
    i2 = jnp.min(jnp.where(el2 == v2, lane, none), axis=-1, keepdims=True)
    e2x = jnp.exp(v2 - v1)
    w1 = gw / (1.0 + e2x)
    w2 = gw * e2x / (1.0 + e2x)
    e1 = i1 - N_GROUPS
    e2 = i2 - N_GROUPS

    oh = (lane == e1) | (lane == e2)
    ohf = jnp.where(oh, 1.0, 0.0)
    lstrict = jnp.where(_iota((w, w), 0) > _iota((w, w), 1), 1.0, 0.0).astype(BF16)
    rank = _dot(lstrict, ohf.astype(BF16))
    cnt = jnp.sum(ohf, axis=0, keepdims=True)
    nch = jnp.floor((cnt + (CH - 1.0)) * (1.0 / CH))
    upper = jnp.where(_iota((LANES, LANES), 0) < _iota((LANES, LANES), 1), 1.0, 0.0).astype(BF16)
    off = _dot(jnp.broadcast_to(nch, (SUBLANES, LANES)).astype(BF16), upper)[0:1]
    dest = off * CH + rank
    r1 = jnp.sum(jnp.where(lane == e1, dest, 0.0), axis=-1, keepdims=True)
    r2 = jnp.sum(jnp.where(lane == e2, dest, 0.0), axis=-1, keepdims=True)
    info = jnp.where(lane_i == 0, r1, jnp.where(lane_i == 1, r2,
                     jnp.where(lane_i == 2, w1, jnp.where(lane_i == 3, w2, 0.0))))
    info_t = info.T
    rows = _iota((WIN_ROWS, w), 0).astype(F32)
    perm = jnp.where((rows == info_t[0:1, :]) | (rows == info_t[1:2, :]), 1.0, 0.0).astype(BF16)
    xs = _dot(perm, hb).astype(BF16)
    sub = _iota((SUBLANES, LANES), 0)
    tab = jnp.where(sub == 0, nch, jnp.where(sub == 1, off, 0.0)).astype(jnp.int32)
    return xs, info, tab


def _unsort_norm(h1, info, ys, g, b, alpha):
    col = _iota((WIN, WIN_ROWS), 1).astype(F32)
    pt = jnp.where(col == info[:, 0:1], info[:, 2:3],
                   jnp.where(col == info[:, 1:2], info[:, 3:4], 0.0)).astype(BF16)
    return _ln(alpha * h1 + _dot(pt, ys), g, b)


def _post_mix(h, mixin, w_out_ref, vecd_ref, wr_ref, alpha, h1_ref, xs_ref, info_ref, tab_ref):
    mix_out = _dot(mixin, w_out_ref[...])
    h1 = _ln(alpha * h + mix_out, vecd_ref[0:1, :], vecd_ref[1:2, :])
    h1_ref[...] = h1
    hb = h1.astype(BF16)
    h_lo = (h1 - hb.astype(F32)).astype(BF16)
    l2 = _dot(hb, wr_ref[...])
    logits = l2[:, 0:LANES] + (l2[:, LANES:] + _dot(h_lo, wr_ref[:, 0:LANES]))
    for wi in range(h.shape[0] // WIN):
        rs = slice(wi * WIN, (wi + 1) * WIN)
        xs, info, tab = _route_window(logits[rs], hb[rs])
        xs_ref[wi] = xs
        info_ref[wi] = info
        tab_ref[wi] = tab


def _mixer_prompt_body(first, alpha, nj, n_tiles, *refs):
    n_prev = 0 if first else 3
    h_ref, prev = refs[0], refs[1:1 + n_prev]
    (w_in_ref, ws_ref, bs_ref, veca_ref, cbw_ref, ccw_ref, w_out_ref, vecd_ref, wr_ref,
     xs_s_ref, info_s_ref, tab_s_ref,
     h1_ref, xs_ref, info_ref, tab_ref, nsb_ref, nsc_ref,
     xsh, cbuf, mixin, wb, cbs) = refs[1 + n_prev:]
    i = pl.program_id(0)
    j = i % nj

    @pl.when((i < n_tiles) & (j == 0))
    def _():
        xsh[0, 0:32, :] = jnp.zeros((32, B_WIDTH), F32)
        cbuf[0:8, :] = jnp.zeros((8, C_WIDTH), F32)
        for k in range(CONV_B_TAPS):
            wb[k] = jnp.broadcast_to(cbw_ref[k:k + 1, :], (SUBLANES, B_WIDTH))

    @pl.when(i < n_tiles)
    def _():
        new_b, new_c = _mixer_prompt_tile(first, alpha, h_ref, prev, w_in_ref, ws_ref, bs_ref, veca_ref,
                                          ccw_ref, w_out_ref, vecd_ref, wr_ref,
                                          h1_ref, xs_ref, info_ref, tab_ref, xsh, cbuf, mixin, wb, cbs)

        @pl.when(j == nj - 1)
        def _():
            nsb_ref[0] = new_b
            nsc_ref[0] = new_c

    @pl.when(i >= n_tiles)
    def _():
        xs_ref[...] = xs_s_ref[...]
        info_ref[...] = info_s_ref[...]
        tab_ref[...] = tab_s_ref[...]


def _mixer_prompt_tile(first, alpha, h_ref, prev, w_in_ref, ws_ref, bs_ref, veca_ref, ccw_ref,
                       w_out_ref, vecd_ref, wr_ref, h1_ref, xs_ref, info_ref, tab_ref,
                       xsh, cbuf, mixin, wb, cbs):
    past_b = CONV_B_TAPS - 1
    past_c = CONV_C_TAPS - 1
    b0 = 32 - past_b
    c0 = 8 - past_c
    n_half = TM // WIN
    oa, ob = 2 * A_WIDTH, 2 * A_WIDTH + 2 * B_WIDTH

    tri = (_iota((2 * GMLP_CHUNK, GMLP_CHUNK), 0) % GMLP_CHUNK) >= _iota((2 * GMLP_CHUNK, GMLP_CHUNK), 1)
    wms = [jnp.where(tri, ws_ref[p], 0.0).astype(BF16) for p in range(A_WIDTH // LANES)]
    lane = _iota((GMLP_CHUNK, LANES), 1)

    hs, bgs = [], []
    for hf in range(n_half):
        r0 = hf * WIN
        h = h_ref[0, r0:r0 + WIN, :]
        if first:
            h = _ln(h, vecd_ref[4:5, :], vecd_ref[5:6, :])
        else:
            info_p, ys_p, vecd_p = prev
            h = _unsort_norm(h, info_p[hf], ys_p[hf], vecd_p[2:3, :], vecd_p[3:4, :], alpha)
        hs.append(h)
        hb = h.astype(BF16)
        za = _dot(hb, w_in_ref[:, 0:oa])
        zb = _dot(hb, w_in_ref[:, oa:ob])
        zc = _dot(hb, w_in_ref[:, ob:ob + 3 * C_WIDTH])

        ga = _gelu(za)
        u = ga[:, 0:A_WIDTH]
        vb = _ln(ga[:, A_WIDTH:], veca_ref[0:1, :], veca_ref[1:2, :]).astype(BF16)
        for c in range(WIN // GMLP_CHUNK):
            rs = slice(c * GMLP_CHUNK, (c + 1) * GMLP_CHUNK)
            parts = []
            for p in range(A_WIDTH // LANES):
                ab = _dot(wms[p], vb[rs, p * LANES:(p + 1) * LANES])
                parts.append(jnp.where(lane < HEAD_DIM, ab[:GMLP_CHUNK], ab[GMLP_CHUNK:]))
            mix = jnp.concatenate(parts, axis=1) + bs_ref[...]
            mixin[r0 + c * GMLP_CHUNK:r0 + (c + 1) * GMLP_CHUNK, 0:A_WIDTH] = (u[rs] * mix).astype(BF16)

        xsh[0, 32 + r0:32 + r0 + WIN, :] = zb[:, 0:B_WIDTH] * jax.nn.sigmoid(zb[:, B_WIDTH:])
        lo = 0 if hf == 0 else r0 + 24
        for r in range(1, SUBLANES):
            xsh[r, lo:r0 + WIN + 24, :] = xsh[0, lo + r:r0 + WIN + 24 + r, :]

        bgs.append(zc[:, 0:C_WIDTH])
        cbuf[8 + r0:8 + r0 + WIN, :] = zc[:, C_WIDTH:2 * C_WIDTH] * zc[:, 2 * C_WIDTH:]

    n_sub = CONV_RB // SUBLANES
    for hf in range(n_half):
        r0 = hf * WIN
        for rb in range(WIN // CONV_RB):
            base = r0 + rb * CONV_RB
            accs = [jnp.broadcast_to(veca_ref[2:3, :], (SUBLANES, B_WIDTH))] * n_sub
            for k in range(CONV_B_TAPS):
                s = k + b0
                w8 = wb[k]
                for a in range(n_sub):
                    row = base + (s // SUBLANES + a) * SUBLANES
                    accs[a] = accs[a] + w8 * xsh[s % SUBLANES, row:row + SUBLANES, :]
            for a in range(n_sub):
                cbs[base + a * SUBLANES:base + (a + 1) * SUBLANES, :] = accs[a]
        yb = _silu(_ln(cbs[r0:r0 + WIN, :], veca_ref[3:4, :], veca_ref[4:5, :]))
        mixin[r0:r0 + WIN, A_WIDTH:A_WIDTH + B_WIDTH] = yb.astype(BF16)

        cc = ccw_ref[0:1, :] * cbuf[c0 + r0:c0 + r0 + WIN, :]
        for k in range(1, CONV_C_TAPS):
            cc = cc + ccw_ref[k:k + 1, :] * cbuf[c0 + r0 + k:c0 + r0 + k + WIN, :]
        mixin[r0:r0 + WIN, A_WIDTH + B_WIDTH:] = (bgs[hf] * cc).astype(BF16)

        _post_mix(hs[hf], mixin[r0:r0 + WIN, :], w_out_ref, vecd_ref, wr_ref, alpha,
                  h1_ref.at[0, r0:r0 + WIN], xs_ref.at[hf:hf + 1], info_ref.at[hf:hf + 1],
                  tab_ref.at[hf:hf + 1])

    new_b = xsh[0, TM + b0:TM + 32, :]
    xsh[0, b0:32, :] = new_b
    new_c = cbuf[TM + c0:TM + 8, :]
    cbuf[c0:8, :] = new_c
    return new_b, new_c


def _mixer_prompt(h, prev, lw, layer, alpha, xs_s, info_s, tab_s):
    first = layer == 0
    nb, t, d = h.shape
    nj = t // TM
    wpt = TM // WIN
    n_tiles = nb * nj
    n_extra = xs_s.shape[0] // wpt
    n_win_total = (n_tiles + n_extra) * wpt
    tile = lambda i: jnp.minimum(i, n_tiles - 1)
    h_map = lambda i: (tile(i) // nj, tile(i) % nj, 0)
    state_map = lambda i: (tile(i) // nj, 0, 0)
    extra_map = lambda i: (jnp.maximum(i - n_tiles, 0), 0, 0)
    prev_win_map = lambda i: (tile(i), 0, 0)
    prev_specs = [] if first else [pl.BlockSpec((wpt, WIN, LANES), prev_win_map),
                                   pl.BlockSpec((wpt, WIN_ROWS, d), prev_win_map),
                                   _layer_spec(lw["vecd"], layer - 1)]
    prev = prev if first else (*prev, lw["vecd"])
    weights = [lw[k] for k in ("w_in", "ws2", "bs_exp", "veca", "cbw", "ccw", "w_out", "vecd", "wr")]
    out_shape = (
        jax.ShapeDtypeStruct((nb, t, d), F32),
        jax.ShapeDtypeStruct((n_win_total, WIN_ROWS, d), BF16),
        jax.ShapeDtypeStruct((n_win_total, WIN, LANES), F32),
        jax.ShapeDtypeStruct((n_win_total, SUBLANES, LANES), jnp.int32),
        jax.ShapeDtypeStruct((nb, CONV_B_TAPS - 1, B_WIDTH), F32),
        jax.ShapeDtypeStruct((nb, CONV_C_TAPS - 1, C_WIDTH), F32),
    )
    win_map = lambda i: (i, 0, 0)
    return pl.pallas_call(
        functools.partial(_mixer_prompt_body, first, alpha, nj, n_tiles),
        grid=(n_tiles + n_extra,),
        in_specs=[pl.BlockSpec((1, TM, d), h_map)] + prev_specs + [_layer_spec(a, layer) for a in weights] + [
            pl.BlockSpec((wpt, WIN_ROWS, d), extra_map),
            pl.BlockSpec((wpt, WIN, LANES), extra_map),
            pl.BlockSpec((wpt, SUBLANES, LANES), extra_map),
        ],
        out_specs=(
            pl.BlockSpec((1, TM, d), h_map),
            pl.BlockSpec((wpt, WIN_ROWS, d), win_map),
            pl.BlockSpec((wpt, WIN, LANES), win_map),
            pl.BlockSpec((wpt, SUBLANES, LANES), win_map),
            pl.BlockSpec((1, CONV_B_TAPS - 1, B_WIDTH), state_map),
            pl.BlockSpec((1, CONV_C_TAPS - 1, C_WIDTH), state_map),
        ),
        out_shape=out_shape,
        scratch_shapes=[
            pltpu.VMEM((SUBLANES, TM + 32, B_WIDTH), F32),
            pltpu.VMEM((TM + 8, C_WIDTH), F32),
            pltpu.VMEM((TM, d), BF16),
            pltpu.VMEM((CONV_B_TAPS, SUBLANES, B_WIDTH), F32),
            pltpu.VMEM((TM, B_WIDTH), F32),
        ],
        compiler_params=pltpu.CompilerParams(
            dimension_semantics=("arbitrary",), vmem_limit_bytes=VMEM_LIMIT),
        name="mixer_prompt",
    )(h, *prev, *weights, xs_s, info_s, tab_s)


def _mixer_sample_body(first, alpha, n_seq, n_t,
                       h_ref, sb_ref, sc_ref, w_in_ref, wexp_ref, bs_ref, veca_ref, cbw_ref, ccw_ref,
                       w_out_ref, vecd_ref, wr_ref,
                       h1_ref, v_ref, nsb_ref, nsc_ref, xs_ref, info_ref, tab_ref,
                       mixin):
    h = h_ref[...]
    if first:
        h = _ln(h, vecd_ref[4:5, :], vecd_ref[5:6, :])
    hb = h.astype(BF16)
    rows = lambda t: slice(t * n_seq, (t + 1) * n_seq)

    oa, ob = 2 * A_WIDTH, 2 * A_WIDTH + 2 * B_WIDTH
    ga = _gelu(_dot(hb, w_in_ref[:, 0:oa]))
    zb = _dot(hb, w_in_ref[:, oa:ob])
    zc = _dot(hb, w_in_ref[:, ob:ob + 3 * C_WIDTH])

    u = ga[:, 0:A_WIDTH]
    v = _ln(ga[:, A_WIDTH:], veca_ref[0:1, :], veca_ref[1:2, :])
    v_ref[...] = v
    for t in range(n_t):
        mix = jnp.broadcast_to(bs_ref[t:t + 1, :], (n_seq, A_WIDTH))
        for s in range(t + 1):
            mix = mix + wexp_ref[t, s:s + 1, :] * v[rows(s)]
        mixin[rows(t), 0:A_WIDTH] = (u[rows(t)] * mix).astype(BF16)

    glu = zb[:, 0:B_WIDTH] * jax.nn.sigmoid(zb[:, B_WIDTH:])
    past_b = CONV_B_TAPS - 1
    xp = lambda m: sb_ref[m] if m < past_b else glu[rows(m - past_b)]
    for t in range(n_t):
        acc = jnp.broadcast_to(veca_ref[2:3, :], (n_seq, B_WIDTH))
        for k in range(CONV_B_TAPS):
            acc = acc + cbw_ref[k:k + 1, :] * xp(t + k)
        yb = _silu(_ln(acc, veca_ref[3:4, :], veca_ref[4:5, :]))
        mixin[rows(t), A_WIDTH:A_WIDTH + B_WIDTH] = yb.astype(BF16)
    for r in range(past_b):
        nsb_ref[r] = xp(r + n_t)

    bg = zc[:, 0:C_WIDTH]
    xc = zc[:, C_WIDTH:2 * C_WIDTH] * zc[:, 2 * C_WIDTH:]
    past_c = CONV_C_TAPS - 1
    xq = lambda m: sc_ref[m] if m < past_c else xc[rows(m - past_c)]
    for t in range(n_t):
        cc = ccw_ref[0:1, :] * xq(t)
        for k in range(1, CONV_C_TAPS):
            cc = cc + ccw_ref[k:k + 1, :] * xq(t + k)
        mixin[rows(t), A_WIDTH + B_WIDTH:] = (bg[rows(t)] * cc).astype(BF16)
    for r in range(past_c):
        nsc_ref[r] = xq(r + n_t)

    _post_mix(h, mixin[...], w_out_ref, vecd_ref, wr_ref, alpha, h1_ref, xs_ref, info_ref, tab_ref)


def _mixer_sample(h, sb_t, sc_t, lw, layer, alpha):
    m, d = h.shape
    n_seq = sb_t.shape[2]
    n_t = m // n_seq
    nw = m // WIN
    full = lambda a: pl.BlockSpec(a.shape, lambda i: (0,) * a.ndim)
    stacked = [sb_t, sc_t] + [lw[k] for k in ("w_in", "wexp", "bs_exp", "veca", "cbw", "ccw", "w_out", "vecd", "wr")]
    out_shape = (
        jax.ShapeDtypeStruct((m, d), F32),
        jax.ShapeDtypeStruct((m, A_WIDTH), F32),
        jax.ShapeDtypeStruct(sb_t.shape[1:], F32),
        jax.ShapeDtypeStruct(sc_t.shape[1:], F32),
        jax.ShapeDtypeStruct((nw, WIN_ROWS, d), BF16),
        jax.ShapeDtypeStruct((nw, WIN, LANES), F32),
        jax.ShapeDtypeStruct((nw, SUBLANES, LANES), jnp.int32),
    )
    return pl.pallas_call(
        functools.partial(_mixer_sample_body, layer == 0, alpha, n_seq, n_t),
        grid=(1,),
        in_specs=[full(h)] + [_layer_spec(a, layer) for a in stacked],
        out_specs=tuple(full(o) for o in out_shape),
        out_shape=out_shape,
        scratch_shapes=[pltpu.VMEM((m, d), BF16)],
        compiler_params=pltpu.CompilerParams(
            dimension_semantics=("arbitrary",), vmem_limit_bytes=VMEM_LIMIT),
        name="mixer_sample",
    )(h, *stacked)


def _experts_body(layer, tile_e_ref, fresh_ref, wslot_ref, next_e_ref, nvalid_ref, clist_ref, ntiles_ref,
                  xs_hbm, wg_hbm, wu_hbm, wd_hbm, ys_hbm,
                  lhs, obuf, sg, su, sd, wgb, wub, wdb, sem_in, sem_out, sem_w):
    nt = ntiles_ref[0]

    def copy_in(cid, slot, j):
        return pltpu.make_async_copy(xs_hbm.at[cid], lhs.at[slot, pl.ds(j * CH, CH)], sem_in.at[slot])

    def copy_out(cid, slot, j):
        return pltpu.make_async_copy(obuf.at[slot, pl.ds(j * CH, CH)], ys_hbm.at[cid], sem_out.at[slot])

    def weight_copies(e, ws):
        row = layer * N_EXPERTS + e
        return [pltpu.make_async_copy(src.at[row], dst.at[ws], sem_w.at[ws])
                for src, dst in ((wg_hbm, sg), (wu_hbm, su), (wd_hbm, sd))]

    def start_all(make, tt, slot):
        n = nvalid_ref[tt]
        base = tt * TILE_CHUNKS
        groups = n // DMA_UNROLL

        def group(g, carry):
            for k in range(DMA_UNROLL):
                j = g * DMA_UNROLL + k
                make(clist_ref[base + j], slot, j).start()
            return carry

        def single(j, carry):
            make(clist_ref[base + j], slot, j).start()
            return carry

        lax.fori_loop(0, groups, group, 0)
        lax.fori_loop(groups * DMA_UNROLL, n, single, 0)

    def wait_all(make, tt, slot):
        def body(j, carry):
            make(0, slot, j).wait()
            return carry
        lax.fori_loop(0, nvalid_ref[tt], body, 0)

    def start_full(make, tt, slot):
        for j in range(TILE_CHUNKS):
            make(clist_ref[tt * TILE_CHUNKS + j], slot, j).start()

    def wait_full(make, slot):
        for j in range(TILE_CHUNKS):
            make(0, slot, j).wait()

    for c in weight_copies(tile_e_ref[0], 0):
        c.start()
    for tt in range(GATHER_AHEAD):
        @pl.when(tt < nt)
        def _():
            start_full(copy_in, tt, tt % N_SLOTS)

    def tile(t, carry):
        slot = t % N_SLOTS
        t_gather = t + GATHER_AHEAD
        t_drain = t - N_SLOTS

        @pl.when(fresh_ref[t] == 1)
        def _():
            ws = wslot_ref[t]
            for c in weight_copies(0, ws):
                c.wait()
            wgb[...] = sg[ws].astype(BF16)
            wub[...] = su[ws].astype(BF16)
            wdb[...] = sd[ws].astype(BF16)

            @pl.when(next_e_ref[t] >= 0)
            def _():
                for c in weight_copies(next_e_ref[t], 1 - ws):
                    c.start()

        def ffn(rows=TILE_ROWS):
            x = lhs[slot, 0:rows, :]
            act = _silu(_dot(x, wgb[...])) * _dot(x, wub[...])
            obuf[slot, 0:rows, :] = _dot(act.astype(BF16), wdb[...]).astype(BF16)

        wait_full(copy_in, slot)

        drain_full = (t_drain >= 0) & (nvalid_ref[jnp.maximum(t_drain, 0)] == TILE_CHUNKS)

        @pl.when(drain_full)
        def _():
            wait_full(copy_out, slot)

        @pl.when((t_drain >= 0) & jnp.logical_not(drain_full))
        def _():
            wait_all(copy_out, t_drain, slot)

        gather_next = t_gather < nt
        scatter_full = (t >= 1) & (nvalid_ref[jnp.maximum(t - 1, 0)] == TILE_CHUNKS)

        short = nvalid_ref[t] <= TILE_CHUNKS // 2
        whole = jnp.logical_not(short)
        static_scatter = gather_next & scatter_full & whole

        @pl.when((t >= 1) & jnp.logical_not(static_scatter))
        def _():
            start_all(copy_out, t - 1, (t - 1) % N_SLOTS)

        @pl.when(static_scatter)
        def _():
            start_full(copy_in, t_gather, t_gather % N_SLOTS)
            start_full(copy_out, t - 1, (t - 1) % N_SLOTS)
            ffn()

        @pl.when(gather_next & jnp.logical_not(scatter_full) & whole)
        def _():
            start_full(copy_in, t_gather, t_gather % N_SLOTS)
            ffn()

        @pl.when(gather_next & short)
        def _():
            start_full(copy_in, t_gather, t_gather % N_SLOTS)
            ffn(TILE_ROWS // 2)

        @pl.when(jnp.logical_not(gather_next) & whole)
        def _():
            ffn()

        @pl.when(jnp.logical_not(gather_next) & short)
        def _():
            ffn(TILE_ROWS // 2)

        @pl.when(t == nt - 1)
        def _():
            start_all(copy_out, t, slot)

        return carry

    lax.fori_loop(0, nt, tile, 0)
    for back in range(N_SLOTS, 0, -1):
        @pl.when(nt >= back)
        def _():
            wait_all(copy_out, nt - back, (nt - back) % N_SLOTS)


def _experts(xs, tables, wg, wu, wd, layer):
    n_win, _, d = xs.shape
    xc = xs.reshape(n_win * WIN_CHUNKS, CH, d)
    de = wg.shape[-1]
    anyspec = pl.BlockSpec(memory_space=pl.ANY)
    grid_spec = pltpu.PrefetchScalarGridSpec(
        num_scalar_prefetch=len(tables),
        grid=(1,),
        in_specs=[anyspec, anyspec, anyspec, anyspec],
        out_specs=anyspec,
        scratch_shapes=[
            pltpu.VMEM((N_SLOTS, TILE_ROWS, d), BF16),
            pltpu.VMEM((N_SLOTS, TILE_ROWS, d), BF16),
            pltpu.VMEM((2, d, de), F32),
            pltpu.VMEM((2, d, de), F32),
            pltpu.VMEM((2, de, d), F32),
            pltpu.VMEM((d, de), BF16),
            pltpu.VMEM((d, de), BF16),
            pltpu.VMEM((de, d), BF16),
            pltpu.SemaphoreType.DMA((N_SLOTS,)),
            pltpu.SemaphoreType.DMA((N_SLOTS,)),
            pltpu.SemaphoreType.DMA((2,)),
        ],
    )
    ys = pl.pallas_call(
        functools.partial(_experts_body, layer),
        grid_spec=grid_spec,
        out_shape=jax.ShapeDtypeStruct(xc.shape, xc.dtype),
        input_output_aliases={len(tables): 0},
        compiler_params=pltpu.CompilerParams(
            dimension_semantics=("arbitrary",), vmem_limit_bytes=VMEM_LIMIT),
        name="experts",
    )(*tables, xc, wg, wu, wd)
    return ys.reshape(xs.shape)


def _expert_tables(tab, t_max):
    i32 = jnp.int32
    nch = tab[:, 0, :N_EXPERTS]
    off = tab[:, 1, :N_EXPERTS]
    n_win = nch.shape[0]
    cum_incl = jnp.cumsum(nch, axis=0)
    cum_excl = cum_incl - nch
    ce = cum_incl[-1]
    te = (ce + TILE_CHUNKS - 1) // TILE_CHUNKS
    tile_end = jnp.cumsum(te)
    tile_start = tile_end - te
    nt = tile_end[-1]
    t_idx = jnp.arange(t_max, dtype=i32)
    tile_e = jnp.minimum(jnp.sum((tile_end[None, :] <= t_idx[:, None]).astype(i32), axis=1), N_EXPERTS - 1)
    sel_e = (tile_e[:, None] == jnp.arange(N_EXPERTS, dtype=i32)[None, :]).astype(i32)
    pick = lambda v: jnp.sum(sel_e * v[None, :], axis=1)
    q0 = (t_idx - pick(tile_start)) * TILE_CHUNKS
    nvalid = jnp.where(t_idx < nt, jnp.clip(pick(ce) - q0, 0, TILE_CHUNKS), 0).astype(i32)
    q = q0[:, None] + jnp.arange(TILE_CHUNKS, dtype=i32)[None, :]
    by_win = lambda m: jnp.sum(sel_e[:, :, None] * m.T[None, :, :], axis=1)
    cum_e = by_win(cum_incl)
    w = jnp.minimum(jnp.sum((cum_e[:, None, :] <= q[:, :, None]).astype(i32), axis=-1), n_win - 1)
    sel_w = (w[:, :, None] == jnp.arange(n_win, dtype=i32)[None, None, :]).astype(i32)
    shift = jnp.sum(sel_w * by_win(off - cum_excl)[:, None, :], axis=-1)
    cid = w * WIN_CHUNKS + shift + q
    valid = jnp.arange(TILE_CHUNKS, dtype=i32)[None, :] < nvalid[:, None]
    clist = jnp.where(valid, cid, cid[:, 0:1]).astype(i32).reshape(-1)
    prev_e = jnp.concatenate([jnp.full((1,), -1, i32), tile_e[:-1]])
    fresh = ((t_idx < nt) & (tile_e != prev_e)).astype(i32)
    wslot = (jnp.cumsum(fresh) - 1) % 2
    later_fresh = (t_idx[None, :] > t_idx[:, None]) & (fresh[None, :] == 1)
    nxt = jnp.min(jnp.where(later_fresh, t_idx[None, :], t_max), axis=1)
    next_e = jnp.where(nxt < t_max, jnp.sum((t_idx[None, :] == nxt[:, None]).astype(i32) * tile_e[None, :], axis=1), -1)
    return (tile_e.astype(i32), fresh, wslot.astype(i32), next_e.astype(i32), nvalid, clist,
            nt.astype(i32).reshape(1))


def _combine_body(alpha, h1_ref, info_ref, ys_ref, vecd_ref, out_ref):
    for wi in range(info_ref.shape[0]):
        rs = slice(wi * WIN, (wi + 1) * WIN)
        out_ref[rs, :] = _unsort_norm(h1_ref[rs, :], info_ref[wi], ys_ref[wi],
                                      vecd_ref[2:3, :], vecd_ref[3:4, :], alpha)


def _combine(h1, info, ys, vecd, layer, alpha, wpb, block0):
    m, d = h1.shape
    win_map = lambda i: (block0 + i, 0, 0)
    return pl.pallas_call(
        functools.partial(_combine_body, alpha),
        grid=(m // (wpb * WIN),),
        in_specs=[
            pl.BlockSpec((wpb * WIN, d), lambda i: (i, 0)),
            pl.BlockSpec((wpb, WIN, LANES), win_map),
            pl.BlockSpec((wpb, WIN_ROWS, d), win_map),
            _layer_spec(vecd, layer),
        ],
        out_specs=pl.BlockSpec((wpb * WIN, d), lambda i: (i, 0)),
        out_shape=jax.ShapeDtypeStruct((m, d), F32),
        compiler_params=pltpu.CompilerParams(
            dimension_semantics=("arbitrary",), vmem_limit_bytes=VMEM_LIMIT),
        name="combine",
    )(h1, info, ys, vecd)


def kernel(x_prompt, x_sample, state_conv_b, state_conv_c, ln_in_g, ln_in_b, w_in, w_s, b_s, ln_v_g, ln_v_b, conv_b_w, conv_b_bias, ln_conv_g, ln_conv_b, conv_c_w, w_out, ln1_g, ln1_b, w_router_group, w_router_expert, w_gate, w_up, w_down, ln2_g, ln2_b):
    depth = w_in.shape[0]
    nb, t, d = x_prompt.shape
    ns, nt_s, _ = x_sample.shape
    assert t % TM == 0 and (ns * nt_s) % WIN == 0 and TM % WIN == 0 and d % LANES == 0
    alpha = (2.0 * depth) ** 0.25
    n_win_p = nb * t // WIN
    n_win_s = ns * nt_s // WIN
    n_win = n_win_p + n_win_s
    assert n_win_p % n_win_s == 0 and n_win_s % (TM // WIN) == 0
    max_chunks = n_win * (2 * WIN // CH + N_EXPERTS)
    t_max = max_chunks // TILE_CHUNKS + N_EXPERTS

    wr = jnp.concatenate([w_router_group, w_router_expert.reshape(depth, d, N_EXPERTS)], axis=2)
    wr = jnp.pad(wr, ((0, 0), (0, 0), (0, LANES - wr.shape[2])))
    wr_hi = wr.astype(BF16)
    zeros_a = jnp.zeros((depth, A_WIDTH), F32)
    zeros_d = jnp.zeros((depth, d), F32)
    over_layers = lambda v: jnp.broadcast_to(v, (depth,) + v.shape)
    lw = dict(
        w_in=w_in.astype(BF16),
        ws2=w_s.reshape(depth, A_WIDTH // LANES, 2 * GMLP_CHUNK, GMLP_CHUNK),
        wexp=jnp.repeat(jnp.transpose(w_s[:, :, :nt_s, :nt_s], (0, 2, 3, 1)), HEAD_DIM, axis=-1),
        bs_exp=jnp.repeat(jnp.transpose(b_s, (0, 2, 1)), HEAD_DIM, axis=-1),
        veca=jnp.stack([ln_v_g, ln_v_b, conv_b_bias, ln_conv_g, ln_conv_b, zeros_a, zeros_a, zeros_a], axis=1),
        cbw=conv_b_w, ccw=conv_c_w,
        w_out=w_out.astype(BF16),
        vecd=jnp.stack([ln1_g, ln1_b, ln2_g, ln2_b, over_layers(ln_in_g), over_layers(ln_in_b),
                        zeros_d, zeros_d], axis=1),
        wr=jnp.concatenate([wr_hi, (wr - wr_hi.astype(F32)).astype(BF16)], axis=2),
    )
    ne = N_EXPERTS
    wg, wu, wd = (w_gate.reshape(depth * ne, d, -1), w_up.reshape(depth * ne, d, -1),
                  w_down.reshape(depth * ne, -1, d))

    hp = x_prompt
    hs = jnp.transpose(x_sample, (1, 0, 2)).reshape(nt_s * ns, d)
    sb_t = jnp.transpose(state_conv_b, (0, 2, 1, 3))
    sc_t = jnp.transpose(state_conv_c, (0, 2, 1, 3))
    cb_p, cc_p, cb_s, cc_s, v_s = [], [], [], [], []
    prev = ()
    for l in range(depth):
        h1s, vs, nbs, ncs, xs_s, info_s, tab_s = _mixer_sample(hs, sb_t, sc_t, lw, l, alpha)
        h1p, xs, info, tab, nbp, ncp = _mixer_prompt(hp, prev, lw, l, alpha, xs_s, info_s, tab_s)
        ys = _experts(xs, _expert_tables(tab, t_max), wg, wu, wd, l)
        hp, prev = h1p, (info, ys)
        hs = _combine(h1s, info, ys, lw["vecd"], l, alpha, n_win_s, n_win_p // n_win_s)
        cb_p.append(nbp)
        cc_p.append(ncp)
        cb_s.append(nbs)
        cc_s.append(ncs)
        v_s.append(vs)
    hp = _combine(hp.reshape(nb * t, d), *prev, lw["vecd"], depth - 1, alpha, TM // WIN, 0).reshape(nb, t, d)
    y_sample = jnp.transpose(hs.reshape(nt_s, ns, d), (1, 0, 2))
    untime = lambda xs_: jnp.transpose(jnp.stack(xs_), (0, 2, 1, 3))
    chunk_v = jnp.transpose(jnp.stack(v_s).reshape(depth, nt_s, ns, A_WIDTH), (0, 2, 1, 3))
    return (hp, y_sample, jnp.stack(cb_p), jnp.stack(cc_p), untime(cb_s), untime(cc_s), chunk_v)
```

```python
import functools
import math

import jax
import jax.numpy as jnp
from jax import lax
from jax.experimental import pallas as pl
from jax.experimental.pallas import tpu as pltpu

F32 = jnp.float32
BF16 = jnp.bfloat16

HEAD_DIM = 64
A_WIDTH = 384
B_WIDTH = 384
C_WIDTH = 256
GMLP_CHUNK = 128
CONV_B_TAPS = 31
CONV_C_TAPS = 3
N_GROUPS = 4
EXPERTS_PER_GROUP = 8
N_EXPERTS = N_GROUPS * EXPERTS_PER_GROUP
LN_EPS = 1e-5
INV_SQRT2 = 1.0 / math.sqrt(2.0)

LANES = 128
SUBLANES = 8
WIN = 256
CH = 16
WIN_ROWS = -(-(2 * WIN + N_EXPERTS * (CH - 1)) // WIN) * WIN
WIN_CHUNKS = WIN_ROWS // CH
TILE_CHUNKS = 32
TILE_ROWS = TILE_CHUNKS * CH
TM = 512
CONV_RB = 32
DMA_UNROLL = 4
N_SLOTS = 3
GATHER_AHEAD = N_SLOTS - 1
VMEM_LIMIT = 56 * 1024 * 1024


def _ln(x, g, b):
    mu = jnp.mean(x, axis=-1, keepdims=True)
    xc = x - mu
    var = jnp.mean(xc * xc, axis=-1, keepdims=True)
    return xc * lax.rsqrt(var + LN_EPS) * g + b


def _gelu(x):
    return 0.5 * x * (1.0 + lax.erf(x * INV_SQRT2))


def _silu(x):
    return x * jax.nn.sigmoid(x)


def _dot(a, b):
    return jnp.dot(a, b, preferred_element_type=F32)


def _iota(shape, dim):
    return lax.broadcasted_iota(jnp.int32, shape, dim)


def _layer_spec(arr, layer):
    rest = arr.shape[1:]
    return pl.BlockSpec((None,) + rest, lambda *_: (layer,) + (0,) * len(rest))


def _route_window(logits, hb):
    w = logits.shape[0]
    lane_i = _iota((w, LANES), 1)
    lane = lane_i.astype(F32)
    neg = jnp.float32(-jnp.inf)
    none = jnp.float32(LANES)
    gmask = lane_i < N_GROUPS
    gl = jnp.where(gmask, logits, neg)
    gmax = jnp.max(gl, axis=-1, keepdims=True)
    gsel = jnp.min(jnp.where(gl == gmax, lane, none), axis=-1, keepdims=True)
    den = jnp.sum(jnp.where(gmask, jnp.exp(gl - gmax), 0.0), axis=-1, keepdims=True)
    gw = 1.0 / den
    lo = N_GROUPS + EXPERTS_PER_GROUP * gsel
    el = jnp.where((lane >= lo) & (lane < lo + EXPERTS_PER_GROUP), logits, neg)
    v1 = jnp.max(el, axis=-1, keepdims=True)
    i1 = jnp.min(jnp.where(el == v1, lane, none), axis=-1, keepdims=True)
    el2 = jnp.where(lane == i1, neg, el)
    v2 = jnp.max(el2, axis=-1, keepdims=True)
    i2 = jnp.min(jnp.where(el2 == v2, lane, none), axis=-1, keepdims=True)
    e2x = jnp.exp(v2 - v1)
    w1 = gw / (1.0 + e2x)
    w2 = gw * e2x / (1.0 + e2x)
    e1 = i1 - N_GROUPS
    e2 = i2 - N_GROUPS

    oh = (lane == e1) | (lane == e2)
    ohf = jnp.where(oh, 1.0, 0.0)
    lstrict = jnp.where(_iota((w, w), 0) > _iota((w, w), 1), 1.0, 0.0).astype(BF16)
    rank = _dot(lstrict, ohf.astype(BF16))
    cnt = jnp.sum(ohf, axis=0, keepdims=True)
    nch = jnp.floor((cnt + (CH - 1.0)) * (1.0 / CH))
    upper = jnp.where(_iota((LANES, LANES), 0) < _iota((LANES, LANES), 1), 1.0, 0.0).astype(BF16)
    off = _dot(jnp.broadcast_to(nch, (SUBLANES, LANES)).astype(BF16), upper)[0:1]
    dest = off * CH + rank
    r1 = jnp.sum(jnp.where(lane == e1, dest, 0.0), axis=-1, keepdims=True)
    r2 = jnp.sum(jnp.where(lane == e2, dest, 0.0), axis=-1, keepdims=True)
    info = jnp.where(lane_i == 0, r1, jnp.where(lane_i == 1, r2,
                     jnp.where(lane_i == 2, w1, jnp.where(lane_i == 3, w2, 0.0))))
    info_t = info.T
    rows = _iota((WIN_ROWS, w), 0).astype(F32)
    perm = jnp.where((rows == info_t[0:1, :]) | (rows == info_t[1:2, :]), 1.0, 0.0).astype(BF16)
    xs = _dot(perm, hb).astype(BF16)
    sub = _iota((SUBLANES, LANES), 0)
    tab = jnp.where(sub == 0, nch, jnp.where(sub == 1, off, 0.0)).astype(jnp.int32)
    return xs, info, tab


def _unsort_norm(h1, info, ys, g, b, alpha):
    col = _iota((WIN, WIN_ROWS), 1).astype(F32)
    pt = jnp.where(col == info[:, 0:1], info[:, 2:3],
                   jnp.where(col == info[:, 1:2], info[:, 3:4], 0.0)).astype(BF16)
    return _ln(alpha * h1 + _dot(pt, ys), g, b)


def _post_mix(h, mixin, w_out_ref, vecd_ref, wr_ref, alpha, h1_ref, xs_ref, info_ref, tab_ref):
    mix_out = _dot(mixin, w_out_ref[...])
    h1 = _ln(alpha * h + mix_out, vecd_ref[0:1, :], vecd_ref[1:2, :])
    h1_ref[...] = h1
    hb = h1.astype(BF16)
    h_lo = (h1 - hb.astype(F32)).astype(BF16)
    l2 = _dot(hb, wr_ref[...])
    logits = l2[:, 0:LANES] + (l2[:, LANES:] + _dot(h_lo, wr_ref[:, 0:LANES]))
    for wi in range(h.shape[0] // WIN):
        rs = slice(wi * WIN, (wi + 1) * WIN)
        xs, info, tab = _route_window(logits[rs], hb[rs])
        xs_ref[wi] = xs
        info_ref[wi] = info
        tab_ref[wi] = tab


def _mixer_prompt_body(first, alpha, nj, n_tiles, *refs):
    n_prev = 0 if first else 3
    h_ref, prev = refs[0], refs[1:1 + n_prev]
    (w_in_ref, ws_ref, bs_ref, veca_ref, cbw_ref, ccw_ref, w_out_ref, vecd_ref, wr_ref,
     xs_s_ref, info_s_ref, tab_s_ref,
     h1_ref, xs_ref, info_ref, tab_ref, nsb_ref, nsc_ref,
     xsh, cbuf, mixin, wb, cbs) = refs[1 + n_prev:]
    i = pl.program_id(0)
    j = i % nj

    @pl.when((i < n_tiles) & (j == 0))
    def _():
        xsh[0, 0:32, :] = jnp.zeros((32, B_WIDTH), F32)
        cbuf[0:8, :] = jnp.zeros((8, C_WIDTH), F32)
        for k in range(CONV_B_TAPS):
            wb[k] = jnp.broadcast_to(cbw_ref[k:k + 1, :], (SUBLANES, B_WIDTH))

    @pl.when(i < n_tiles)
    def _():
        new_b, new_c = _mixer_prompt_tile(first, alpha, h_ref, prev, w_in_ref, ws_ref, bs_ref, veca_ref,
                                          ccw_ref, w_out_ref, vecd_ref, wr_ref,
                                          h1_ref, xs_ref, info_ref, tab_ref, xsh, cbuf, mixin, wb, cbs)

        @pl.when(j == nj - 1)
        def _():
            nsb_ref[0] = new_b
            nsc_ref[0] = new_c

    @pl.when(i >= n_tiles)
    def _():
        xs_ref[...] = xs_s_ref[...]
        info_ref[...] = info_s_ref[...]
        tab_ref[...] = tab_s_ref[...]


def _mixer_prompt_tile(first, alpha, h_ref, prev, w_in_ref, ws_ref, bs_ref, veca_ref, ccw_ref,
                       w_out_ref, vecd_ref, wr_ref, h1_ref, xs_ref, info_ref, tab_ref,
                       xsh, cbuf, mixin, wb, cbs):
    past_b = CONV_B_TAPS - 1
    past_c = CONV_C_TAPS - 1
    b0 = 32 - past_b
    c0 = 8 - past_c
    n_half = TM // WIN
    oa, ob = 2 * A_WIDTH, 2 * A_WIDTH + 2 * B_WIDTH

    tri = (_iota((2 * GMLP_CHUNK, GMLP_CHUNK), 0) % GMLP_CHUNK) >= _iota((2 * GMLP_CHUNK, GMLP_CHUNK), 1)
    wms = [jnp.where(tri, ws_ref[p], 0.0).astype(BF16) for p in range(A_WIDTH // LANES)]
    lane = _iota((GMLP_CHUNK, LANES), 1)

    hs, bgs = [], []
    for hf in range(n_half):
        r0 = hf * WIN
        h = h_ref[0, r0:r0 + WIN, :]
        if first:
            h = _ln(h, vecd_ref[4:5, :], vecd_ref[5:6, :])
        else:
            info_p, ys_p, vecd_p = prev
            h = _unsort_norm(h, info_p[hf], ys_p[hf], vecd_p[2:3, :], vecd_p[3:4, :], alpha)
        hs.append(h)
        hb = h.astype(BF16)
        za = _dot(hb, w_in_ref[:, 0:oa])
        zb = _dot(hb, w_in_ref[:, oa:ob])
        zc = _dot(hb, w_in_ref[:, ob:ob + 3 * C_WIDTH])

        ga = _gelu(za)
        u = ga[:, 0:A_WIDTH]
        vb = _ln(ga[:, A_WIDTH:], veca_ref[0:1, :], veca_ref[1:2, :]).astype(BF16)
        for c in range(WIN // GMLP_CHUNK):
            rs = slice(c * GMLP_CHUNK, (c + 1) * GMLP_CHUNK)
            parts = []
            for p in range(A_WIDTH // LANES):
                ab = _dot(wms[p], vb[rs, p * LANES:(p + 1) * LANES])
                parts.append(jnp.where(lane < HEAD_DIM, ab[:GMLP_CHUNK], ab[GMLP_CHUNK:]))
            mix = jnp.concatenate(parts, axis=1) + bs_ref[...]
            mixin[r0 + c * GMLP_CHUNK:r0 + (c + 1) * GMLP_CHUNK, 0:A_WIDTH] = (u[rs] * mix).astype(BF16)

        xsh[0, 32 + r0:32 + r0 + WIN, :] = zb[:, 0:B_WIDTH] * jax.nn.sigmoid(zb[:, B_WIDTH:])
        lo = 0 if hf == 0 else r0 + 24
        for r in range(1, SUBLANES):
            xsh[r, lo:r0 + WIN + 24, :] = xsh[0, lo + r:r0 + WIN + 24 + r, :]

        bgs.append(zc[:, 0:C_WIDTH])
        cbuf[8 + r0:8 + r0 + WIN, :] = zc[:, C_WIDTH:2 * C_WIDTH] * zc[:, 2 * C_WIDTH:]

    n_sub = CONV_RB // SUBLANES
    for hf in range(n_half):
        r0 = hf * WIN
        for rb in range(WIN // CONV_RB):
            base = r0 + rb * CONV_RB
            accs = [jnp.broadcast_to(veca_ref[2:3, :], (SUBLANES, B_WIDTH))] * n_sub
            for k in range(CONV_B_TAPS):
                s = k + b0
                w8 = wb[k]
                for a in range(n_sub):
                    row = base + (s // SUBLANES + a) * SUBLANES
                    accs[a] = accs[a] + w8 * xsh[s % SUBLANES, row:row + SUBLANES, :]
            for a in range(n_sub):
                cbs[base + a * SUBLANES:base + (a + 1) * SUBLANES, :] = accs[a]
        yb = _silu(_ln(cbs[r0:r0 + WIN, :], veca_ref[3:4, :], veca_ref[4:5, :]))
        mixin[r0:r0 + WIN, A_WIDTH:A_WIDTH + B_WIDTH] = yb.astype(BF16)

        cc = ccw_ref[0:1, :] * cbuf[c0 + r0:c0 + r0 + WIN, :]
        for k in range(1, CONV_C_TAPS):
            cc = cc + ccw_ref[k:k + 1, :] * cbuf[c0 + r0 + k:c0 + r0 + k + WIN, :]
        mixin[r0:r0 + WIN, A_WIDTH + B_WIDTH:] = (bgs[hf] * cc).astype(BF16)

        _post_mix(hs[hf], mixin[r0:r0 + WIN, :], w_out_ref, vecd_ref, wr_ref, alpha,
                  h1_ref.at[0, r0:r0 + WIN], xs_ref.at[hf:hf + 1], info_ref.at[hf:hf + 1],
                  tab_ref.at[hf:hf + 1])

    new_b = xsh[0, TM + b0:TM + 32, :]
    xsh[0, b0:32, :] = new_b
    new_c = cbuf[TM + c0:TM + 8, :]
    cbuf[c0:8, :] = new_c
    return new_b, new_c


def _mixer_prompt(h, prev, lw, layer, alpha, xs_s, info_s, tab_s):
    first = layer == 0
    nb, t, d = h.shape
    nj = t // TM
    wpt = TM // WIN
    n_tiles = nb * nj
    n_extra = xs_s.shape[0] // wpt
    n_win_total = (n_tiles + n_extra) * wpt
    tile = lambda i: jnp.minimum(i, n_tiles - 1)
    h_map = lambda i: (tile(i) // nj, tile(i) % nj, 0)
    state_map = lambda i: (tile(i) // nj, 0, 0)
    extra_map = lambda i: (jnp.maximum(i - n_tiles, 0), 0, 0)
    prev_win_map = lambda i: (tile(i), 0, 0)
    prev_specs = [] if first else [pl.BlockSpec((wpt, WIN, LANES), prev_win_map),
                                   pl.BlockSpec((wpt, WIN_ROWS, d), prev_win_map),
                                   _layer_spec(lw["vecd"], layer - 1)]
    prev = prev if first else (*prev, lw["vecd"])
    weights = [lw[k] for k in ("w_in", "ws2", "bs_exp", "veca", "cbw", "ccw", "w_out", "vecd", "wr")]
    out_shape = (
        jax.ShapeDtypeStruct((nb, t, d), F32),
        jax.ShapeDtypeStruct((n_win_total, WIN_ROWS, d), BF16),
        jax.ShapeDtypeStruct((n_win_total, WIN, LANES), F32),
        jax.ShapeDtypeStruct((n_win_total, SUBLANES, LANES), jnp.int32),
        jax.ShapeDtypeStruct((nb, CONV_B_TAPS - 1, B_WIDTH), F32),
        jax.ShapeDtypeStruct((nb, CONV_C_TAPS - 1, C_WIDTH), F32),
    )
    win_map = lambda i: (i, 0, 0)
    return pl.pallas_call(
        functools.partial(_mixer_prompt_body, first, alpha, nj, n_tiles),
        grid=(n_tiles + n_extra,),
        in_specs=[pl.BlockSpec((1, TM, d), h_map)] + prev_specs + [_layer_spec(a, layer) for a in weights] + [
            pl.BlockSpec((wpt, WIN_ROWS, d), extra_map),
            pl.BlockSpec((wpt, WIN, LANES), extra_map),
            pl.BlockSpec((wpt, SUBLANES, LANES), extra_map),
        ],
        out_specs=(
            pl.BlockSpec((1, TM, d), h_map),
            pl.BlockSpec((wpt, WIN_ROWS, d), win_map),
            pl.BlockSpec((wpt, WIN, LANES), win_map),
            pl.BlockSpec((wpt, SUBLANES, LANES), win_map),
            pl.BlockSpec((1, CONV_B_TAPS - 1, B_WIDTH), state_map),
            pl.BlockSpec((1, CONV_C_TAPS - 1, C_WIDTH), state_map),
        ),
        out_shape=out_shape,
        scratch_shapes=[
            pltpu.VMEM((SUBLANES, TM + 32, B_WIDTH), F32),
            pltpu.VMEM((TM + 8, C_WIDTH), F32),
            pltpu.VMEM((TM, d), BF16),
            pltpu.VMEM((CONV_B_TAPS, SUBLANES, B_WIDTH), F32),
            pltpu.VMEM((TM, B_WIDTH), F32),
        ],
        compiler_params=pltpu.CompilerParams(
            dimension_semantics=("arbitrary",), vmem_limit_bytes=VMEM_LIMIT),
        name="mixer_prompt",
    )(h, *prev, *weights, xs_s, info_s, tab_s)


def _mixer_sample_body(first, alpha, n_seq, n_t,
                       h_ref, sb_ref, sc_ref, w_in_ref, wexp_ref, bs_ref, veca_ref, cbw_ref, ccw_ref,
                       w_out_ref, vecd_ref, wr_ref,
                       h1_ref, v_ref, nsb_ref, nsc_ref, xs_ref, info_ref, tab_ref,
                       mixin):
    h = h_ref[...]
    if first:
        h = _ln(h, vecd_ref[4:5, :], vecd_ref[5:6, :])
    hb = h.astype(BF16)
    rows = lambda t: slice(t * n_seq, (t + 1) * n_seq)

    oa, ob = 2 * A_WIDTH, 2 * A_WIDTH + 2 * B_WIDTH
    ga = _gelu(_dot(hb, w_in_ref[:, 0:oa]))
    zb = _dot(hb, w_in_ref[:, oa:ob])
    zc = _dot(hb, w_in_ref[:, ob:ob + 3 * C_WIDTH])

    u = ga[:, 0:A_WIDTH]
    v = _ln(ga[:, A_WIDTH:], veca_ref[0:1, :], veca_ref[1:2, :])
    v_ref[...] = v
    for t in range(n_t):
        mix = jnp.broadcast_to(bs_ref[t:t + 1, :], (n_seq, A_WIDTH))
        for s in range(t + 1):
            mix = mix + wexp_ref[t, s:s + 1, :] * v[rows(s)]
        mixin[rows(t), 0:A_WIDTH] = (u[rows(t)] * mix).astype(BF16)

    glu = zb[:, 0:B_WIDTH] * jax.nn.sigmoid(zb[:, B_WIDTH:])
    past_b = CONV_B_TAPS - 1
    xp = lambda m: sb_ref[m] if m < past_b else glu[rows(m - past_b)]
    for t in range(n_t):
        acc = jnp.broadcast_to(veca_ref[2:3, :], (n_seq, B_WIDTH))
        for k in range(CONV_B_TAPS):
            acc = acc + cbw_ref[k:k + 1, :] * xp(t + k)
        yb = _silu(_ln(acc, veca_ref[3:4, :], veca_ref[4:5, :]))
        mixin[rows(t), A_WIDTH:A_WIDTH + B_WIDTH] = yb.astype(BF16)
    for r in range(past_b):
        nsb_ref[r] = xp(r + n_t)

    bg = zc[:, 0:C_WIDTH]
    xc = zc[:, C_WIDTH:2 * C_WIDTH] * zc[:, 2 * C_WIDTH:]
    past_c = CONV_C_TAPS - 1
    xq = lambda m: sc_ref[m] if m < past_c else xc[rows(m - past_c)]
    for t in range(n_t):
        cc = ccw_ref[0:1, :] * xq(t)
        for k in range(1, CONV_C_TAPS):
            cc = cc + ccw_ref[k:k + 1, :] * xq(t + k)
        mixin[rows(t), A_WIDTH + B_WIDTH:] = (bg[rows(t)] * cc).astype(BF16)
    for r in range(past_c):
        nsc_ref[r] = xq(r + n_t)

    _post_mix(h, mixin[...], w_out_ref, vecd_ref, wr_ref, alpha, h1_ref, xs_ref, info_ref, tab_ref)


def _mixer_sample(h, sb_t, sc_t, lw, layer, alpha):
    m, d = h.shape
    n_seq = sb_t.shape[2]
    n_t = m // n_seq
    nw = m // WIN
    full = lambda a: pl.BlockSpec(a.shape, lambda i: (0,) * a.ndim)
    stacked = [sb_t, sc_t] + [lw[k] for k in ("w_in", "wexp", "bs_exp", "veca", "cbw", "ccw", "w_out", "vecd", "wr")]
    out_shape = (
        jax.ShapeDtypeStruct((m, d), F32),
        jax.ShapeDtypeStruct((m, A_WIDTH), F32),
        jax.ShapeDtypeStruct(sb_t.shape[1:], F32),
        jax.ShapeDtypeStruct(sc_t.shape[1:], F32),
        jax.ShapeDtypeStruct((nw, WIN_ROWS, d), BF16),
        jax.ShapeDtypeStruct((nw, WIN, LANES), F32),
        jax.ShapeDtypeStruct((nw, SUBLANES, LANES), jnp.int32),
    )
    return pl.pallas_call(
        functools.partial(_mixer_sample_body, layer == 0, alpha, n_seq, n_t),
        grid=(1,),
        in_specs=[full(h)] + [_layer_spec(a, layer) for a in stacked],
        out_specs=tuple(full(o) for o in out_shape),
        out_shape=out_shape,
        scratch_shapes=[pltpu.VMEM((m, d), BF16)],
        compiler_params=pltpu.CompilerParams(
            dimension_semantics=("arbitrary",), vmem_limit_bytes=VMEM_LIMIT),
        name="mixer_sample",
    )(h, *stacked)


def _experts_body(layer, tile_e_ref, fresh_ref, wslot_ref, next_e_ref, nvalid_ref, clist_ref, ntiles_ref,
                  xs_hbm, wg_hbm, wu_hbm, wd_hbm, ys_hbm,
                  lhs, obuf, sg, su, sd, wgb, wub, wdb, sem_in, sem_out, sem_w):
    nt = ntiles_ref[0]

    def copy_in(cid, slot, j):
        return pltpu.make_async_copy(xs_hbm.at[cid], lhs.at[slot, pl.ds(j * CH, CH)], sem_in.at[slot])

    def copy_out(cid, slot, j):
        return pltpu.make_async_copy(obuf.at[slot, pl.ds(j * CH, CH)], ys_hbm.at[cid], sem_out.at[slot])

    def weight_copies(e, ws):
        row = layer * N_EXPERTS + e
        return [pltpu.make_async_copy(src.at[row], dst.at[ws], sem_w.at[ws])
                for src, dst in ((wg_hbm, sg), (wu_hbm, su), (wd_hbm, sd))]

    def start_all(make, tt, slot):
        n = nvalid_ref[tt]
        base = tt * TILE_CHUNKS
        groups = n // DMA_UNROLL

        def group(g, carry):
            for k in range(DMA_UNROLL):
                j = g * DMA_UNROLL + k
                make(clist_ref[base + j], slot, j).start()
            return carry

        def single(j, carry):
            make(clist_ref[base + j], slot, j).start()
            return carry

        lax.fori_loop(0, groups, group, 0)
        lax.fori_loop(groups * DMA_UNROLL, n, single, 0)

    def wait_all(make, tt, slot):
        def body(j, carry):
            make(0, slot, j).wait()
            return carry
        lax.fori_loop(0, nvalid_ref[tt], body, 0)

    def start_full(make, tt, slot):
        for j in range(TILE_CHUNKS):
            make(clist_ref[tt * TILE_CHUNKS + j], slot, j).start()

    def wait_full(make, slot):
        for j in range(TILE_CHUNKS):
            make(0, slot, j).wait()

    for c in weight_copies(tile_e_ref[0], 0):
        c.start()
    for tt in range(GATHER_AHEAD):
        @pl.when(tt < nt)
        def _():
            start_full(copy_in, tt, tt % N_SLOTS)

    def tile(t, carry):
        slot = t % N_SLOTS
        t_gather = t + GATHER_AHEAD
        t_drain = t - N_SLOTS

        @pl.when(fresh_ref[t] == 1)
        def _():
            ws = wslot_ref[t]
            for c in weight_copies(0, ws):
                c.wait()
            wgb[...] = sg[ws].astype(BF16)
            wub[...] = su[ws].astype(BF16)
            wdb[...] = sd[ws].astype(BF16)

            @pl.when(next_e_ref[t] >= 0)
            def _():
                for c in weight_copies(next_e_ref[t], 1 - ws):
                    c.start()

        def ffn():
            x = lhs[slot]
            act = _silu(_dot(x, wgb[...])) * _dot(x, wub[...])
            obuf[slot] = _dot(act.astype(BF16), wdb[...]).astype(BF16)

        wait_full(copy_in, slot)

        drain_full = (t_drain >= 0) & (nvalid_ref[jnp.maximum(t_drain, 0)] == TILE_CHUNKS)

        @pl.when(drain_full)
        def _():
            wait_full(copy_out, slot)

        @pl.when((t_drain >= 0) & jnp.logical_not(drain_full))
        def _():
            wait_all(copy_out, t_drain, slot)

        gather_next = t_gather < nt
        scatter_full = (t >= 1) & (nvalid_ref[jnp.maximum(t - 1, 0)] == TILE_CHUNKS)

        @pl.when((t >= 1) & jnp.logical_not(gather_next & scatter_full))
        def _():
            start_all(copy_out, t - 1, (t - 1) % N_SLOTS)

        @pl.when(gather_next & scatter_full)
        def _():
            ffn()
            start_full(copy_in, t_gather, t_gather % N_SLOTS)
            start_full(copy_out, t - 1, (t - 1) % N_SLOTS)

        @pl.when(gather_next & jnp.logical_not(scatter_full))
        def _():
            ffn()
            start_full(copy_in, t_gather, t_gather % N_SLOTS)

        @pl.when(jnp.logical_not(gather_next))
        def _():
            ffn()

        @pl.when(t == nt - 1)
        def _():
            start_all(copy_out, t, slot)

        return carry

    lax.fori_loop(0, nt, tile, 0)
    for back in range(N_SLOTS, 0, -1):
        @pl.when(nt >= back)
        def _():
            wait_all(copy_out, nt - back, (nt - back) % N_SLOTS)


def _experts(xs, tables, wg, wu, wd, layer):
    n_win, _, d = xs.shape
    xc = xs.reshape(n_win * WIN_CHUNKS, CH, d)
    de = wg.shape[-1]
    anyspec = pl.BlockSpec(memory_space=pl.ANY)
    grid_spec = pltpu.PrefetchScalarGridSpec(
        num_scalar_prefetch=len(tables),
        grid=(1,),
        in_specs=[anyspec, anyspec, anyspec, anyspec],
        out_specs=anyspec,
        scratch_shapes=[
            pltpu.VMEM((N_SLOTS, TILE_ROWS, d), BF16),
            pltpu.VMEM((N_SLOTS, TILE_ROWS, d), BF16),
            pltpu.VMEM((2, d, de), F32),
            pltpu.VMEM((2, d, de), F32),
            pltpu.VMEM((2, de, d), F32),
            pltpu.VMEM((d, de), BF16),
            pltpu.VMEM((d, de), BF16),
            pltpu.VMEM((de, d), BF16),
            pltpu.SemaphoreType.DMA((N_SLOTS,)),
            pltpu.SemaphoreType.DMA((N_SLOTS,)),
            pltpu.SemaphoreType.DMA((2,)),
        ],
    )
    ys = pl.pallas_call(
        functools.partial(_experts_body, layer),
        grid_spec=grid_spec,
        out_shape=jax.ShapeDtypeStruct(xc.shape, xc.dtype),
        input_output_aliases={len(tables): 0},
        compiler_params=pltpu.CompilerParams(
            dimension_semantics=("arbitrary",), vmem_limit_bytes=VMEM_LIMIT),
        name="experts",
    )(*tables, xc, wg, wu, wd)
    return ys.reshape(xs.shape)


def _expert_tables(tab, t_max):
    i32 = jnp.int32
    nch = tab[:, 0, :N_EXPERTS]
    off = tab[:, 1, :N_EXPERTS]
    n_win = nch.shape[0]
    cum_incl = jnp.cumsum(nch, axis=0)
    cum_excl = cum_incl - nch
    ce = cum_incl[-1]
    te = (ce + TILE_CHUNKS - 1) // TILE_CHUNKS
    tile_end = jnp.cumsum(te)
    tile_start = tile_end - te
    nt = tile_end[-1]
    t_idx = jnp.arange(t_max, dtype=i32)
    tile_e = jnp.minimum(jnp.sum((tile_end[None, :] <= t_idx[:, None]).astype(i32), axis=1), N_EXPERTS - 1)
    sel_e = (tile_e[:, None] == jnp.arange(N_EXPERTS, dtype=i32)[None, :]).astype(i32)
    pick = lambda v: jnp.sum(sel_e * v[None, :], axis=1)
    q0 = (t_idx - pick(tile_start)) * TILE_CHUNKS
    nvalid = jnp.where(t_idx < nt, jnp.clip(pick(ce) - q0, 0, TILE_CHUNKS), 0).astype(i32)
    q = q0[:, None] + jnp.arange(TILE_CHUNKS, dtype=i32)[None, :]
    by_win = lambda m: jnp.sum(sel_e[:, :, None] * m.T[None, :, :], axis=1)
    cum_e = by_win(cum_incl)
    w = jnp.minimum(jnp.sum((cum_e[:, None, :] <= q[:, :, None]).astype(i32), axis=-1), n_win - 1)
    sel_w = (w[:, :, None] == jnp.arange(n_win, dtype=i32)[None, None, :]).astype(i32)
    shift = jnp.sum(sel_w * by_win(off - cum_excl)[:, None, :], axis=-1)
    cid = w * WIN_CHUNKS + shift + q
    valid = jnp.arange(TILE_CHUNKS, dtype=i32)[None, :] < nvalid[:, None]
    clist = jnp.where(valid, cid, cid[:, 0:1]).astype(i32).reshape(-1)
    prev_e = jnp.concatenate([jnp.full((1,), -1, i32), tile_e[:-1]])
    fresh = ((t_idx < nt) & (tile_e != prev_e)).astype(i32)
    wslot = (jnp.cumsum(fresh) - 1) % 2
    later_fresh = (t_idx[None, :] > t_idx[:, None]) & (fresh[None, :] == 1)
    nxt = jnp.min(jnp.where(later_fresh, t_idx[None, :], t_max), axis=1)
    next_e = jnp.where(nxt < t_max, jnp.sum((t_idx[None, :] == nxt[:, None]).astype(i32) * tile_e[None, :], axis=1), -1)
    return (tile_e.astype(i32), fresh, wslot.astype(i32), next_e.astype(i32), nvalid, clist,
            nt.astype(i32).reshape(1))


def _combine_body(alpha, h1_ref, info_ref, ys_ref, vecd_ref, out_ref):
    for wi in range(info_ref.shape[0]):
        rs = slice(wi * WIN, (wi + 1) * WIN)
        out_ref[rs, :] = _unsort_norm(h1_ref[rs, :], info_ref[wi], ys_ref[wi],
                                      vecd_ref[2:3, :], vecd_ref[3:4, :], alpha)


def _combine(h1, info, ys, vecd, layer, alpha, wpb, block0):
    m, d = h1.shape
    win_map = lambda i: (block0 + i, 0, 0)
    return pl.pallas_call(
        functools.partial(_combine_body, alpha),
        grid=(m // (wpb * WIN),),
        in_specs=[
            pl.BlockSpec((wpb * WIN, d), lambda i: (i, 0)),
            pl.BlockSpec((wpb, WIN, LANES), win_map),
            pl.BlockSpec((wpb, WIN_ROWS, d), win_map),
            _layer_spec(vecd, layer),
        ],
        out_specs=pl.BlockSpec((wpb * WIN, d), lambda i: (i, 0)),
        out_shape=jax.ShapeDtypeStruct((m, d), F32),
        compiler_params=pltpu.CompilerParams(
            dimension_semantics=("arbitrary",), vmem_limit_bytes=VMEM_LIMIT),
        name="combine",
    )(h1, info, ys, vecd)


def kernel(x_prompt, x_sample, state_conv_b, state_conv_c, ln_in_g, ln_in_b, w_in, w_s, b_s, ln_v_g, ln_v_b, conv_b_w, conv_b_bias, ln_conv_g, ln_conv_b, conv_c_w, w_out, ln1_g, ln1_b, w_router_group, w_router_expert, w_gate, w_up, w_down, ln2_g, ln2_b):
    depth = w_in.shape[0]
    nb, t, d = x_prompt.shape
    ns, nt_s, _ = x_sample.shape
    assert t % TM == 0 and (ns * nt_s) % WIN == 0 and TM % WIN == 0 and d % LANES == 0
    alpha = (2.0 * depth) ** 0.25
    n_win_p = nb * t // WIN
    n_win_s = ns * nt_s // WIN
    n_win = n_win_p + n_win_s
    assert n_win_p % n_win_s == 0 and n_win_s % (TM // WIN) == 0
    max_chunks = n_win * (2 * WIN // CH + N_EXPERTS)
    t_max = max_chunks // TILE_CHUNKS + N_EXPERTS

    wr = jnp.concatenate([w_router_group, w_router_expert.reshape(depth, d, N_EXPERTS)], axis=2)
    wr = jnp.pad(wr, ((0, 0), (0, 0), (0, LANES - wr.shape[2])))
    wr_hi = wr.astype(BF16)
    zeros_a = jnp.zeros((depth, A_WIDTH), F32)
    zeros_d = jnp.zeros((depth, d), F32)
    over_layers = lambda v: jnp.broadcast_to(v, (depth,) + v.shape)
    lw = dict(
        w_in=w_in.astype(BF16),
        ws2=w_s.reshape(depth, A_WIDTH // LANES, 2 * GMLP_CHUNK, GMLP_CHUNK),
        wexp=jnp.repeat(jnp.transpose(w_s[:, :, :nt_s, :nt_s], (0, 2, 3, 1)), HEAD_DIM, axis=-1),
        bs_exp=jnp.repeat(jnp.transpose(b_s, (0, 2, 1)), HEAD_DIM, axis=-1),
        veca=jnp.stack([ln_v_g, ln_v_b, conv_b_bias, ln_conv_g, ln_conv_b, zeros_a, zeros_a, zeros_a], axis=1),
        cbw=conv_b_w, ccw=conv_c_w,
        w_out=w_out.astype(BF16),
        vecd=jnp.stack([ln1_g, ln1_b, ln2_g, ln2_b, over_layers(ln_in_g), over_layers(ln_in_b),
                        zeros_d, zeros_d], axis=1),
        wr=jnp.concatenate([wr_hi, (wr - wr_hi.astype(F32)).astype(BF16)], axis=2),
    )
    ne = N_EXPERTS
    wg, wu, wd = (w_gate.reshape(depth * ne, d, -1), w_up.reshape(depth * ne, d, -1),
                  w_down.reshape(depth * ne, -1, d))

    hp = x_prompt
    hs = jnp.transpose(x_sample, (1, 0, 2)).reshape(nt_s * ns, d)
    sb_t = jnp.transpose(state_conv_b, (0, 2, 1, 3))
    sc_t = jnp.transpose(state_conv_c, (0, 2, 1, 3))
    cb_p, cc_p, cb_s, cc_s, v_s = [], [], [], [], []
    prev = ()
    for l in range(depth):
        h1s, vs, nbs, ncs, xs_s, info_s, tab_s = _mixer_sample(hs, sb_t, sc_t, lw, l, alpha)
        h1p, xs, info, tab, nbp, ncp = _mixer_prompt(hp, prev, lw, l, alpha, xs_s, info_s, tab_s)
        ys = _experts(xs, _expert_tables(tab, t_max), wg, wu, wd, l)
        hp, prev = h1p, (info, ys)
        hs = _combine(h1s, info, ys, lw["vecd"], l, alpha, n_win_s, n_win_p // n_win_s)
        cb_p.append(nbp)
        cc_p.append(ncp)
        cb_s.append(nbs)
        cc_s.append(ncs)
        v_s.append(vs)
    hp = _combine(hp.reshape(nb * t, d), *prev, lw["vecd"], depth - 1, alpha, TM // WIN, 0).reshape(nb, t, d)
    y_sample = jnp.transpose(hs.reshape(nt_s, ns, d), (1, 0, 2))
    untime = lambda xs_: jnp.transpose(jnp.stack(xs_), (0, 2, 1, 3))
    chunk_v = jnp.transpose(jnp.stack(v_s).reshape(depth, nt_s, ns, A_WIDTH), (0, 2, 1, 3))
    return (hp, y_sample, jnp.stack(cb_p), jnp.stack(cc_p), untime(cb_s), untime(cc_s), chunk_v)
```
